```python
import math, functools
import jax, jax.numpy as jnp
from jax import lax
import numpy as np

D_MODEL = 1024
BATCH = 4
SEQ = 4096
DEPTH = 2
DEC_BATCH = 32
DEC_SEQ = 1
PAST_LEN = 8192
PAGE_SIZE = 128

N_AB = (DEPTH + 1) // 2
N_C = DEPTH // 2

D_INNER = 2 * D_MODEL
SSM_HEAD_DIM = 64
SSM_HEADS = D_INNER // SSM_HEAD_DIM
SSM_GROUPS = 4
SSM_STATE = 128
SSM_CONV = 4
SSM_CHUNK = 128
CONV_DIM = D_INNER + 2 * SSM_GROUPS * SSM_STATE

NSA_HEADS = 16
NSA_KV = 2
NSA_HPG = NSA_HEADS // NSA_KV
NSA_HEAD_DIM = 64
CMP_LEN = 32
CMP_STRIDE = 16
CMP_HIDDEN = 128
SLC_LEN = 64
SLC_TOPN = 16
WINDOW = 512
Q_BLOCK = 128
ROPE_THETA = 10000.0
FORCE_SCORE = 1e4

D_CONF = D_MODEL
CONF_CONV = 31

D_FF = 2816
FFN_CONV = 3

EPS = 1e-6

PROJ_SPLITS = (D_INNER, CONV_DIM, SSM_HEADS, NSA_HEADS * NSA_HEAD_DIM,
               2 * NSA_KV * NSA_HEAD_DIM, 2 * NSA_KV * NSA_HEAD_DIM, 2 * NSA_KV * NSA_HEAD_DIM,
               3 * NSA_HEADS)
D_PROJ = sum(PROJ_SPLITS)
D_MIX0 = D_INNER + NSA_HEADS * NSA_HEAD_DIM

kernel_name = 'hybrid_ssd_nsa_conformer_convffn_step'


def rms_norm(x, g):
    xf = x.astype(jnp.float32)
    y = xf * lax.rsqrt(jnp.mean(xf * xf, axis=-1, keepdims=True) + EPS)
    return (y * g.astype(jnp.float32)).astype(x.dtype)


def layer_norm(x, g, b):
    xf = x.astype(jnp.float32)
    mu = jnp.mean(xf, axis=-1, keepdims=True)
    var = jnp.mean(jnp.square(xf - mu), axis=-1, keepdims=True)
    y = (xf - mu) * lax.rsqrt(var + EPS)
    return (y * g.astype(jnp.float32) + b.astype(jnp.float32)).astype(x.dtype)


def rope(x, pos):
    half = x.shape[-1] // 2
    inv = ROPE_THETA ** (-jnp.arange(half, dtype=jnp.float32) / half)
    ang = pos.astype(jnp.float32)[:, None] * inv[None, :]
    cos = jnp.cos(ang)[None, :, None, :]
    sin = jnp.sin(ang)[None, :, None, :]
    xf = x.astype(jnp.float32)
    x1, x2 = xf[..., :half], xf[..., half:]
    return jnp.concatenate([x1 * cos - x2 * sin, x2 * cos + x1 * sin], axis=-1).astype(x.dtype)


def masked_softmax(s, mask):
    s = jnp.where(mask, s.astype(jnp.float32), -jnp.inf)
    m = jnp.max(s, axis=-1, keepdims=True)
    m = jnp.where(jnp.isfinite(m), m, 0.0)
    e = jnp.exp(s - m)
    d = jnp.sum(e, axis=-1, keepdims=True)
    return e / jnp.where(d > 0, d, 1.0)


def causal_dwconv(xh, w, b):
    c = w.shape[-1]
    y = lax.conv_general_dilated(xh, w[:, None, :].astype(xh.dtype), window_strides=(1,),
                                 padding='VALID', dimension_numbers=('NWC', 'WIO', 'NWC'),
                                 feature_group_count=c)
    return y + b.astype(xh.dtype)


def ssd_scan(x, dt, a, bm, cm, h0, chunk):
    f32 = jnp.float32
    bsz, l, h, p = x.shape
    g, n = bm.shape[2], bm.shape[3]
    hg = h // g
    q = min(chunk, l)
    pad = (-l) % q
    x, dt, bm, cm = (v.astype(f32) for v in (x, dt, bm, cm))
    if pad:
        padt = lambda v: jnp.pad(v, [(0, 0), (0, pad)] + [(0, 0)] * (v.ndim - 2))
        x, dt, bm, cm = padt(x), padt(dt), padt(bm), padt(cm)
    c = (l + pad) // q
    x = x.reshape(bsz, c, q, g, hg, p)
    dt = dt.reshape(bsz, c, q, g, hg)
    bm = bm.reshape(bsz, c, q, g, n)
    cm = cm.reshape(bsz, c, q, g, n)
    cum = jnp.cumsum(dt * a.astype(f32).reshape(g, hg), axis=2)
    seg = cum[:, :, :, None] - cum[:, :, None, :]
    causal = jnp.tril(jnp.ones((q, q), bool))[:, :, None, None]
    decay = jnp.exp(jnp.where(causal, seg, -jnp.inf))
    cb = jnp.einsum('bcign,bcjgn->bcijg', cm, bm)
    w = cb[..., None] * decay * dt[:, :, None]
    y_diag = jnp.einsum('bcijgh,bcjghp->bcighp', w, x)
    to_end = jnp.exp(cum[:, :, -1:] - cum) * dt
    states = jnp.einsum('bcjgn,bcjgh,bcjghp->bcghpn', bm, to_end, x)
    chunk_decay = jnp.exp(cum[:, :, -1])

    def step(hc, inp):
        st, dec = inp
        return dec[..., None, None] * hc + st, hc

    h_T, h_start = lax.scan(step, h0.astype(f32).reshape(bsz, g, hg, p, n),
                            (jnp.moveaxis(states, 1, 0), jnp.moveaxis(chunk_decay, 1, 0)))
    h_start = jnp.moveaxis(h_start, 0, 1)
    y_off = jnp.einsum('bcign,bcghpn->bcighp', cm, h_start) * jnp.exp(cum)[..., None]
    y = (y_diag + y_off).reshape(bsz, c * q, h, p)[:, :l]
    return y, h_T.reshape(bsz, h, p, n)


def mamba_branch(z, xbc, dt_raw, conv_hist, h0, conv_w, conv_b, dt_bias, a_log, d_skip, ssm_norm):
    bsz, t = z.shape[:2]
    xh = jnp.concatenate([conv_hist.astype(xbc.dtype), xbc], axis=1)
    new_hist = xh[:, xh.shape[1] - (SSM_CONV - 1):]
    xbc = jax.nn.silu(causal_dwconv(xh, conv_w, conv_b))
    xs, bm, cm = jnp.split(xbc, [D_INNER, D_INNER + SSM_GROUPS * SSM_STATE], axis=-1)
    xs = xs.reshape(bsz, t, SSM_HEADS, SSM_HEAD_DIM)
    bm = bm.reshape(bsz, t, SSM_GROUPS, SSM_STATE)
    cm = cm.reshape(bsz, t, SSM_GROUPS, SSM_STATE)
    dt = jax.nn.softplus(dt_raw.astype(jnp.float32) + dt_bias.astype(jnp.float32))
    a = -jnp.exp(a_log.astype(jnp.float32))
    y, h_T = ssd_scan(xs, dt, a, bm, cm, h0, SSM_CHUNK)
    y = y.astype(xs.dtype) + d_skip[:, None] * xs
    y = y.reshape(bsz, t, D_INNER) * jax.nn.silu(z)
    y = rms_norm(y.reshape(bsz, t, SSM_GROUPS, D_INNER // SSM_GROUPS),
                 ssm_norm.reshape(SSM_GROUPS, D_INNER // SSM_GROUPS)).reshape(bsz, t, D_INNER)
    return y, new_hist, h_T.astype(h0.dtype)


def compress_rows(rows, pe, w1, b1, w2, b2):
    bsz, t = rows.shape[:2]
    r = CMP_LEN // CMP_STRIDE
    nch = t // CMP_STRIDE
    n_cmp = nch - r + 1
    ch = rows[:, :nch * CMP_STRIDE].reshape(bsz, nch, CMP_STRIDE, NSA_KV, NSA_HEAD_DIM)
    blk = jnp.concatenate([ch[:, k:k + n_cmp] for k in range(r)], axis=2)
    blk = blk + pe[None, None, :, None, :]
    flat = blk.transpose(0, 1, 3, 2, 4).reshape(bsz, n_cmp, NSA_KV, CMP_LEN * NSA_HEAD_DIM)
    hid = jax.nn.silu(flat @ w1 + b1)
    return hid @ w2 + b2


def compress_kv(kv, cmp_w):
    pe, w1, b1, w2, b2 = cmp_w
    kc = compress_rows(kv[:, :, 0], pe[0], w1[0], b1[0], w2[0], b2[0])
    vc = compress_rows(kv[:, :, 1], pe[1], w1[1], b1[1], w2[1], b2[1])
    cmp_end = jnp.arange(kc.shape[1]) * CMP_STRIDE + CMP_LEN - 1
    return kc, vc, cmp_end


def slc_blocks_of(kv):
    bsz, t = kv.shape[:2]
    n_slc = -(-t // SLC_LEN)
    kv = jnp.pad(kv, [(0, 0), (0, n_slc * SLC_LEN - t), (0, 0), (0, 0), (0, 0)])
    return kv.reshape(bsz, n_slc, SLC_LEN, 2, NSA_KV, NSA_HEAD_DIM).transpose(0, 4, 1, 2, 3, 5)


def overlap_matrix(n_cmp, n_slc):
    i = np.arange(n_cmp)[:, None]
    j = np.arange(n_slc)[None, :]
    ov = (i * CMP_STRIDE < (j + 1) * SLC_LEN) & (i * CMP_STRIDE + CMP_LEN > j * SLC_LEN)
    return jnp.asarray(ov.astype(np.float32))


def nsa_core(q, qpos, gates, kc, vc, cmp_end, slc_blocks, win_kv, win_pos):
    f32 = jnp.float32
    bsz, tq = q.shape[:2]
    qg = (q * NSA_HEAD_DIM ** -0.5).reshape(bsz, tq, NSA_KV, NSA_HPG, NSA_HEAD_DIM)
    s_c = jnp.einsum('btghd,bngd->btghn', qg, kc, preferred_element_type=f32)
    m_c = (cmp_end[None, :] <= qpos[:, None])[None, :, None, None, :]
    p_c = masked_softmax(s_c, m_c)
    o_c = jnp.einsum('btghn,bngd->btghd', p_c, vc.astype(f32))
    n_slc = slc_blocks.shape[2]
    imp = jnp.einsum('btghn,nj->btgj', p_c, overlap_matrix(kc.shape[1], n_slc))
    j = jnp.arange(n_slc)[None, :]
    qblk = (qpos // SLC_LEN)[:, None]
    valid = j * SLC_LEN <= qpos[:, None]
    forced = (j == 0) | (j == qblk) | (j == qblk - 1)
    score = jnp.where(valid[None, :, None, :], imp + FORCE_SCORE * forced[None, :, None, :], -jnp.inf)
    topn = min(SLC_TOPN, n_slc)
    _, idx = lax.top_k(score, topn)
    idx_t = idx.transpose(0, 2, 1, 3)
    bi = jnp.arange(bsz)[:, None, None, None]
    gi = jnp.arange(NSA_KV)[None, :, None, None]
    sel = slc_blocks[bi, gi, idx_t]
    s_s = jnp.einsum('btghd,bgtnld->btghnl', qg, sel[..., 0, :], preferred_element_type=f32)
    kpos = idx_t[..., None] * SLC_LEN + jnp.arange(SLC_LEN)
    m_s = (kpos <= qpos[None, None, :, None, None]).transpose(0, 2, 1, 3, 4)[:, :, :, None]
    p_s = masked_softmax(s_s.reshape(bsz, tq, NSA_KV, NSA_HPG, topn * SLC_LEN),
                         m_s.reshape(bsz, tq, NSA_KV, 1, topn * SLC_LEN)).reshape(s_s.shape)
    o_s = jnp.einsum('btghnl,bgtnld->btghd', p_s, sel[..., 1, :].astype(f32))
    s_w = jnp.einsum('btghd,bsgd->btghs', qg, win_kv[:, :, 0], preferred_element_type=f32)
    dpos = qpos[:, None] - win_pos[None, :]
    m_w = ((dpos >= 0) & (dpos < WINDOW) & (win_pos[None, :] >= 0))[None, :, None, None, :]
    p_w = masked_softmax(s_w, m_w)
    o_w = jnp.einsum('btghs,bsgd->btghd', p_w, win_kv[:, :, 1].astype(f32))
    g = jax.nn.sigmoid(gates.astype(f32)).reshape(bsz, tq, NSA_KV, NSA_HPG, 3)
    o = g[..., 0:1] * o_c + g[..., 1:2] * o_s + g[..., 2:3] * o_w
    return o.reshape(bsz, tq, NSA_HEADS * NSA_HEAD_DIM).astype(q.dtype)


def nsa_prompt(q, gates, kvc, kvs, kvw, cmp_w):
    bsz, s = q.shape[:2]
    kc, vc, cmp_end = compress_kv(kvc, cmp_w)
    slc_blocks = slc_blocks_of(kvs)
    win_pad = jnp.concatenate([jnp.zeros((bsz, WINDOW) + kvw.shape[2:], kvw.dtype), kvw], axis=1)
    nb = s // Q_BLOCK
    qb = q.reshape(bsz, nb, Q_BLOCK, NSA_HEADS, NSA_HEAD_DIM).swapaxes(0, 1)
    gb = gates.reshape(bsz, nb, Q_BLOCK, NSA_HEADS, 3).swapaxes(0, 1)

    def one_block(args):
        qi, gi, i = args
        qpos = i * Q_BLOCK + jnp.arange(Q_BLOCK)
        wkv = lax.dynamic_slice_in_dim(win_pad, i * Q_BLOCK, Q_BLOCK + WINDOW, axis=1)
        wpos = i * Q_BLOCK - WINDOW + jnp.arange(Q_BLOCK + WINDOW)
        return nsa_core(qi, qpos, gi, kc, vc, cmp_end, slc_blocks, wkv, wpos)

    out = lax.map(one_block, (qb, gb, jnp.arange(nb)))
    return out.swapaxes(0, 1).reshape(bsz, s, NSA_HEADS * NSA_HEAD_DIM)


def nsa_sample(q, gates, kvc, kvs, kvw, cache_cmp, cache_slc, cache_win, page_table, cmp_w):
    db, t = q.shape[:2]
    past_len = page_table.shape[1] * cache_cmp.shape[1]

    def gather(cache):
        return cache[page_table].reshape((db, past_len) + cache.shape[2:])

    rows_cmp = jnp.concatenate([gather(cache_cmp).astype(kvc.dtype), kvc], axis=1)
    rows_slc = jnp.concatenate([gather(cache_slc).astype(kvs.dtype), kvs], axis=1)
    kc, vc, cmp_end = compress_kv(rows_cmp, cmp_w)
    slc_blocks = slc_blocks_of(rows_slc)
    wkv = jnp.concatenate([cache_win.astype(kvw.dtype), kvw], axis=1)
    w_rows = cache_win.shape[1]
    wpos = past_len - w_rows + jnp.arange(w_rows + t)
    qpos = past_len + jnp.arange(t)
    return nsa_core(q, qpos, gates, kc, vc, cmp_end, slc_blocks, wkv, wpos)


def ab_project(xn, pos, w_in):
    bsz, t = xn.shape[:2]
    proj = xn @ w_in
    z, xbc, dt_raw, q, kvc, kvs, kvw, gates = jnp.split(
        proj, [int(v) for v in np.cumsum(PROJ_SPLITS)[:-1]], axis=-1)
    q = rope(q.reshape(bsz, t, NSA_HEADS, NSA_HEAD_DIM), pos)

    def kv_form(u):
        u = u.reshape(bsz, t, 2, NSA_KV, NSA_HEAD_DIM)
        return jnp.stack([rope(u[:, :, 0], pos), u[:, :, 1]], axis=2)

    return z, xbc, dt_raw, q, kv_form(kvc), kv_form(kvs), kv_form(kvw), gates.reshape(bsz, t, NSA_HEADS, 3)


def mixer_ab(xn, pos, conv_hist, h0, nsa_fn, w_in, ssm_w, w_out):
    z, xbc, dt_raw, q, kvc, kvs, kvw, gates = ab_project(xn, pos, w_in)
    y_ssm, new_hist, h_T = mamba_branch(z, xbc, dt_raw, conv_hist, h0, *ssm_w)
    y_nsa = nsa_fn(q, gates, kvc, kvs, kvw)
    out = jnp.concatenate([y_ssm, y_nsa.astype(y_ssm.dtype)], axis=-1) @ w_out
    return out, new_hist, h_T, kvc, kvs, kvw


def conformer_conv(xn, hist, w_pw1, b_pw1, dw_w, dw_b, ln_g, ln_b, w_pw2, b_pw2):
    u = xn @ w_pw1 + b_pw1
    a, gt = jnp.split(u, 2, axis=-1)
    a = a * jax.nn.sigmoid(gt)
    ah = jnp.concatenate([hist.astype(a.dtype), a], axis=1)
    new_hist = ah[:, ah.shape[1] - (CONF_CONV - 1):]
    c = layer_norm(causal_dwconv(ah, dw_w, dw_b), ln_g, ln_b)
    return jax.nn.silu(c) @ w_pw2 + b_pw2, new_hist


def conv_ffn(xn, hist, w_up, conv_w, conv_b, w_down):
    u = xn @ w_up
    uh = jnp.concatenate([hist.astype(u.dtype), u], axis=1)
    new_hist = uh[:, uh.shape[1] - (FFN_CONV - 1):]
    gate, val = jnp.split(causal_dwconv(uh, conv_w, conv_b), 2, axis=-1)
    return (jax.nn.gelu(gate, approximate=True) * val) @ w_down, new_hist


def setup_inputs(seed: int = 0) -> dict:
    key = jax.random.key(seed)
    ks = iter(jax.random.split(key, 64))
    nrm = lambda shape, scale: jax.random.normal(next(ks), shape, jnp.float32) * scale
    n_pages = PAST_LEN // PAGE_SIZE
    n_phys = (DEC_BATCH * n_pages * 5) // 4
    w_rows = min(WINDOW, PAST_LEN)
    page_table = jax.random.permutation(next(ks), n_phys)[:DEC_BATCH * n_pages].reshape(
        DEC_BATCH, n_pages).astype(jnp.int32)
    dt0 = jnp.exp(jax.random.uniform(next(ks), (N_AB, SSM_HEADS), jnp.float32,
                                     math.log(1e-3), math.log(1e-1)))
    dt_bias = dt0 + jnp.log(-jnp.expm1(-dt0))
    a_log = jnp.log(jax.random.uniform(next(ks), (N_AB, SSM_HEADS), jnp.float32, 1.0, 16.0))
    kvshape = (2, NSA_KV, NSA_HEAD_DIM)
    return {
        'x_prompt': nrm((BATCH, SEQ, D_MODEL), 1.0),
        'x_sample': nrm((DEC_BATCH, DEC_SEQ, D_MODEL), 1.0),
        'state_ssm': nrm((N_AB, DEC_BATCH, SSM_HEADS, SSM_HEAD_DIM, SSM_STATE), 0.3),
        'state_ssm_conv': nrm((N_AB, DEC_BATCH, SSM_CONV - 1, CONV_DIM), 1.0),
        'cache_cmp': nrm((N_AB, n_phys, PAGE_SIZE) + kvshape, 1.0),
        'cache_slc': nrm((N_AB, n_phys, PAGE_SIZE) + kvshape, 1.0),
        'cache_win': nrm((N_AB, DEC_BATCH, w_rows) + kvshape, 1.0),
        'state_conf_conv': nrm((N_C, DEC_BATCH, CONF_CONV - 1, D_CONF), 0.5),
        'state_ffn_conv': nrm((DEPTH, DEC_BATCH, FFN_CONV - 1, 2 * D_FF), 1.0),
        'page_table': page_table,
        'norm_g': 1.0 + nrm((DEPTH, 4, D_MODEL), 0.01),
        'ab_w_in': nrm((N_AB, D_MODEL, D_PROJ), D_MODEL ** -0.5),
        'ab_conv_w': nrm((N_AB, SSM_CONV, CONV_DIM), SSM_CONV ** -0.5),
        'ab_conv_b': nrm((N_AB, CONV_DIM), 0.01),
        'ab_dt_bias': dt_bias,
        'ab_a_log': a_log,
        'ab_d_skip': 1.0 + nrm((N_AB, SSM_HEADS), 0.01),
        'ab_ssm_norm': 1.0 + nrm((N_AB, D_INNER), 0.01),
        'ab_cmp_pe': nrm((N_AB, 2, CMP_LEN, NSA_HEAD_DIM), 0.02),
        'ab_cmp_w1': nrm((N_AB, 2, CMP_LEN * NSA_HEAD_DIM, CMP_HIDDEN), (CMP_LEN * NSA_HEAD_DIM) ** -0.5),
        'ab_cmp_b1': nrm((N_AB, 2, CMP_HIDDEN), 0.01),
        'ab_cmp_w2': nrm((N_AB, 2, CMP_HIDDEN, NSA_HEAD_DIM), CMP_HIDDEN ** -0.5),
        'ab_cmp_b2': nrm((N_AB, 2, NSA_HEAD_DIM), 0.01),
        'ab_w_out': nrm((N_AB, D_MIX0, D_MODEL), D_MIX0 ** -0.5),
        'c_w_pw1': nrm((N_C, D_MODEL, 2 * D_CONF), D_MODEL ** -0.5),
        'c_b_pw1': nrm((N_C, 2 * D_CONF), 0.01),
        'c_dw_w': nrm((N_C, CONF_CONV, D_CONF), CONF_CONV ** -0.5),
        'c_dw_b': nrm((N_C, D_CONF), 0.01),
        'c_ln_g': 1.0 + nrm((N_C, D_CONF), 0.01),
        'c_ln_b': nrm((N_C, D_CONF), 0.01),
        'c_w_pw2': nrm((N_C, D_CONF, D_MODEL), D_CONF ** -0.5),
        'c_b_pw2': nrm((N_C, D_MODEL), 0.01),
        'ffn_w_up': nrm((DEPTH, D_MODEL, 2 * D_FF), D_MODEL ** -0.5),
        'ffn_conv_w': nrm((DEPTH, FFN_CONV, 2 * D_FF), FFN_CONV ** -0.5),
        'ffn_conv_b': nrm((DEPTH, 2 * D_FF), 0.01),
        'ffn_w_down': nrm((DEPTH, D_FF, D_MODEL), D_FF ** -0.5),
    }


def reference(x_prompt, x_sample, state_ssm, state_ssm_conv, cache_cmp, cache_slc, cache_win,
              state_conf_conv, state_ffn_conv, page_table, norm_g, ab_w_in, ab_conv_w, ab_conv_b,
              ab_dt_bias, ab_a_log, ab_d_skip, ab_ssm_norm, ab_cmp_pe, ab_cmp_w1, ab_cmp_b1,
              ab_cmp_w2, ab_cmp_b2, ab_w_out, c_w_pw1, c_b_pw1, c_dw_w, c_dw_b, c_ln_g, c_ln_b,
              c_w_pw2, c_b_pw2, ffn_w_up, ffn_conv_w, ffn_conv_b, ffn_w_down):
    bsz, s = x_prompt.shape[:2]
    db, t = x_sample.shape[:2]
    past_len = page_table.shape[1] * cache_cmp.shape[2]
    pos_p = jnp.arange(s, dtype=jnp.int32)
    pos_s = past_len + jnp.arange(t, dtype=jnp.int32)
    hp, hs = x_prompt, x_sample
    (ssm_p, ssm_s, sconv_p, sconv_s, cmp_p, cmp_s, slc_p, slc_s,
     win_p, win_s, conf_p, conf_s, ffn_p, ffn_s) = ([] for _ in range(14))
    for layer in range(DEPTH):
        g = norm_g[layer]
        if layer % 2 == 0:
            a = layer // 2
            cmp_w = (ab_cmp_pe[a], ab_cmp_w1[a], ab_cmp_b1[a], ab_cmp_w2[a], ab_cmp_b2[a])
            ssm_w = (ab_conv_w[a], ab_conv_b[a], ab_dt_bias[a], ab_a_log[a], ab_d_skip[a], ab_ssm_norm[a])
            mix, hist, h_T, kvc, kvs, kvw = mixer_ab(
                rms_norm(hp, g[0]), pos_p,
                jnp.zeros((bsz, SSM_CONV - 1, CONV_DIM), hp.dtype),
                jnp.zeros((bsz, SSM_HEADS, SSM_HEAD_DIM, SSM_STATE), hp.dtype),
                functools.partial(nsa_prompt, cmp_w=cmp_w), ab_w_in[a], ssm_w, ab_w_out[a])
            hp = hp + rms_norm(mix, g[1])
            ssm_p.append(h_T); sconv_p.append(hist); cmp_p.append(kvc); slc_p.append(kvs)
            win_p.append(kvw[:, s - min(WINDOW, s):])
            mix, hist, h_T, kvc, kvs, kvw = mixer_ab(
                rms_norm(hs, g[0]), pos_s, state_ssm_conv[a], state_ssm[a],
                functools.partial(nsa_sample, cache_cmp=cache_cmp[a], cache_slc=cache_slc[a],
                                  cache_win=cache_win[a], page_table=page_table, cmp_w=cmp_w),
                ab_w_in[a], ssm_w, ab_w_out[a])
            hs = hs + rms_norm(mix, g[1])
            ssm_s.append(h_T); sconv_s.append(hist); cmp_s.append(kvc); slc_s.append(kvs); win_s.append(kvw)
        else:
            c = layer // 2
            cw = (c_w_pw1[c], c_b_pw1[c], c_dw_w[c], c_dw_b[c], c_ln_g[c], c_ln_b[c], c_w_pw2[c], c_b_pw2[c])
            mix, hist = conformer_conv(rms_norm(hp, g[0]),
                                       jnp.zeros((bsz, CONF_CONV - 1, D_CONF), hp.dtype), *cw)
            hp = hp + rms_norm(mix, g[1])
            conf_p.append(hist)
            mix, hist = conformer_conv(rms_norm(hs, g[0]), state_conf_conv[c], *cw)
            hs = hs + rms_norm(mix, g[1])
            conf_s.append(hist)
        fw = (ffn_w_up[layer], ffn_conv_w[layer], ffn_conv_b[layer], ffn_w_down[layer])
        f, hist = conv_ffn(rms_norm(hp, g[2]), jnp.zeros((bsz, FFN_CONV - 1, 2 * D_FF), hp.dtype), *fw)
        hp = hp + rms_norm(f, g[3])
        ffn_p.append(hist)
        f, hist = conv_ffn(rms_norm(hs, g[2]), state_ffn_conv[layer], *fw)
        hs = hs + rms_norm(f, g[3])
        ffn_s.append(hist)
    return (hp, hs, jnp.stack(ssm_p), jnp.stack(ssm_s), jnp.stack(sconv_p), jnp.stack(sconv_s),
            jnp.stack(cmp_p), jnp.stack(cmp_s), jnp.stack(slc_p), jnp.stack(slc_s),
            jnp.stack(win_p), jnp.stack(win_s), jnp.stack(conf_p), jnp.stack(conf_s),
            jnp.stack(ffn_p), jnp.stack(ffn_s))
```

```python
import functools
import math

import jax
import jax.numpy as jnp
import numpy as np
from jax import lax
from jax.experimental import pallas as pl
from jax.experimental.pallas import tpu as pltpu

F32 = jnp.float32
BF16 = jnp.bfloat16

EPS = 1e-6
LANES = 128
VMEM_LIMIT = 48 * 1024 * 1024

SSM_HEAD_DIM = 64
SSM_GROUPS = 4
SSM_STATE = 128
SSM_CONV = 4
SSM_CHUNK = 128
NSA_HEADS = 16
NSA_KV = 2
NSA_HPG = NSA_HEADS // NSA_KV
NSA_HEAD_DIM = 64
CMP_LEN = 32
CMP_STRIDE = 16
SLC_LEN = 64
SLC_TOPN = 16
WINDOW = 512
Q_BLOCK = 128
ROPE_THETA = 10000.0
FORCE_SCORE = 1e4
CONF_CONV = 31
FFN_CONV = 3


def _params(*sem):
    return pltpu.CompilerParams(dimension_semantics=sem, vmem_limit_bytes=VMEM_LIMIT)


def _split3(v):
    hi = v.astype(BF16)
    r1 = v - hi.astype(F32)
    mid = r1.astype(BF16)
    lo = (r1 - mid.astype(F32)).astype(BF16)
    return hi, mid, lo


def _dot(a, b):
    return jnp.dot(a, b, preferred_element_type=F32)


def _dot_nt(a, b):
    return lax.dot_general(a, b, (((1,), (1,)), ((), ())), preferred_element_type=F32)


def _dot3(v, m_bf16):
    hi, mid, lo = _split3(v)
    return _dot(hi, m_bf16) + _dot(mid, m_bf16) + _dot(lo, m_bf16)


def _dot3_left(m_bf16, v):
    hi, mid, lo = _split3(v)
    return _dot(m_bf16, hi) + _dot(m_bf16, mid) + _dot(m_bf16, lo)


def _norm_matmul_body(x_ref, g_ref, w_ref, b_ref, o_ref, xn_ref):
    @pl.when(pl.program_id(1) == 0)
    def _():
        x = x_ref[...]
        ms = jnp.mean(x * x, axis=-1, keepdims=True)
        xn_ref[...] = (x * lax.rsqrt(ms + EPS) * g_ref[...]).astype(BF16)

    o_ref[...] = _dot(xn_ref[...], w_ref[...]) + b_ref[...]


def norm_matmul(x, g, w, b, tm, tn):
    m, k = x.shape
    n = w.shape[1]
    return pl.pallas_call(
        _norm_matmul_body,
        grid=(m // tm, n // tn),
        in_specs=[pl.BlockSpec((tm, k), lambda i, j: (i, 0)),
                  pl.BlockSpec((1, k), lambda i, j: (0, 0)),
                  pl.BlockSpec((k, tn), lambda i, j: (0, j)),
                  pl.BlockSpec((1, tn), lambda i, j: (0, j))],
        out_specs=pl.BlockSpec((tm, tn), lambda i, j: (i, j)),
        out_shape=jax.ShapeDtypeStruct((m, n), F32),
        scratch_shapes=[pltpu.VMEM((tm, k), BF16)],
        compiler_params=_params("parallel", "arbitrary"),
        name="norm_matmul",
    )(x, g.reshape(1, k), w, b.reshape(1, n))


def _matmul_post_body(n_pairs, *refs):
    a_refs = refs[:n_pairs]
    w_refs = refs[n_pairs:2 * n_pairs]
    b_ref, r_ref, g_ref, o_ref = refs[2 * n_pairs:]
    acc = _dot(a_refs[0][...], w_refs[0][...])
    for a_ref, w_ref in zip(a_refs[1:], w_refs[1:]):
        acc = acc + _dot(a_ref[...], w_ref[...])
    acc = acc + b_ref[...]
    ms = jnp.mean(acc * acc, axis=-1, keepdims=True)
    o_ref[...] = r_ref[...] + acc * lax.rsqrt(ms + EPS) * g_ref[...]


def matmul_post(a_list, w_list, b, resid, g, tm):
    m, n = resid.shape
    n_pairs = len(a_list)
    in_specs = [pl.BlockSpec((tm, a.shape[1]), lambda i: (i, 0)) for a in a_list]
    in_specs += [pl.BlockSpec(w.shape, lambda i: (0, 0)) for w in w_list]
    in_specs += [pl.BlockSpec((1, n), lambda i: (0, 0)),
                 pl.BlockSpec((tm, n), lambda i: (i, 0)),
                 pl.BlockSpec((1, n), lambda i: (0, 0))]
    return pl.pallas_call(
        functools.partial(_matmul_post_body, n_pairs),
        grid=(m // tm,),
        in_specs=in_specs,
        out_specs=pl.BlockSpec((tm, n), lambda i: (i, 0)),
        out_shape=jax.ShapeDtypeStruct((m, n), F32),
        compiler_params=_params("parallel"),
        name="matmul_post",
    )(*a_list, *w_list, b.reshape(1, n), resid, g.reshape(1, n))


def _rope_tables(pos):
    half = NSA_HEAD_DIM // 2
    inv = ROPE_THETA ** (-jnp.arange(half, dtype=F32) / half)
    ang = pos.astype(F32)[:, None] * inv[None, :]
    cos, sin = jnp.cos(ang), jnp.sin(ang)
    reps = LANES // NSA_HEAD_DIM
    cosf = jnp.tile(jnp.concatenate([cos, cos], axis=-1), (1, reps))
    sinf = jnp.tile(jnp.concatenate([-sin, sin], axis=-1), (1, reps))
    return cosf, sinf


def _rotate_half_partner(x):
    w = x.shape[-1]
    lane = lax.broadcasted_iota(jnp.int32, x.shape, x.ndim - 1)
    first = (lane % NSA_HEAD_DIM) < (NSA_HEAD_DIM // 2)
    return jnp.where(first, pltpu.roll(x, w - NSA_HEAD_DIM // 2, x.ndim - 1),
                     pltpu.roll(x, NSA_HEAD_DIM // 2, x.ndim - 1))


def _swap_head_pair(x):
    w = x.shape[-1]
    lane = lax.broadcasted_iota(jnp.int32, x.shape, x.ndim - 1)
    first = (lane % LANES) < NSA_HEAD_DIM
    return jnp.where(first, pltpu.roll(x, w - NSA_HEAD_DIM, x.ndim - 1),
                     pltpu.roll(x, NSA_HEAD_DIM, x.ndim - 1))


def _rope_body(q_ref, kv_ref, cos_ref, sin_ref, q_o, qs_o, kv_o, kvc_o, kvs_o, kvw_o):
    cos = cos_ref[...]
    sin = sin_ref[...]
    q = q_ref[...]
    nq = q.shape[1] // LANES
    cq = jnp.concatenate([cos] * nq, axis=1)
    sq = jnp.concatenate([sin] * nq, axis=1)
    qr = (q * cq + _rotate_half_partner(q) * sq) * (NSA_HEAD_DIM ** -0.5)
    q_o[...] = qr.astype(BF16)
    qs_o[...] = _swap_head_pair(qr).astype(BF16)

    kv = kv_ref[...]
    lane = lax.broadcasted_iota(jnp.int32, (kv.shape[0], LANES), 1)
    lo = lane < NSA_HEAD_DIM
    zero = jnp.zeros((kv.shape[0], LANES), F32)
    padded = []
    for c, full_o in enumerate((kvc_o, kvs_o, kvw_o)):
        k = kv[:, c * 256:c * 256 + LANES]
        v = kv[:, c * 256 + LANES:(c + 1) * 256]
        kr = k * cos + _rotate_half_partner(k) * sin
        full_o[...] = jnp.concatenate([kr, v], axis=1)
        if c > 0:
            kr_sw = pltpu.roll(kr, NSA_HEAD_DIM, 1)
            v_sw = pltpu.roll(v, NSA_HEAD_DIM, 1)
            padded += [jnp.where(lo, kr, zero), jnp.where(lo, kr_sw, zero),
                       jnp.where(lo, v, zero), jnp.where(lo, v_sw, zero)]
    kv_o[...] = jnp.concatenate(padded, axis=1).astype(BF16)


def rope_call(proj, cosf, sinf, tm):
    m = proj.shape[0]
    nblk = cosf.shape[0] // tm
    f32s = jax.ShapeDtypeStruct((m, 256), F32)
    return pl.pallas_call(
        _rope_body,
        grid=(m // tm,),
        in_specs=[pl.BlockSpec((tm, 1024), lambda i: (i, 2)),
                  pl.BlockSpec((tm, 768), lambda i: (i, 8)),
                  pl.BlockSpec((tm, LANES), lambda i: (i % nblk, 0)),
                  pl.BlockSpec((tm, LANES), lambda i: (i % nblk, 0))],
        out_specs=[pl.BlockSpec((tm, 1024), lambda i: (i, 0)),
                   pl.BlockSpec((tm, 1024), lambda i: (i, 0)),
                   pl.BlockSpec((tm, 1024), lambda i: (i, 0)),
                   pl.BlockSpec((tm, 256), lambda i: (i, 0)),
                   pl.BlockSpec((tm, 256), lambda i: (i, 0)),
                   pl.BlockSpec((tm, 256), lambda i: (i, 0))],
        out_shape=[jax.ShapeDtypeStruct((m, 1024), BF16), jax.ShapeDtypeStruct((m, 1024), BF16),
                   jax.ShapeDtypeStruct((m, 1024), BF16), f32s, f32s, f32s],
        compiler_params=_params("parallel"),
        name="rope",
    )(proj, proj, cosf, sinf)


def _compress_body(ch_ref, pe_ref, w1_ref, b1_ref, w2_ref, b2_ref, o_ref):
    ch = ch_ref[0, 0]
    nc, half = ch.shape
    pe = pe_ref[0]
    top = _dot((ch + pe[:, :half]).astype(BF16), w1_ref[0, :half, :])
    bot = _dot((ch + pe[:, half:]).astype(BF16), w1_ref[0, half:, :])
    hid = top + pltpu.roll(bot, nc - 1, 0) + b1_ref[0]
    hid = hid * jax.nn.sigmoid(hid)
    o_ref[0, 0] = _dot(hid.astype(BF16), w2_ref[0]) + b2_ref[0]


def compress_call(ch, pe_flat, w1, b1, w2p, b2p):
    bsz, _, nc, half = ch.shape
    hidden = w1.shape[-1]
    return pl.pallas_call(
        _compress_body,
        grid=(bsz, 4),
        in_specs=[pl.BlockSpec((1, 1, nc, half), lambda b, q: (b, q, 0, 0)),
                  pl.BlockSpec((1, 1, 2 * half), lambda b, q: (q // 2, 0, 0)),
                  pl.BlockSpec((1, 2 * half, hidden), lambda b, q: (q // 2, 0, 0)),
                  pl.BlockSpec((1, 1, hidden), lambda b, q: (q // 2, 0, 0)),
                  pl.BlockSpec((1, hidden, LANES), lambda b, q: (q // 2, 0, 0)),
                  pl.BlockSpec((1, 1, LANES), lambda b, q: (q // 2, 0, 0))],
        out_specs=pl.BlockSpec((1, 1, nc, LANES), lambda b, q: (b, q, 0, 0)),
        out_shape=jax.ShapeDtypeStruct((bsz, 4, nc, LANES), F32),
        compiler_params=_params("parallel", "parallel"),
        name="compress",
    )(ch, pe_flat, w1, b1, w2p, b2p)


def _overlap_np(nc, ns):
    i = np.arange(nc)[:, None]
    j = np.arange(ns)[None, :]
    return ((i * CMP_STRIDE < (j + 1) * SLC_LEN) & (i * CMP_STRIDE + CMP_LEN > j * SLC_LEN)).astype(np.float32)


def _gate_select_np():
    sel = np.zeros((3, LANES, NSA_HEADS * NSA_HEAD_DIM), np.float32)
    for k in range(3):
        for h in range(NSA_HEADS):
            sel[k, 32 + 3 * h + k, h * NSA_HEAD_DIM:(h + 1) * NSA_HEAD_DIM] = 1.0
    return sel


def _nsa_body(seq, n_cmp, ck, wk, q_ref, qs_ref, kvs_ref, kvw_ref, ckv_ref, small_ref, ov_ref, gsel_ref, o_ref):
    i = pl.program_id(1)
    tq = Q_BLOCK
    n_slc = seq // SLC_LEN
    nc = ckv_ref.shape[2]
    npair = NSA_HPG // 2
    rows = NSA_HPG * tq
    tpos = i * tq + lax.broadcasted_iota(jnp.int32, (tq, 1), 0)
    tpos8 = jnp.concatenate([tpos] * NSA_HPG, axis=0)
    q = q_ref[...]
    qs = qs_ref[...]
    lane_lo = lax.broadcasted_iota(jnp.int32, (tq, LANES), 1) < NSA_HEAD_DIM

    def pair_tiles(o):
        tiles = []
        for p in range(npair):
            even = o[p * tq:(p + 1) * tq]
            odd = o[(npair + p) * tq:(npair + p + 1) * tq]
            tiles.append(jnp.where(lane_lo, even, pltpu.roll(odd, NSA_HEAD_DIM, 1)))
        return tiles

    oc_tiles, os_tiles, ow_tiles = [], [], []
    for g in range(NSA_KV):
        qg = jnp.concatenate([q[:, (g * npair + p) * LANES:(g * npair + p + 1) * LANES] for p in range(npair)]
                             + [qs[:, (g * npair + p) * LANES:(g * npair + p + 1) * LANES] for p in range(npair)],
                             axis=0)
        kc = ckv_ref[0, g].astype(BF16)
        vc = ckv_ref[0, NSA_KV + g].astype(BF16)
        s_c = _dot_nt(qg, kc)
        cidx = lax.broadcasted_iota(jnp.int32, (1, nc), 1)
        m_c = (cidx * CMP_STRIDE + (CMP_LEN - 1) <= tpos8) & (cidx < n_cmp)
        s_c = jnp.where(m_c, s_c, -jnp.inf)
        mx = jnp.max(s_c, axis=-1, keepdims=True)
        mx = jnp.where(mx == -jnp.inf, 0.0, mx)
        e = jnp.exp(s_c - mx)
        d = jnp.sum(e, axis=-1, keepdims=True)
        p_c = e / jnp.where(d > 0, d, 1.0)
        oc_tiles += pair_tiles(_dot(p_c.astype(BF16), vc))
        p_sum = p_c[0:tq]
        for h in range(1, NSA_HPG):
            p_sum = p_sum + p_c[h * tq:(h + 1) * tq]
        imp = _dot3(p_sum, ov_ref[...])
        jidx = lax.broadcasted_iota(jnp.int32, (1, n_slc), 1)
        qblk = tpos // SLC_LEN
        valid = jidx * SLC_LEN <= tpos
        forced = (jidx == 0) | (jidx == qblk) | (jidx == qblk - 1)
        score = jnp.where(valid, imp + FORCE_SCORE * forced.astype(F32), -jnp.inf)
        rank = jnp.zeros((tq, n_slc), F32)
        for k in range(n_slc):
            col = score[:, k:k + 1]
            ahead = (col > score) | ((col == score) & (jidx > k))
            rank = rank + ahead.astype(F32)
        sel = (rank < float(min(SLC_TOPN, n_slc))).astype(BF16)

        def slc_step(c, carry):
            m_i, l_i, acc = carry
            start = pl.multiple_of(c * ck, ck)
            k_c = kvs_ref[pl.ds(start, ck), g * LANES:(g + 1) * LANES]
            v_c = kvs_ref[pl.ds(start, ck), (NSA_KV + g) * LANES:(NSA_KV + g + 1) * LANES]
            s = _dot_nt(qg, k_c)
            blk = c * (ck // SLC_LEN) + lax.broadcasted_iota(jnp.int32, (n_slc, ck), 1) // SLC_LEN
            expand = (lax.broadcasted_iota(jnp.int32, (n_slc, ck), 0) == blk).astype(BF16)
            kpos = start + lax.broadcasted_iota(jnp.int32, (1, ck), 1)
            mb = (_dot(sel, expand) > 0.5) & (kpos <= tpos)
            mb8 = jnp.concatenate([mb] * NSA_HPG, axis=0)
            m_new = jnp.maximum(m_i, jnp.max(jnp.where(mb8, s, -jnp.inf), axis=-1, keepdims=True))
            alpha = jnp.exp(m_i - m_new)
            p = jnp.where(mb8, jnp.exp(s - m_new), 0.0)
            l_new = alpha * l_i + jnp.sum(p, axis=-1, keepdims=True)
            acc_new = alpha * acc + _dot(p.astype(BF16), v_c)
            return m_new, l_new, acc_new

        n_chunks = (i * tq + tq - 1) // ck + 1
        m0 = jnp.full((rows, 1), -1e30, F32)
        l0 = jnp.zeros((rows, 1), F32)
        a0 = jnp.zeros((rows, LANES), F32)
        _, l_s, acc_s = lax.fori_loop(0, n_chunks, slc_step, (m0, l0, a0))
        os_tiles += pair_tiles(acc_s / l_s)

        w_start = jnp.clip((i - WINDOW // tq) * tq, 0, seq - wk)
        w_start = pl.multiple_of(w_start, tq)
        k_w = kvw_ref[pl.ds(w_start, wk), g * LANES:(g + 1) * LANES]
        v_w = kvw_ref[pl.ds(w_start, wk), (NSA_KV + g) * LANES:(NSA_KV + g + 1) * LANES]
        s_w = _dot_nt(qg, k_w)
        dpos = tpos8 - (w_start + lax.broadcasted_iota(jnp.int32, (1, wk), 1))
        m_w = (dpos >= 0) & (dpos < WINDOW)
        s_w = jnp.where(m_w, s_w, -jnp.inf)
        e_w = jnp.exp(s_w - jnp.max(s_w, axis=-1, keepdims=True))
        o_w = _dot(e_w.astype(BF16), v_w) / jnp.sum(e_w, axis=-1, keepdims=True)
        ow_tiles += pair_tiles(o_w)

    gates = jax.nn.sigmoid(small_ref[...])
    out = (_dot3(gates, gsel_ref[0]) * jnp.concatenate(oc_tiles, axis=1)
           + _dot3(gates, gsel_ref[1]) * jnp.concatenate(os_tiles, axis=1)
           + _dot3(gates, gsel_ref[2]) * jnp.concatenate(ow_tiles, axis=1))
    o_ref[...] = out.astype(BF16)


def nsa_prompt_call(q_rot, q_swp, kv_pad, ckv, proj, bsz, seq, n_cmp):
    nq = seq // Q_BLOCK
    nc = ckv.shape[2]
    n_slc = seq // SLC_LEN
    ck = min(512, seq)
    wk = min(WINDOW + Q_BLOCK, seq)
    ov = jnp.asarray(np.pad(_overlap_np(n_cmp, n_slc), ((0, nc - n_cmp), (0, 0))), BF16)
    gsel = jnp.asarray(_gate_select_np(), BF16)
    small_blk = proj.shape[1] // LANES - 1
    return pl.pallas_call(
        functools.partial(_nsa_body, seq, n_cmp, ck, wk),
        grid=(bsz, nq),
        in_specs=[pl.BlockSpec((Q_BLOCK, 1024), lambda b, i: (b * nq + i, 0)),
                  pl.BlockSpec((Q_BLOCK, 1024), lambda b, i: (b * nq + i, 0)),
                  pl.BlockSpec((seq, 512), lambda b, i: (b, 0)),
                  pl.BlockSpec((seq, 512), lambda b, i: (b, 1)),
                  pl.BlockSpec((1, 4, nc, LANES), lambda b, i: (b, 0, 0, 0)),
                  pl.BlockSpec((Q_BLOCK, LANES), lambda b, i: (b * nq + i, small_blk)),
                  pl.BlockSpec((nc, n_slc), lambda b, i: (0, 0)),
                  pl.BlockSpec((3, LANES, 1024), lambda b, i: (0, 0, 0))],
        out_specs=pl.BlockSpec((Q_BLOCK, 1024), lambda b, i: (b * nq + i, 0)),
        out_shape=jax.ShapeDtypeStruct((bsz * seq, 1024), BF16),
        compiler_params=_params("parallel", "arbitrary"),
        name="nsa_prompt",
    )(q_rot, q_swp, kv_pad, kv_pad, ckv, proj, ov, gsel)


def _softplus(x):
    return jnp.maximum(x, 0.0) + jnp.log1p(jnp.exp(-jnp.abs(x)))


def _ssd_body(z_ref, xbc_ref, small_ref, smallt_ref, cw_ref, cb_ref, dtb_ref, dtbt_ref, alog_ref, alogt_ref,
              dskip_ref, nrm_ref, rexp_ref, ltri_ref, utri_ref, y_ref, st_ref, xbuf, ht):
    c = pl.program_id(1)
    q = SSM_CHUNK
    d_inner = z_ref.shape[1]
    n_heads = d_inner // SSM_HEAD_DIM
    gw = d_inner // SSM_GROUPS
    n = SSM_STATE

    @pl.when(c == 0)
    def _():
        xbuf[0:8, :] = jnp.zeros((8, xbuf.shape[1]), F32)
        ht[...] = jnp.zeros(ht.shape, F32)

    x = xbc_ref[...]
    xbuf[8:8 + q, :] = x
    cw = cw_ref[...]
    conv = (x * cw[3:4] + xbuf[7:7 + q, :] * cw[2:3] + xbuf[6:6 + q, :] * cw[1:2]
            + xbuf[5:5 + q, :] * cw[0:1] + cb_ref[...])
    xbuf[0:8, :] = x[q - 8:q]
    xc = conv * jax.nn.sigmoid(conv)
    xs = xc[:, :d_inner]
    bm = xc[:, d_inner:d_inner + SSM_GROUPS * n]
    cm = xc[:, d_inner + SSM_GROUPS * n:]

    dt = _softplus(small_ref[:, 0:n_heads] + dtb_ref[...])
    dtt = _softplus(smallt_ref[0:n_heads, :] + dtbt_ref[...])
    a = -jnp.exp(alog_ref[...])
    at = -jnp.exp(alogt_ref[...])
    cum = _dot3_left(ltri_ref[...], dt * a)
    cumt = _dot3(dtt * at, utri_ref[...])
    cum_last = cum[q - 1:q, :]
    rexp = rexp_ref[...]
    expcum_f = _dot3(jnp.exp(cum), rexp)
    toend_f = _dot3(jnp.exp(cum_last - cum) * dt, rexp)
    cdec_f = _dot3(jnp.broadcast_to(jnp.exp(cum_last), (8, n_heads)), rexp)[0:1]

    causal = (lax.broadcasted_iota(jnp.int32, (q, q), 0) >= lax.broadcasted_iota(jnp.int32, (q, q), 1))
    lane_lo = lax.broadcasted_iota(jnp.int32, (q, LANES), 1) < SSM_HEAD_DIM
    xs_bf = xs.astype(BF16)
    zero_bf = jnp.zeros((q, LANES), BF16)
    hpg = n_heads // SSM_GROUPS
    y_tiles, yoff_tiles = [], []
    for g in range(SSM_GROUPS):
        bg = bm[:, g * n:(g + 1) * n]
        cg_bf = cm[:, g * n:(g + 1) * n].astype(BF16)
        cb = _dot_nt(cg_bf, bg.astype(BF16))
        h_old = ht[g]
        yoff_tiles.append(_dot(cg_bf, h_old.astype(BF16)))
        for pr in range(hpg // 2):
            h0 = g * hpg + 2 * pr
            xp = xs_bf[:, h0 * SSM_HEAD_DIM:h0 * SSM_HEAD_DIM + LANES]
            acc = None
            for par, xm in ((0, jnp.where(lane_lo, xp, zero_bf)), (1, jnp.where(lane_lo, zero_bf, xp))):
                h = h0 + par
                seg = cum[:, h:h + 1] - cumt[h:h + 1, :]
                w = cb * jnp.exp(jnp.where(causal, seg, -jnp.inf)) * dtt[h:h + 1, :]
                part = _dot(w.astype(BF16), xm)
                acc = part if acc is None else acc + part
            y_tiles.append(acc)
        xw = (xs[:, g * gw:(g + 1) * gw] * toend_f[:, g * gw:(g + 1) * gw]).astype(BF16)
        ht[g] = h_old * cdec_f[:, g * gw:(g + 1) * gw] + _dot(bg.T.astype(BF16), xw)

    y = (jnp.concatenate(y_tiles, axis=1) + jnp.concatenate(yoff_tiles, axis=1) * expcum_f
         + dskip_ref[...] * xs)
    z = z_ref[...]
    y = y * (z * jax.nn.sigmoid(z))
    outs = []
    for g in range(SSM_GROUPS):
        yg = y[:, g * gw:(g + 1) * gw]
        ms = jnp.mean(yg * yg, axis=-1, keepdims=True)
        outs.append(yg * lax.rsqrt(ms + EPS) * nrm_ref[:, g * gw:(g + 1) * gw])
    y_ref[...] = jnp.concatenate(outs, axis=1).astype(BF16)
    st_ref[0] = ht[...]


def ssd_call(proj, small_t, conv_w, conv_b, dt_bias, a_log, d_skip, ssm_norm, bsz, seq):
    q = SSM_CHUNK
    nchunk = seq // q
    n_heads = dt_bias.shape[0]
    d_inner = n_heads * SSM_HEAD_DIM
    conv_dim = conv_w.shape[1]
    gw = d_inner // SSM_GROUPS
    small_blk = proj.shape[1] // LANES - 1
    rexp = jnp.asarray(np.repeat(np.eye(n_heads, dtype=np.float32), SSM_HEAD_DIM, axis=1), BF16)
    ltri = jnp.asarray(np.tril(np.ones((q, q), np.float32)), BF16)
    utri = jnp.asarray(np.triu(np.ones((q, q), np.float32)), BF16)
    const = lambda shape: pl.BlockSpec(shape, lambda b, c: (0,) * len(shape))
    return pl.pallas_call(
        _ssd_body,
        grid=(bsz, nchunk),
        in_specs=[pl.BlockSpec((q, d_inner), lambda b, c: (b * nchunk + c, 0)),
                  pl.BlockSpec((q, conv_dim), lambda b, c: (b * nchunk + c, 1)),
                  pl.BlockSpec((q, LANES), lambda b, c: (b * nchunk + c, small_blk)),
                  pl.BlockSpec((LANES, q), lambda b, c: (0, b * nchunk + c)),
                  const((SSM_CONV, conv_dim)), const((1, conv_dim)),
                  const((1, n_heads)), const((n_heads, 1)), const((1, n_heads)), const((n_heads, 1)),
                  const((1, d_inner)), const((1, d_inner)),
                  const((n_heads, d_inner)), const((q, q)), const((q, q))],
        out_specs=[pl.BlockSpec((q, d_inner), lambda b, c: (b * nchunk + c, 0)),
                   pl.BlockSpec((1, SSM_GROUPS, SSM_STATE, gw), lambda b, c: (b, 0, 0, 0))],
        out_shape=[jax.ShapeDtypeStruct((bsz * seq, d_inner), BF16),
                   jax.ShapeDtypeStruct((bsz, SSM_GROUPS, SSM_STATE, gw), F32)],
        scratch_shapes=[pltpu.VMEM((8 + q, conv_dim), F32),
                        pltpu.VMEM((SSM_GROUPS, SSM_STATE, gw), F32)],
        compiler_params=_params("parallel", "arbitrary"),
        name="ssd",
    )(proj, proj, proj, small_t, conv_w, conv_b.reshape(1, -1),
      dt_bias.reshape(1, -1), dt_bias.reshape(-1, 1), a_log.reshape(1, -1), a_log.reshape(-1, 1),
      jnp.repeat(d_skip, SSM_HEAD_DIM).reshape(1, -1), ssm_norm.reshape(1, -1), rexp, ltri, utri)


CONF_HALO = 32


def _conf_body(u_ref, w_ref, b_ref, g_ref, beta_ref, o_ref, tail_ref, buf):
    t = pl.program_id(1)
    tm, d = o_ref.shape

    @pl.when(t == 0)
    def _():
        buf[0:CONF_HALO, :] = jnp.zeros((CONF_HALO, d), F32)

    u = u_ref[...]
    a = u[:, :d] * jax.nn.sigmoid(u[:, d:])
    buf[CONF_HALO:CONF_HALO + tm, :] = a
    w = w_ref[...]
    off = CONF_HALO - (CONF_CONV - 1)
    acc = b_ref[...] + buf[off:off + tm, :] * w[0:1]
    for k in range(1, CONF_CONV):
        acc = acc + buf[off + k:off + k + tm, :] * w[k:k + 1]
    buf[0:CONF_HALO, :] = a[tm - CONF_HALO:tm]
    tail_ref[0] = a[tm - CONF_HALO:tm]
    mu = jnp.mean(acc, axis=-1, keepdims=True)
    cen = acc - mu
    var = jnp.mean(cen * cen, axis=-1, keepdims=True)
    y = cen * lax.rsqrt(var + EPS) * g_ref[...] + beta_ref[...]
    o_ref[...] = (y * jax.nn.sigmoid(y)).astype(BF16)


def conformer_mid_call(u, dw_w, dw_b, ln_g, ln_b, bsz, seq, tm):
    d = dw_w.shape[1]
    nt = seq // tm
    const = lambda shape: pl.BlockSpec(shape, lambda b, t: (0,) * len(shape))
    return pl.pallas_call(
        _conf_body,
        grid=(bsz, nt),
        in_specs=[pl.BlockSpec((tm, 2 * d), lambda b, t: (b * nt + t, 0)),
                  const((CONF_CONV, d)), const((1, d)), const((1, d)), const((1, d))],
        out_specs=[pl.BlockSpec((tm, d), lambda b, t: (b * nt + t, 0)),
                   pl.BlockSpec((1, CONF_HALO, d), lambda b, t: (b, 0, 0))],
        out_shape=[jax.ShapeDtypeStruct((bsz * seq, d), BF16),
                   jax.ShapeDtypeStruct((bsz, CONF_HALO, d), F32)],
        scratch_shapes=[pltpu.VMEM((CONF_HALO + tm, d), F32)],
        compiler_params=_params("parallel", "arbitrary"),
        name="conformer_mid",
    )(u, dw_w, dw_b.reshape(1, d), ln_g.reshape(1, d), ln_b.reshape(1, d))


def _ffn_act_body(u_ref, w_ref, b_ref, o_ref, buf):
    t = pl.program_id(1)
    tm, dff = o_ref.shape

    @pl.when(t == 0)
    def _():
        buf[0:8, :] = jnp.zeros((8, 2 * dff), F32)

    u = u_ref[...]
    buf[8:8 + tm, :] = u
    w = w_ref[...]
    y = u * w[2:3] + buf[7:7 + tm, :] * w[1:2] + buf[6:6 + tm, :] * w[0:1] + b_ref[...]
    buf[0:8, :] = u[tm - 8:tm]
    o_ref[...] = (jax.nn.gelu(y[:, :dff], approximate=True) * y[:, dff:]).astype(BF16)


def ffn_act_call(u, conv_w, conv_b, bsz, seq, tm):
    d2 = conv_w.shape[1]
    nt = seq // tm
    const = lambda shape: pl.BlockSpec(shape, lambda b, t: (0,) * len(shape))
    return pl.pallas_call(
        _ffn_act_body,
        grid=(bsz, nt),
        in_specs=[pl.BlockSpec((tm, d2), lambda b, t: (b * nt + t, 0)),
                  const((FFN_CONV, d2)), const((1, d2))],
        out_specs=pl.BlockSpec((tm, d2 // 2), lambda b, t: (b * nt + t, 0)),
        out_shape=jax.ShapeDtypeStruct((bsz * seq, d2 // 2), BF16),
        scratch_shapes=[pltpu.VMEM((8 + tm, d2), F32)],
        compiler_params=_params("parallel", "arbitrary"),
        name="ffn_act",
    )(u, conv_w, conv_b.reshape(1, d2))


def _rms(x, g):
    return x * lax.rsqrt(jnp.mean(x * x, axis=-1, keepdims=True) + EPS) * g


def _masked_softmax(s, mask):
    s = jnp.where(mask, s, -jnp.inf)
    m = jnp.max(s, axis=-1, keepdims=True)
    m = jnp.where(jnp.isfinite(m), m, 0.0)
    e = jnp.exp(s - m)
    d = jnp.sum(e, axis=-1, keepdims=True)
    return e / jnp.where(d > 0, d, 1.0)


def _sample_ssd_step(z, xbc, dt_raw, conv_hist, h0, conv_w, conv_b, dt_bias, a_log, d_skip, ssm_norm):
    bsz = z.shape[0]
    n_heads = dt_bias.shape[0]
    d_inner = n_heads * SSM_HEAD_DIM
    hpg = n_heads // SSM_GROUPS
    xh = jnp.concatenate([conv_hist, xbc[:, None, :]], axis=1)
    new_hist = xh[:, 1:]
    conv = jnp.sum(xh * conv_w[None], axis=1) + conv_b
    xc = conv * jax.nn.sigmoid(conv)
    xs = xc[:, :d_inner].reshape(bsz, SSM_GROUPS, hpg, SSM_HEAD_DIM)
    bm = xc[:, d_inner:d_inner + SSM_GROUPS * SSM_STATE].reshape(bsz, SSM_GROUPS, SSM_STATE)
    cm = xc[:, d_inner + SSM_GROUPS * SSM_STATE:].reshape(bsz, SSM_GROUPS, SSM_STATE)
    dt = jax.nn.softplus(dt_raw + dt_bias).reshape(bsz, SSM_GROUPS, hpg)
    a = (-jnp.exp(a_log)).reshape(SSM_GROUPS, hpg)
    dec = jnp.exp(dt * a)
    h0g = h0.reshape(bsz, SSM_GROUPS, hpg, SSM_HEAD_DIM, SSM_STATE)
    h_new = dec[..., None, None] * h0g + (dt[..., None] * xs)[..., None] * bm[:, :, None, None, :]
    y = jnp.sum(h_new * cm[:, :, None, None, :], axis=-1)
    y = y + d_skip.reshape(SSM_GROUPS, hpg)[None, :, :, None] * xs
    y = y.reshape(bsz, d_inner) * (z * jax.nn.sigmoid(z))
    y = _rms(y.reshape(bsz, SSM_GROUPS, d_inner // SSM_GROUPS),
             ssm_norm.reshape(SSM_GROUPS, d_inner // SSM_GROUPS)).reshape(bsz, d_inner)
    return y, new_hist, h_new.reshape(h0.shape)


def _compress_rows_jnp(rows, pe, w1, b1, w2, b2):
    bsz, t = rows.shape[:2]
    r = CMP_LEN // CMP_STRIDE
    nch = t // CMP_STRIDE
    n_cmp = nch - r + 1
    ch = rows[:, :nch * CMP_STRIDE].reshape(bsz, nch, CMP_STRIDE, NSA_KV, NSA_HEAD_DIM)
    blk = jnp.concatenate([ch[:, k:k + n_cmp] for k in range(r)], axis=2)
    blk = blk + pe[None, None, :, None, :]
    flat = blk.transpose(0, 1, 3, 2, 4).reshape(bsz, n_cmp, NSA_KV, CMP_LEN * NSA_HEAD_DIM)
    hid = jax.nn.silu(flat @ w1 + b1)
    return hid @ w2 + b2


def _sample_nsa(q, gates, kvc, kvs, kvw, cache_cmp, cache_slc, cache_win, page_table, cmp_w):
    db = q.shape[0]
    past_len = page_table.shape[1] * cache_cmp.shape[1]
    pe, w1, b1, w2, b2 = cmp_w
    gather = lambda cache: cache[page_table].reshape((db, past_len) + cache.shape[2:])
    rows_cmp = jnp.concatenate([gather(cache_cmp), kvc], axis=1)
    rows_slc = jnp.concatenate([gather(cache_slc), kvs], axis=1)
    kc = _compress_rows_jnp(rows_cmp[:, :, 0], pe[0], w1[0], b1[0], w2[0], b2[0])
    vc = _compress_rows_jnp(rows_cmp[:, :, 1], pe[1], w1[1], b1[1], w2[1], b2[1])
    n_cmp = kc.shape[1]
    cmp_end = jnp.arange(n_cmp) * CMP_STRIDE + CMP_LEN - 1
    t_all = rows_slc.shape[1]
    n_slc = -(-t_all // SLC_LEN)
    slc = jnp.pad(rows_slc, [(0, 0), (0, n_slc * SLC_LEN - t_all), (0, 0), (0, 0), (0, 0)])
    slc = slc.reshape(db, n_slc, SLC_LEN, 2, NSA_KV, NSA_HEAD_DIM).transpose(0, 4, 1, 2, 3, 5)
    wkv = jnp.concatenate([cache_win, kvw], axis=1)
    w_rows = cache_win.shape[1]
    wpos = past_len - w_rows + jnp.arange(w_rows + 1)
    qpos = past_len
    qg = (q[:, 0] * NSA_HEAD_DIM ** -0.5).reshape(db, NSA_KV, NSA_HPG, NSA_HEAD_DIM)
    s_c = jnp.einsum('bghd,bngd->bghn', qg, kc)
    p_c = _masked_softmax(s_c, (cmp_end <= qpos)[None, None, None, :])
    o_c = jnp.einsum('bghn,bngd->bghd', p_c, vc)
    imp = jnp.einsum('bghn,nj->bgj', p_c, jnp.asarray(_overlap_np(n_cmp, n_slc)),
                     precision=lax.Precision.HIGHEST)
    j = jnp.arange(n_slc)
    qblk = qpos // SLC_LEN
    forced = (j == 0) | (j == qblk) | (j == qblk - 1)
    score = jnp.where((j * SLC_LEN <= qpos)[None, None, :], imp + FORCE_SCORE * forced[None, None, :], -jnp.inf)
    _, idx = lax.top_k(score, min(SLC_TOPN, n_slc))
    bi = jnp.arange(db)[:, None, None]
    gi = jnp.arange(NSA_KV)[None, :, None]
    sel = slc[bi, gi, idx]
    s_s = jnp.einsum('bghd,bgnld->bghnl', qg, sel[..., 0, :])
    kpos = idx[..., None] * SLC_LEN + jnp.arange(SLC_LEN)
    m_s = (kpos <= qpos)[:, :, None]
    p_s = _masked_softmax(s_s.reshape(db, NSA_KV, NSA_HPG, -1), m_s.reshape(db, NSA_KV, 1, -1)).reshape(s_s.shape)
    o_s = jnp.einsum('bghnl,bgnld->bghd', p_s, sel[..., 1, :])
    s_w = jnp.einsum('bghd,bsgd->bghs', qg, wkv[:, :, 0])
    dpos = qpos - wpos
    p_w = _masked_softmax(s_w, ((dpos >= 0) & (dpos < WINDOW) & (wpos >= 0))[None, None, None, :])
    o_w = jnp.einsum('bghs,bsgd->bghd', p_w, wkv[:, :, 1])
    gt = jax.nn.sigmoid(gates).reshape(db, NSA_KV, NSA_HPG, 3)
    o = gt[..., 0:1] * o_c + gt[..., 1:2] * o_s + gt[..., 2:3] * o_w
    return o.reshape(db, NSA_HEADS * NSA_HEAD_DIM)


def _prep_w_in(w_in, d_inner, conv_dim, n_heads):
    nq = NSA_HEADS * NSA_HEAD_DIM
    nkv = 2 * NSA_KV * NSA_HEAD_DIM
    o = np.cumsum([0, d_inner, conv_dim, n_heads, nq, nkv, nkv, nkv, 3 * NSA_HEADS])
    z, xbc, dt, q, kvc, kvs, kvw, gates = (w_in[:, o[k]:o[k + 1]] for k in range(8))
    small = jnp.concatenate([dt, gates], axis=1)
    small = jnp.pad(small, ((0, 0), (0, LANES - small.shape[1])))
    return jnp.concatenate([z, q, xbc, kvc, kvs, kvw, small], axis=1).astype(BF16)


def _pick_tile(m, pref):
    return pref if m % pref == 0 else m


def kernel(x_prompt, x_sample, state_ssm, state_ssm_conv, cache_cmp, cache_slc, cache_win, state_conf_conv, state_ffn_conv, page_table, norm_g, ab_w_in, ab_conv_w, ab_conv_b, ab_dt_bias, ab_a_log, ab_d_skip, ab_ssm_norm, ab_cmp_pe, ab_cmp_w1, ab_cmp_b1, ab_cmp_w2, ab_cmp_b2, ab_w_out, c_w_pw1, c_b_pw1, c_dw_w, c_dw_b, c_ln_g, c_ln_b, c_w_pw2, c_b_pw2, ffn_w_up, ffn_conv_w, ffn_conv_b, ffn_w_down):
    bsz, seq, d_model = x_prompt.shape
    db, dseq, _ = x_sample.shape
    assert dseq == 1
    depth = norm_g.shape[0]
    n_heads = ab_dt_bias.shape[1]
    d_inner = n_heads * SSM_HEAD_DIM
    conv_dim = ab_conv_w.shape[2]
    d_ff2 = ffn_w_up.shape[2]
    d_conf = c_dw_w.shape[2] if c_dw_w.shape[0] else d_model
    past_len = page_table.shape[1] * cache_cmp.shape[2]
    mp, ms = bsz * seq, db
    hp = x_prompt.reshape(mp, d_model)
    hs = x_sample.reshape(ms, d_model)
    tm_p = _pick_tile(mp, 1024)
    tm_post = _pick_tile(mp, 512)
    tm_seq = _pick_tile(seq, 256)
    nkv = 2 * NSA_KV * NSA_HEAD_DIM

    cos_p, sin_p = _rope_tables(jnp.arange(seq, dtype=jnp.int32))
    cos_s, sin_s = _rope_tables(jnp.full((ms,), past_len, jnp.int32))
    nchk = seq // CMP_STRIDE
    n_cmp = nchk - CMP_LEN // CMP_STRIDE + 1

    outs = {k: [] for k in ("ssm_p", "ssm_s", "sconv_p", "sconv_s", "cmp_p", "cmp_s", "slc_p", "slc_s",
                            "win_p", "win_s", "conf_p", "conf_s", "ffn_p", "ffn_s")}
    kvshape = (2, NSA_KV, NSA_HEAD_DIM)
    for layer in range(depth):
        g = norm_g[layer]
        if layer % 2 == 0:
            a = layer // 2
            w_in = _prep_w_in(ab_w_in[a], d_inner, conv_dim, n_heads)
            n_proj = w_in.shape[1]
            zero_b = jnp.zeros((n_proj,), F32)
            w_out = ab_w_out[a].astype(BF16)
            zero_d = jnp.zeros((d_model,), F32)
            proj = norm_matmul(hp, g[0], w_in, zero_b, tm_p, 640)
            q_rot, q_swp, kv_pad, kvc, kvs, kvw = rope_call(proj, cos_p, sin_p, _pick_tile(seq, 512))
            ch = kvc.reshape(bsz, nchk, CMP_STRIDE, 4, NSA_HEAD_DIM).transpose(0, 3, 1, 2, 4)
            ch = ch.reshape(bsz, 4, nchk, CMP_STRIDE * NSA_HEAD_DIM)
            pe_flat = ab_cmp_pe[a].reshape(2, 1, CMP_LEN * NSA_HEAD_DIM)
            w2p = jnp.pad(ab_cmp_w2[a], ((0, 0), (0, 0), (0, LANES - NSA_HEAD_DIM))).astype(BF16)
            b2p = jnp.pad(ab_cmp_b2[a], ((0, 0), (0, LANES - NSA_HEAD_DIM))).reshape(2, 1, LANES)
            ckv = compress_call(ch, pe_flat, ab_cmp_w1[a].astype(BF16), ab_cmp_b1[a].reshape(2, 1, -1), w2p, b2p)
            y_nsa = nsa_prompt_call(q_rot, q_swp, kv_pad, ckv, proj, bsz, seq, n_cmp)
            small_t = proj[:, n_proj - LANES:].T
            y_ssm, st = ssd_call(proj, small_t, ab_conv_w[a], ab_conv_b[a], ab_dt_bias[a], ab_a_log[a],
                                 ab_d_skip[a], ab_ssm_norm[a], bsz, seq)
            hp = matmul_post([y_ssm, y_nsa], [w_out[:d_inner], w_out[d_inner:]], zero_d, hp, g[1], tm_post)
            hpg = n_heads // SSM_GROUPS
            outs["ssm_p"].append(st.reshape(bsz, SSM_GROUPS, SSM_STATE, hpg, SSM_HEAD_DIM)
                                 .transpose(0, 1, 3, 4, 2).reshape(bsz, n_heads, SSM_HEAD_DIM, SSM_STATE))
            xbc_p = proj[:, d_inner + 1024:d_inner + 1024 + conv_dim].reshape(bsz, seq, conv_dim)
            outs["sconv_p"].append(xbc_p[:, seq - (SSM_CONV - 1):])
            outs["cmp_p"].append(kvc.reshape((bsz, seq) + kvshape))
            outs["slc_p"].append(kvs.reshape((bsz, seq) + kvshape))
            outs["win_p"].append(kvw.reshape((bsz, seq) + kvshape)[:, seq - min(WINDOW, seq):])
            proj_s = norm_matmul(hs, g[0], w_in, zero_b, ms, 640)
            q_s, _, _, kvc_s, kvs_s, kvw_s = rope_call(proj_s, cos_s, sin_s, ms)
            z_s = proj_s[:, :d_inner]
            xbc_s = proj_s[:, d_inner + 1024:d_inner + 1024 + conv_dim]
            dt_s = proj_s[:, n_proj - LANES:n_proj - LANES + n_heads]
            gates_s = proj_s[:, n_proj - LANES + n_heads:n_proj - LANES + n_heads + 3 * NSA_HEADS]
            ys_ssm, hist_s, h_s = _sample_ssd_step(z_s, xbc_s, dt_s, state_ssm_conv[a], state_ssm[a],
                                                   ab_conv_w[a], ab_conv_b[a], ab_dt_bias[a], ab_a_log[a],
                                                   ab_d_skip[a], ab_ssm_norm[a])
            kvc_s = kvc_s.reshape((ms, 1) + kvshape)
            kvs_s = kvs_s.reshape((ms, 1) + kvshape)
            kvw_s = kvw_s.reshape((ms, 1) + kvshape)
            q_full = q_s.astype(F32).reshape(ms, 1, NSA_HEADS, NSA_HEAD_DIM) * NSA_HEAD_DIM ** 0.5
            cmp_w = (ab_cmp_pe[a], ab_cmp_w1[a], ab_cmp_b1[a], ab_cmp_w2[a], ab_cmp_b2[a])
            ys_nsa = _sample_nsa(q_full, gates_s, kvc_s, kvs_s, kvw_s, cache_cmp[a], cache_slc[a], cache_win[a],
                                 page_table, cmp_w)
            hs = matmul_post([ys_ssm.astype(BF16), ys_nsa.astype(BF16)], [w_out[:d_inner], w_out[d_inner:]],
                             zero_d, hs, g[1], ms)
            outs["ssm_s"].append(h_s)
            outs["sconv_s"].append(hist_s)
            outs["cmp_s"].append(kvc_s)
            outs["slc_s"].append(kvs_s)
            outs["win_s"].append(kvw_s)
        else:
            c = layer // 2
            w1 = c_w_pw1[c].astype(BF16)
            w2 = c_w_pw2[c].astype(BF16)
            u = norm_matmul(hp, g[0], w1, c_b_pw1[c], tm_p, 512)
            act, tail = conformer_mid_call(u, c_dw_w[c], c_dw_b[c], c_ln_g[c], c_ln_b[c], bsz, seq, tm_seq)
            hp = matmul_post([act], [w2], c_b_pw2[c], hp, g[1], tm_post)
            outs["conf_p"].append(tail[:, CONF_HALO - (CONF_CONV - 1):])
            u_s = norm_matmul(hs, g[0], w1, c_b_pw1[c], ms, 512)
            a_s = u_s[:, :d_conf] * jax.nn.sigmoid(u_s[:, d_conf:])
            ah = jnp.concatenate([state_conf_conv[c], a_s[:, None, :]], axis=1)
            conv = jnp.sum(ah * c_dw_w[c][None], axis=1) + c_dw_b[c]
            mu = jnp.mean(conv, axis=-1, keepdims=True)
            var = jnp.mean(jnp.square(conv - mu), axis=-1, keepdims=True)
            yln = (conv - mu) * lax.rsqrt(var + EPS) * c_ln_g[c] + c_ln_b[c]
            hs = matmul_post([(yln * jax.nn.sigmoid(yln)).astype(BF16)], [w2], c_b_pw2[c], hs, g[1], ms)
            outs["conf_s"].append(ah[:, 1:])
        w_up = ffn_w_up[layer].astype(BF16)
        w_dn = ffn_w_down[layer].astype(BF16)
        zero_up = jnp.zeros((d_ff2,), F32)
        zero_d = jnp.zeros((d_model,), F32)
        u = norm_matmul(hp, g[2], w_up, zero_up, tm_p, 512)
        act = ffn_act_call(u, ffn_conv_w[layer], ffn_conv_b[layer], bsz, seq, tm_seq)
        hp = matmul_post([act], [w_dn], zero_d, hp, g[3], tm_post)
        outs["ffn_p"].append(u.reshape(bsz, seq, d_ff2)[:, seq - (FFN_CONV - 1):])
        u_s = norm_matmul(hs, g[2], w_up, zero_up, ms, 512)
        uh = jnp.concatenate([state_ffn_conv[layer], u_s[:, None, :]], axis=1)
        y_s = jnp.sum(uh * ffn_conv_w[layer][None], axis=1) + ffn_conv_b[layer]
        act_s = jax.nn.gelu(y_s[:, :d_ff2 // 2], approximate=True) * y_s[:, d_ff2 // 2:]
        hs = matmul_post([act_s.astype(BF16)], [w_dn], zero_d, hs, g[3], ms)
        outs["ffn_s"].append(uh[:, 1:])

    st = lambda k: jnp.stack(outs[k])
    return (hp.reshape(bsz, seq, d_model), hs.reshape(db, dseq, d_model),
            st("ssm_p"), st("ssm_s"), st("sconv_p"), st("sconv_s"), st("cmp_p"), st("cmp_s"),
            st("slc_p"), st("slc_s"), st("win_p"), st("win_s"), st("conf_p"), st("conf_s"),
            st("ffn_p"), st("ffn_s"))
```

```python
import functools
import math

import jax
import jax.numpy as jnp
import numpy as np
from jax import lax
from jax.experimental import pallas as pl
from jax.experimental.pallas import tpu as pltpu

F32 = jnp.float32
BF16 = jnp.bfloat16

EPS = 1e-6
LANES = 128
VMEM_LIMIT = 48 * 1024 * 1024

SSM_HEAD_DIM = 64
SSM_GROUPS = 4
SSM_STATE = 128
SSM_CONV = 4
SSM_CHUNK = 128
NSA_HEADS = 16
NSA_KV = 2
NSA_HPG = NSA_HEADS // NSA_KV
NSA_HEAD_DIM = 64
CMP_LEN = 32
CMP_STRIDE = 16
SLC_LEN = 64
SLC_TOPN = 16
WINDOW = 512
Q_BLOCK = 128
ROPE_THETA = 10000.0
FORCE_SCORE = 1e4
CONF_CONV = 31
FFN_CONV = 3


def _params(*sem):
    return pltpu.CompilerParams(dimension_semantics=sem, vmem_limit_bytes=VMEM_LIMIT)


def _split3(v):
    hi = v.astype(BF16)
    r1 = v - hi.astype(F32)
    mid = r1.astype(BF16)
    lo = (r1 - mid.astype(F32)).astype(BF16)
    return hi, mid, lo


def _dot(a, b):
    return jnp.dot(a, b, preferred_element_type=F32)


def _dot_nt(a, b):
    return lax.dot_general(a, b, (((1,), (1,)), ((), ())), preferred_element_type=F32)


def _dot3(v, m_bf16):
    hi, mid, lo = _split3(v)
    return _dot(hi, m_bf16) + _dot(mid, m_bf16) + _dot(lo, m_bf16)


def _dot3_left(m_bf16, v):
    hi, mid, lo = _split3(v)
    return _dot(m_bf16, hi) + _dot(m_bf16, mid) + _dot(m_bf16, lo)


def _norm_matmul_body(x_ref, g_ref, w_ref, b_ref, o_ref, xn_ref):
    @pl.when(pl.program_id(1) == 0)
    def _():
        x = x_ref[...]
        ms = jnp.mean(x * x, axis=-1, keepdims=True)
        xn_ref[...] = (x * lax.rsqrt(ms + EPS) * g_ref[...]).astype(BF16)

    o_ref[...] = _dot(xn_ref[...], w_ref[...]) + b_ref[...]


def norm_matmul(x, g, w, b, tm, tn):
    m, k = x.shape
    n = w.shape[1]
    return pl.pallas_call(
        _norm_matmul_body,
        grid=(m // tm, n // tn),
        in_specs=[pl.BlockSpec((tm, k), lambda i, j: (i, 0)),
                  pl.BlockSpec((1, k), lambda i, j: (0, 0)),
                  pl.BlockSpec((k, tn), lambda i, j: (0, j)),
                  pl.BlockSpec((1, tn), lambda i, j: (0, j))],
        out_specs=pl.BlockSpec((tm, tn), lambda i, j: (i, j)),
        out_shape=jax.ShapeDtypeStruct((m, n), F32),
        scratch_shapes=[pltpu.VMEM((tm, k), BF16)],
        compiler_params=_params("parallel", "arbitrary"),
        name="norm_matmul",
    )(x, g.reshape(1, k), w, b.reshape(1, n))


def _matmul_post_body(n_pairs, *refs):
    a_refs = refs[:n_pairs]
    w_refs = refs[n_pairs:2 * n_pairs]
    b_ref, r_ref, g_ref, o_ref = refs[2 * n_pairs:]
    acc = _dot(a_refs[0][...], w_refs[0][...])
    for a_ref, w_ref in zip(a_refs[1:], w_refs[1:]):
        acc = acc + _dot(a_ref[...], w_ref[...])
    acc = acc + b_ref[...]
    ms = jnp.mean(acc * acc, axis=-1, keepdims=True)
    o_ref[...] = r_ref[...] + acc * lax.rsqrt(ms + EPS) * g_ref[...]


def matmul_post(a_list, w_list, b, resid, g, tm):
    m, n = resid.shape
    n_pairs = len(a_list)
    in_specs = [pl.BlockSpec((tm, a.shape[1]), lambda i: (i, 0)) for a in a_list]
    in_specs += [pl.BlockSpec(w.shape, lambda i: (0, 0)) for w in w_list]
    in_specs += [pl.BlockSpec((1, n), lambda i: (0, 0)),
                 pl.BlockSpec((tm, n), lambda i: (i, 0)),
                 pl.BlockSpec((1, n), lambda i: (0, 0))]
    return pl.pallas_call(
        functools.partial(_matmul_post_body, n_pairs),
        grid=(m // tm,),
        in_specs=in_specs,
        out_specs=pl.BlockSpec((tm, n), lambda i: (i, 0)),
        out_shape=jax.ShapeDtypeStruct((m, n), F32),
        compiler_params=_params("parallel"),
        name="matmul_post",
    )(*a_list, *w_list, b.reshape(1, n), resid, g.reshape(1, n))


def _rope_tables(pos):
    half = NSA_HEAD_DIM // 2
    inv = ROPE_THETA ** (-jnp.arange(half, dtype=F32) / half)
    ang = pos.astype(F32)[:, None] * inv[None, :]
    cos, sin = jnp.cos(ang), jnp.sin(ang)
    reps = LANES // NSA_HEAD_DIM
    cosf = jnp.tile(jnp.concatenate([cos, cos], axis=-1), (1, reps))
    sinf = jnp.tile(jnp.concatenate([-sin, sin], axis=-1), (1, reps))
    return cosf, sinf


def _rotate_half_partner(x):
    w = x.shape[-1]
    lane = lax.broadcasted_iota(jnp.int32, x.shape, x.ndim - 1)
    first = (lane % NSA_HEAD_DIM) < (NSA_HEAD_DIM // 2)
    return jnp.where(first, pltpu.roll(x, w - NSA_HEAD_DIM // 2, x.ndim - 1),
                     pltpu.roll(x, NSA_HEAD_DIM // 2, x.ndim - 1))


def _swap_head_pair(x):
    w = x.shape[-1]
    lane = lax.broadcasted_iota(jnp.int32, x.shape, x.ndim - 1)
    first = (lane % LANES) < NSA_HEAD_DIM
    return jnp.where(first, pltpu.roll(x, w - NSA_HEAD_DIM, x.ndim - 1),
                     pltpu.roll(x, NSA_HEAD_DIM, x.ndim - 1))


def _rope_body(q_ref, kv_ref, cos_ref, sin_ref, q_o, qs_o, kv_o, kvc_o, kvs_o, kvw_o):
    cos = cos_ref[...]
    sin = sin_ref[...]
    q = q_ref[...]
    nq = q.shape[1] // LANES
    cq = jnp.concatenate([cos] * nq, axis=1)
    sq = jnp.concatenate([sin] * nq, axis=1)
    qr = (q * cq + _rotate_half_partner(q) * sq) * (NSA_HEAD_DIM ** -0.5)
    q_o[...] = qr.astype(BF16)
    qs_o[...] = _swap_head_pair(qr).astype(BF16)

    kv = kv_ref[...]
    lane = lax.broadcasted_iota(jnp.int32, (kv.shape[0], LANES), 1)
    lo = lane < NSA_HEAD_DIM
    zero = jnp.zeros((kv.shape[0], LANES), F32)
    padded = []
    for c, full_o in enumerate((kvc_o, kvs_o, kvw_o)):
        k = kv[:, c * 256:c * 256 + LANES]
        v = kv[:, c * 256 + LANES:(c + 1) * 256]
        kr = k * cos + _rotate_half_partner(k) * sin
        full_o[...] = jnp.concatenate([kr, v], axis=1)
        if c > 0:
            kr_sw = pltpu.roll(kr, NSA_HEAD_DIM, 1)
            v_sw = pltpu.roll(v, NSA_HEAD_DIM, 1)
            padded += [jnp.where(lo, kr, zero), jnp.where(lo, kr_sw, zero),
                       jnp.where(lo, v, zero), jnp.where(lo, v_sw, zero)]
    kv_o[...] = jnp.concatenate(padded, axis=1).astype(BF16)


def rope_call(proj, cosf, sinf, tm):
    m = proj.shape[0]
    nblk = cosf.shape[0] // tm
    f32s = jax.ShapeDtypeStruct((m, 256), F32)
    return pl.pallas_call(
        _rope_body,
        grid=(m // tm,),
        in_specs=[pl.BlockSpec((tm, 1024), lambda i: (i, 2)),
                  pl.BlockSpec((tm, 768), lambda i: (i, 8)),
                  pl.BlockSpec((tm, LANES), lambda i: (i % nblk, 0)),
                  pl.BlockSpec((tm, LANES), lambda i: (i % nblk, 0))],
        out_specs=[pl.BlockSpec((tm, 1024), lambda i: (i, 0)),
                   pl.BlockSpec((tm, 1024), lambda i: (i, 0)),
                   pl.BlockSpec((tm, 1024), lambda i: (i, 0)),
                   pl.BlockSpec((tm, 256), lambda i: (i, 0)),
                   pl.BlockSpec((tm, 256), lambda i: (i, 0)),
                   pl.BlockSpec((tm, 256), lambda i: (i, 0))],
        out_shape=[jax.ShapeDtypeStruct((m, 1024), BF16), jax.ShapeDtypeStruct((m, 1024), BF16),
                   jax.ShapeDtypeStruct((m, 1024), BF16), f32s, f32s, f32s],
        compiler_params=_params("parallel"),
        name="rope",
    )(proj, proj, cosf, sinf)


def _compress_body(ch_ref, pe_ref, w1_ref, b1_ref, w2_ref, b2_ref, o_ref):
    ch = ch_ref[0, 0]
    nc, half = ch.shape
    pe = pe_ref[0]
    top = _dot((ch + pe[:, :half]).astype(BF16), w1_ref[0, :half, :])
    bot = _dot((ch + pe[:, half:]).astype(BF16), w1_ref[0, half:, :])
    hid = top + pltpu.roll(bot, nc - 1, 0) + b1_ref[0]
    hid = hid * jax.nn.sigmoid(hid)
    o_ref[0, 0] = _dot(hid.astype(BF16), w2_ref[0]) + b2_ref[0]


def compress_call(ch, pe_flat, w1, b1, w2p, b2p):
    bsz, _, nc, half = ch.shape
    hidden = w1.shape[-1]
    return pl.pallas_call(
        _compress_body,
        grid=(bsz, 4),
        in_specs=[pl.BlockSpec((1, 1, nc, half), lambda b, q: (b, q, 0, 0)),
                  pl.BlockSpec((1, 1, 2 * half), lambda b, q: (q // 2, 0, 0)),
                  pl.BlockSpec((1, 2 * half, hidden), lambda b, q: (q // 2, 0, 0)),
                  pl.BlockSpec((1, 1, hidden), lambda b, q: (q // 2, 0, 0)),
                  pl.BlockSpec((1, hidden, LANES), lambda b, q: (q // 2, 0, 0)),
                  pl.BlockSpec((1, 1, LANES), lambda b, q: (q // 2, 0, 0))],
        out_specs=pl.BlockSpec((1, 1, nc, LANES), lambda b, q: (b, q, 0, 0)),
        out_shape=jax.ShapeDtypeStruct((bsz, 4, nc, LANES), F32),
        compiler_params=_params("parallel", "parallel"),
        name="compress",
    )(ch, pe_flat, w1, b1, w2p, b2p)


def _overlap_np(nc, ns):
    i = np.arange(nc)[:, None]
    j = np.arange(ns)[None, :]
    return ((i * CMP_STRIDE < (j + 1) * SLC_LEN) & (i * CMP_STRIDE + CMP_LEN > j * SLC_LEN)).astype(np.float32)


def _gate_select_np():
    sel = np.zeros((3, LANES, NSA_HEADS * NSA_HEAD_DIM), np.float32)
    for k in range(3):
        for h in range(NSA_HEADS):
            sel[k, 32 + 3 * h + k, h * NSA_HEAD_DIM:(h + 1) * NSA_HEAD_DIM] = 1.0
    return sel


def _nsa_body(seq, n_cmp, ck, wk, q_ref, qs_ref, kvs_ref, kvw_ref, ckv_ref, small_ref, ov_ref, gsel_ref, o_ref):
    i = pl.program_id(1)
    tq = Q_BLOCK
    n_slc = seq // SLC_LEN
    nc = ckv_ref.shape[2]
    npair = NSA_HPG // 2
    rows = NSA_HPG * tq
    tpos = i * tq + lax.broadcasted_iota(jnp.int32, (tq, 1), 0)
    tpos8 = jnp.concatenate([tpos] * NSA_HPG, axis=0)
    q = q_ref[...]
    qs = qs_ref[...]
    lane_lo = lax.broadcasted_iota(jnp.int32, (tq, LANES), 1) < NSA_HEAD_DIM

    def pair_tiles(o):
        tiles = []
        for p in range(npair):
            even = o[p * tq:(p + 1) * tq]
            odd = o[(npair + p) * tq:(npair + p + 1) * tq]
            tiles.append(jnp.where(lane_lo, even, pltpu.roll(odd, NSA_HEAD_DIM, 1)))
        return tiles

    oc_tiles, os_tiles, ow_tiles = [], [], []
    for g in range(NSA_KV):
        qg = jnp.concatenate([q[:, (g * npair + p) * LANES:(g * npair + p + 1) * LANES] for p in range(npair)]
                             + [qs[:, (g * npair + p) * LANES:(g * npair + p + 1) * LANES] for p in range(npair)],
                             axis=0)
        kc = ckv_ref[0, g].astype(BF16)
        vc = ckv_ref[0, NSA_KV + g].astype(BF16)
        s_c = _dot_nt(qg, kc)
        cidx = lax.broadcasted_iota(jnp.int32, (1, nc), 1)
        m_c = (cidx * CMP_STRIDE + (CMP_LEN - 1) <= tpos8) & (cidx < n_cmp)
        s_c = jnp.where(m_c, s_c, -jnp.inf)
        mx = jnp.max(s_c, axis=-1, keepdims=True)
        mx = jnp.where(mx == -jnp.inf, 0.0, mx)
        e = jnp.exp(s_c - mx)
        d = jnp.sum(e, axis=-1, keepdims=True)
        p_c = e / jnp.where(d > 0, d, 1.0)
        oc_tiles += pair_tiles(_dot(p_c.astype(BF16), vc))
        p_sum = p_c[0:tq]
        for h in range(1, NSA_HPG):
            p_sum = p_sum + p_c[h * tq:(h + 1) * tq]
        imp = _dot3(p_sum, ov_ref[...])
        jidx = lax.broadcasted_iota(jnp.int32, (1, n_slc), 1)
        qblk = tpos // SLC_LEN
        valid = jidx * SLC_LEN <= tpos
        forced = (jidx == 0) | (jidx == qblk) | (jidx == qblk - 1)
        score = jnp.where(valid, imp + FORCE_SCORE * forced.astype(F32), -jnp.inf)
        rank = jnp.zeros((tq, n_slc), F32)
        for k in range(n_slc):
            col = score[:, k:k + 1]
            ahead = (col > score) | ((col == score) & (jidx > k))
            rank = rank + ahead.astype(F32)
        sel = (rank < float(min(SLC_TOPN, n_slc))).astype(BF16)

        def slc_step(c, carry):
            m_i, l_i, acc = carry
            start = pl.multiple_of(c * ck, ck)
            k_c = kvs_ref[pl.ds(start, ck), g * LANES:(g + 1) * LANES]
            v_c = kvs_ref[pl.ds(start, ck), (NSA_KV + g) * LANES:(NSA_KV + g + 1) * LANES]
            s = _dot_nt(qg, k_c)
            blk = c * (ck // SLC_LEN) + lax.broadcasted_iota(jnp.int32, (n_slc, ck), 1) // SLC_LEN
            expand = (lax.broadcasted_iota(jnp.int32, (n_slc, ck), 0) == blk).astype(BF16)
            kpos = start + lax.broadcasted_iota(jnp.int32, (1, ck), 1)
            mb = (_dot(sel, expand) > 0.5) & (kpos <= tpos)
            mb8 = jnp.concatenate([mb] * NSA_HPG, axis=0)
            m_new = jnp.maximum(m_i, jnp.max(jnp.where(mb8, s, -jnp.inf), axis=-1, keepdims=True))
            alpha = jnp.exp(m_i - m_new)
            p = jnp.where(mb8, jnp.exp(s - m_new), 0.0)
            l_new = alpha * l_i + jnp.sum(p, axis=-1, keepdims=True)
            acc_new = alpha * acc + _dot(p.astype(BF16), v_c)
            return m_new, l_new, acc_new

        n_chunks = (i * tq + tq - 1) // ck + 1
        m0 = jnp.full((rows, 1), -1e30, F32)
        l0 = jnp.zeros((rows, 1), F32)
        a0 = jnp.zeros((rows, LANES), F32)
        _, l_s, acc_s = lax.fori_loop(0, n_chunks, slc_step, (m0, l0, a0))
        os_tiles += pair_tiles(acc_s / l_s)

        w_start = jnp.clip((i - WINDOW // tq) * tq, 0, seq - wk)
        w_start = pl.multiple_of(w_start, tq)
        k_w = kvw_ref[pl.ds(w_start, wk), g * LANES:(g + 1) * LANES]
        v_w = kvw_ref[pl.ds(w_start, wk), (NSA_KV + g) * LANES:(NSA_KV + g + 1) * LANES]
        s_w = _dot_nt(qg, k_w)
        dpos = tpos8 - (w_start + lax.broadcasted_iota(jnp.int32, (1, wk), 1))
        m_w = (dpos >= 0) & (dpos < WINDOW)
        s_w = jnp.where(m_w, s_w, -jnp.inf)
        e_w = jnp.exp(s_w - jnp.max(s_w, axis=-1, keepdims=True))
        o_w = _dot(e_w.astype(BF16), v_w) / jnp.sum(e_w, axis=-1, keepdims=True)
        ow_tiles += pair_tiles(o_w)

    gates = jax.nn.sigmoid(small_ref[...])
    out = (_dot3(gates, gsel_ref[0]) * jnp.concatenate(oc_tiles, axis=1)
           + _dot3(gates, gsel_ref[1]) * jnp.concatenate(os_tiles, axis=1)
           + _dot3(gates, gsel_ref[2]) * jnp.concatenate(ow_tiles, axis=1))
    o_ref[...] = out.astype(BF16)


def nsa_prompt_call(q_rot, q_swp, kv_pad, ckv, proj, bsz, seq, n_cmp):
    nq = seq // Q_BLOCK
    nc = ckv.shape[2]
    n_slc = seq // SLC_LEN
    ck = min(512, seq)
    wk = min(WINDOW + Q_BLOCK, seq)
    ov = jnp.asarray(np.pad(_overlap_np(n_cmp, n_slc), ((0, nc - n_cmp), (0, 0))), BF16)
    gsel = jnp.asarray(_gate_select_np(), BF16)
    small_blk = proj.shape[1] // LANES - 1
    return pl.pallas_call(
        functools.partial(_nsa_body, seq, n_cmp, ck, wk),
        grid=(bsz, nq),
        in_specs=[pl.BlockSpec((Q_BLOCK, 1024), lambda b, i: (b * nq + i, 0)),
                  pl.BlockSpec((Q_BLOCK, 1024), lambda b, i: (b * nq + i, 0)),
                  pl.BlockSpec((seq, 512), lambda b, i: (b, 0)),
                  pl.BlockSpec((seq, 512), lambda b, i: (b, 1)),
                  pl.BlockSpec((1, 4, nc, LANES), lambda b, i: (b, 0, 0, 0)),
                  pl.BlockSpec((Q_BLOCK, LANES), lambda b, i: (b * nq + i, small_blk)),
                  pl.BlockSpec((nc, n_slc), lambda b, i: (0, 0)),
                  pl.BlockSpec((3, LANES, 1024), lambda b, i: (0, 0, 0))],
        out_specs=pl.BlockSpec((Q_BLOCK, 1024), lambda b, i: (b * nq + i, 0)),
        out_shape=jax.ShapeDtypeStruct((bsz * seq, 1024), BF16),
        compiler_params=_params("parallel", "arbitrary"),
        name="nsa_prompt",
    )(q_rot, q_swp, kv_pad, kv_pad, ckv, proj, ov, gsel)


def _page_copy(cache_hbm, pt_ref, buf, sem, b, p, half, slot):
    src = cache_hbm.at[pt_ref[b, p], :, pl.ds(half * LANES, LANES)]
    return pltpu.make_async_copy(src, buf.at[slot, p, half], sem.at[slot])


def _sample_nsa_body(n_pages, n_cmp, qpos, w_rows,
                     pt_ref, ccmp_hbm, cslc_hbm, q16_ref, qpad_ref, new_ref, win_ref, gates_ref,
                     pet_ref, w1t_ref, b1_ref, w2t_ref, b2_ref, ov_ref,
                     y_ref, buf, sem, sel_scr, oc_scr):
    s = pl.program_id(0)
    nb = pl.num_programs(0) // 2
    b = s // 2
    page = buf.shape[3]
    n_rows = n_pages * page
    nch = n_rows // CMP_STRIDE
    nsp = ov_ref.shape[1]
    heads = NSA_HEADS
    row_g0 = lax.broadcasted_iota(jnp.int32, (heads, 1), 0) < NSA_HPG

    def fetch(cache_hbm, bb, slot):
        for p in range(n_pages):
            for half in range(2):
                _page_copy(cache_hbm, pt_ref, buf, sem, bb, p, half, slot).start()

    def wait(cache_hbm, bb, slot):
        for p in range(n_pages):
            for half in range(2):
                _page_copy(cache_hbm, pt_ref, buf, sem, bb, p, half, slot).wait()

    @pl.when(s == 0)
    def _():
        fetch(ccmp_hbm, 0, 0)

    @pl.when(s % 2 == 0)
    def _():
        fetch(cslc_hbm, b, 1)
        wait(ccmp_hbm, b, 0)
        tops = [[] for _ in range(4)]
        bots = [[] for _ in range(4)]
        for r in range(CMP_STRIDE):
            for kind in range(2):
                x_r = buf[pl.ds(0, 1), :, pl.ds(kind, 1), pl.ds(r, page // CMP_STRIDE, stride=CMP_STRIDE), :]
                xt = x_r.reshape(nch, LANES).T
                pe_rows = pl.ds(kind * LANES, LANES)
                top = (xt + pet_ref[pe_rows, r:r + 1]).astype(BF16)
                bot = (xt + pet_ref[pe_rows, CMP_STRIDE + r:CMP_STRIDE + r + 1]).astype(BF16)
                for g in range(NSA_KV):
                    tops[kind * NSA_KV + g].append(top[g * NSA_HEAD_DIM:(g + 1) * NSA_HEAD_DIM])
                    bots[kind * NSA_KV + g].append(bot[g * NSA_HEAD_DIM:(g + 1) * NSA_HEAD_DIM])
        half_w = CMP_STRIDE * NSA_HEAD_DIM
        ckv_t = []
        for qq in range(4):
            kind = qq // NSA_KV
            w1t = w1t_ref[kind]
            hid = (_dot(w1t[:, :half_w], jnp.concatenate(tops[qq], axis=0))
                   + pltpu.roll(_dot(w1t[:, half_w:], jnp.concatenate(bots[qq], axis=0)), nch - 1, 1)
                   + b1_ref[kind])
            hid = hid * jax.nn.sigmoid(hid)
            ckv_t.append((_dot(w2t_ref[kind], hid.astype(BF16)) + b2_ref[kind]).astype(BF16))
        q16 = q16_ref[0]
        cidx = lax.broadcasted_iota(jnp.int32, (1, nch), 1)
        m_c = (cidx * CMP_STRIDE + (CMP_LEN - 1) <= qpos) & (cidx < n_cmp)
        oc = []
        for g in range(NSA_KV):
            s_c = jnp.where(m_c, _dot(q16, ckv_t[g]), -jnp.inf)
            mx = jnp.max(s_c, axis=-1, keepdims=True)
            mx = jnp.where(mx == -jnp.inf, 0.0, mx)
            e = jnp.exp(s_c - mx)
            d = jnp.sum(e, axis=-1, keepdims=True)
            p_c = e / jnp.where(d > 0, d, 1.0)
            oc.append(_dot_nt(p_c.astype(BF16), ckv_t[NSA_KV + g]))
            in_group = row_g0 if g == 0 else jnp.logical_not(row_g0)
            p_sum = jnp.sum(jnp.where(in_group, p_c, 0.0), axis=0, keepdims=True)
            imp = _dot3(jnp.broadcast_to(p_sum, (8, nch)), ov_ref[...])[0:1]
            jidx = lax.broadcasted_iota(jnp.int32, (1, nsp), 1)
            qblk = qpos // SLC_LEN
            forced = (jidx == 0) | (jidx == qblk) | (jidx == qblk - 1)
            score = jnp.where(jidx * SLC_LEN <= qpos, imp + FORCE_SCORE * forced.astype(F32), -jnp.inf)
            s_j = jnp.broadcast_to(score, (nsp, nsp))
            s_k = s_j.T
            kk = lax.broadcasted_iota(jnp.int32, (nsp, nsp), 0)
            jj = lax.broadcasted_iota(jnp.int32, (nsp, nsp), 1)
            ahead = (s_k > s_j) | ((s_k == s_j) & (kk < jj))
            rank = jnp.sum(ahead.astype(F32), axis=0, keepdims=True)
            sel_scr[g:g + 1, :] = (rank < float(SLC_TOPN)).astype(F32)
        oc_scr[...] = jnp.where(row_g0, oc[0], oc[1])

    @pl.when(s % 2 == 1)
    def _():
        @pl.when(b + 1 < nb)
        def _():
            fetch(ccmp_hbm, b + 1, 0)

        wait(cslc_hbm, b, 1)
        qpad = qpad_ref[0]
        sel16 = jnp.where(row_g0, jnp.broadcast_to(sel_scr[0:1, :], (heads, nsp)),
                          jnp.broadcast_to(sel_scr[1:2, :], (heads, nsp)))
        sel16_bf = sel16.astype(BF16)
        ppc = 8
        ck = ppc * page
        n_ck = n_rows // ck
        vals, scores = [], []
        for c in range(n_ck):
            k_c = buf[1, c * ppc:(c + 1) * ppc, 0].reshape(ck, LANES).astype(BF16)
            vals.append(buf[1, c * ppc:(c + 1) * ppc, 1].reshape(ck, LANES).astype(BF16))
            blk = c * (ck // SLC_LEN) + lax.broadcasted_iota(jnp.int32, (nsp, ck), 1) // SLC_LEN
            expand = (lax.broadcasted_iota(jnp.int32, (nsp, ck), 0) == blk).astype(BF16)
            kpos = c * ck + lax.broadcasted_iota(jnp.int32, (1, ck), 1)
            mask = (_dot(sel16_bf, expand) > 0.5) & (kpos <= qpos)
            scores.append(jnp.where(mask, _dot_nt(qpad, k_c), -jnp.inf))
        new = new_ref[0]
        ks_new = new[:, 0:LANES].astype(BF16)
        vs_new = new[:, LANES:2 * LANES].astype(BF16)
        kw_new = new[:, 2 * LANES:3 * LANES].astype(BF16)
        vw_new = new[:, 3 * LANES:].astype(BF16)
        first = lax.broadcasted_iota(jnp.int32, (1, new.shape[0]), 1) == 0
        nblk = n_rows // SLC_LEN
        s_new = jnp.where(first & (sel16[:, nblk:nblk + 1] > 0.5), _dot_nt(qpad, ks_new), -jnp.inf)
        mx = jnp.max(s_new, axis=-1, keepdims=True)
        for sc in scores:
            mx = jnp.maximum(mx, jnp.max(sc, axis=-1, keepdims=True))
        e_new = jnp.exp(s_new - mx)
        den = jnp.sum(e_new, axis=-1, keepdims=True)
        acc = _dot(e_new.astype(BF16), vs_new)
        for sc, v_c in zip(scores, vals):
            e = jnp.exp(sc - mx)
            den = den + jnp.sum(e, axis=-1, keepdims=True)
            acc = acc + _dot(e.astype(BF16), v_c)
        o_s = acc / den
        k_w = win_ref[0, :, 0:LANES].astype(BF16)
        v_w = win_ref[0, :, LANES:2 * LANES].astype(BF16)
        dpos = w_rows - lax.broadcasted_iota(jnp.int32, (1, w_rows), 1)
        m_w = (dpos >= 0) & (dpos < WINDOW) & (qpos - dpos >= 0)
        s_w = jnp.where(m_w, _dot_nt(qpad, k_w), -jnp.inf)
        s_wn = jnp.where(first, _dot_nt(qpad, kw_new), -jnp.inf)
        mxw = jnp.maximum(jnp.max(s_w, axis=-1, keepdims=True), jnp.max(s_wn, axis=-1, keepdims=True))
        e_w = jnp.exp(s_w - mxw)
        e_wn = jnp.exp(s_wn - mxw)
        o_w = ((_dot(e_w.astype(BF16), v_w) + _dot(e_wn.astype(BF16), vw_new))
               / (jnp.sum(e_w, axis=-1, keepdims=True) + jnp.sum(e_wn, axis=-1, keepdims=True)))

        def own_group(o):
            return jnp.where(row_g0, o, pltpu.roll(o, NSA_HEAD_DIM, 1))[:, :NSA_HEAD_DIM]

        gt = jax.nn.sigmoid(gates_ref[0])
        y_ref[0] = gt[:, 0:1] * oc_scr[...] + gt[:, 1:2] * own_group(o_s) + gt[:, 2:3] * own_group(o_w)


def sample_nsa_call(page_table, cache_cmp, cache_slc, cache_win, q_rot, kvs_new, kvw_new, gates, pe, w1, b1, w2, b2):
    db, n_pages = page_table.shape
    n_phys, page = cache_cmp.shape[:2]
    past_len = n_pages * page
    nch = past_len // CMP_STRIDE
    n_cmp = nch - CMP_LEN // CMP_STRIDE + 1
    n_slc = -(-(past_len + 1) // SLC_LEN)
    nsp = -(-n_slc // LANES) * LANES
    w_rows = cache_win.shape[1]
    lanes = 4 * NSA_HEAD_DIM
    q16 = q_rot.reshape(db, NSA_HEADS, NSA_HEAD_DIM)
    q4 = q_rot.reshape(db, NSA_KV, NSA_HPG, NSA_HEAD_DIM)
    qpad = jnp.concatenate([jnp.pad(q4[:, g], ((0, 0), (0, 0), (g * NSA_HEAD_DIM, LANES - (g + 1) * NSA_HEAD_DIM)))
                            for g in range(NSA_KV)], axis=1)
    new = jnp.pad(jnp.concatenate([kvs_new, kvw_new], axis=1)[:, None, :], ((0, 0), (0, 15), (0, 0)))
    pet = jnp.repeat(pe, NSA_KV, axis=0).transpose(0, 2, 1).reshape(lanes, CMP_LEN)
    ov = jnp.asarray(np.pad(_overlap_np(n_cmp, n_slc), ((0, nch - n_cmp), (0, nsp - n_slc))), BF16)
    hidden = w1.shape[-1]
    full = lambda shape: pl.BlockSpec(shape, lambda s, pt: (0,) * len(shape))
    per_b = lambda shape: pl.BlockSpec((1,) + shape, lambda s, pt: (s // 2,) + (0,) * len(shape))
    grid_spec = pltpu.PrefetchScalarGridSpec(
        num_scalar_prefetch=1,
        grid=(2 * db,),
        in_specs=[pl.BlockSpec(memory_space=pl.ANY), pl.BlockSpec(memory_space=pl.ANY),
                  per_b((NSA_HEADS, NSA_HEAD_DIM)), per_b((NSA_HEADS, LANES)), per_b((16, 2 * lanes)),
                  per_b((w_rows, lanes)), per_b((NSA_HEADS, 3)),
                  full((lanes, CMP_LEN)), full((2, hidden, CMP_LEN * NSA_HEAD_DIM)), full((2, hidden, 1)),
                  full((2, NSA_HEAD_DIM, hidden)), full((2, NSA_HEAD_DIM, 1)), full((nch, nsp))],
        out_specs=per_b((NSA_HEADS, NSA_HEAD_DIM)),
        scratch_shapes=[pltpu.VMEM((2, n_pages, 2, page, LANES), F32),
                        pltpu.SemaphoreType.DMA((2,)),
                        pltpu.VMEM((8, nsp), F32),
                        pltpu.VMEM((NSA_HEADS, NSA_HEAD_DIM), F32)])
    y = pl.pallas_call(
        functools.partial(_sample_nsa_body, n_pages, n_cmp, past_len, w_rows),
        grid_spec=grid_spec,
        out_shape=jax.ShapeDtypeStruct((db, NSA_HEADS, NSA_HEAD_DIM), F32),
        compiler_params=_params("arbitrary"),
        name="sample_nsa",
    )(page_table, cache_cmp.reshape(n_phys, page, lanes), cache_slc.reshape(n_phys, page, lanes),
      q16, qpad, new, cache_win.reshape(db, w_rows, lanes), gates.reshape(db, NSA_HEADS, 3),
      pet, w1.transpose(0, 2, 1).astype(BF16), b1.reshape(2, hidden, 1),
      w2.transpose(0, 2, 1).astype(BF16), b2.reshape(2, NSA_HEAD_DIM, 1), ov)
    return y.reshape(db, NSA_HEADS * NSA_HEAD_DIM)


def _softplus(x):
    return jnp.maximum(x, 0.0) + jnp.log1p(jnp.exp(-jnp.abs(x)))


def _ssd_body(z_ref, xbc_ref, small_ref, cw_ref, cb_ref, dtb_ref, dtbt_ref, alog_ref, alogt_ref,
              dskip_ref, nrm_ref, rexp_ref, ltri_ref, utri_ref, y_ref, st_ref, xbuf, ht):
    c = pl.program_id(1)
    q = SSM_CHUNK
    d_inner = z_ref.shape[1]
    n_heads = d_inner // SSM_HEAD_DIM
    gw = d_inner // SSM_GROUPS
    n = SSM_STATE

    @pl.when(c == 0)
    def _():
        xbuf[0:8, :] = jnp.zeros((8, xbuf.shape[1]), F32)
        ht[...] = jnp.zeros(ht.shape, F32)

    x = xbc_ref[...]
    xbuf[8:8 + q, :] = x
    cw = cw_ref[...]
    conv = (x * cw[3:4] + xbuf[7:7 + q, :] * cw[2:3] + xbuf[6:6 + q, :] * cw[1:2]
            + xbuf[5:5 + q, :] * cw[0:1] + cb_ref[...])
    xbuf[0:8, :] = x[q - 8:q]
    xc = conv * jax.nn.sigmoid(conv)
    xs = xc[:, :d_inner]
    bm = xc[:, d_inner:d_inner + SSM_GROUPS * n]
    cm = xc[:, d_inner + SSM_GROUPS * n:]

    small = small_ref[...]
    dt = _softplus(small[:, 0:n_heads] + dtb_ref[...])
    dtt = _softplus(small.T[0:n_heads, :] + dtbt_ref[...])
    a = -jnp.exp(alog_ref[...])
    at = -jnp.exp(alogt_ref[...])
    cum = _dot3_left(ltri_ref[...], dt * a)
    cumt = _dot3(dtt * at, utri_ref[...])
    cum_last = cum[q - 1:q, :]
    rexp = rexp_ref[...]
    expcum_f = _dot3(jnp.exp(cum), rexp)
    toend_f = _dot3(jnp.exp(cum_last - cum) * dt, rexp)
    cdec_f = _dot3(jnp.broadcast_to(jnp.exp(cum_last), (8, n_heads)), rexp)[0:1]

    causal = (lax.broadcasted_iota(jnp.int32, (q, q), 0) >= lax.broadcasted_iota(jnp.int32, (q, q), 1))
    lane_lo = lax.broadcasted_iota(jnp.int32, (q, LANES), 1) < SSM_HEAD_DIM
    xs_bf = xs.astype(BF16)
    zero_bf = jnp.zeros((q, LANES), BF16)
    hpg = n_heads // SSM_GROUPS
    y_tiles, yoff_tiles = [], []
    for g in range(SSM_GROUPS):
        bg = bm[:, g * n:(g + 1) * n]
        cg_bf = cm[:, g * n:(g + 1) * n].astype(BF16)
        cb = _dot_nt(cg_bf, bg.astype(BF16))
        h_old = ht[g]
        yoff_tiles.append(_dot(cg_bf, h_old.astype(BF16)))
        for pr in range(hpg // 2):
            h0 = g * hpg + 2 * pr
            xp = xs_bf[:, h0 * SSM_HEAD_DIM:h0 * SSM_HEAD_DIM + LANES]
            acc = None
            for par, xm in ((0, jnp.where(lane_lo, xp, zero_bf)), (1, jnp.where(lane_lo, zero_bf, xp))):
                h = h0 + par
                seg = cum[:, h:h + 1] - cumt[h:h + 1, :]
                w = cb * jnp.exp(jnp.where(causal, seg, -jnp.inf)) * dtt[h:h + 1, :]
                part = _dot(w.astype(BF16), xm)
                acc = part if acc is None else acc + part
            y_tiles.append(acc)
        xw = (xs[:, g * gw:(g + 1) * gw] * toend_f[:, g * gw:(g + 1) * gw]).astype(BF16)
        ht[g] = h_old * cdec_f[:, g * gw:(g + 1) * gw] + _dot(bg.T.astype(BF16), xw)

    y = (jnp.concatenate(y_tiles, axis=1) + jnp.concatenate(yoff_tiles, axis=1) * expcum_f
         + dskip_ref[...] * xs)
    z = z_ref[...]
    y = y * (z * jax.nn.sigmoid(z))
    outs = []
    for g in range(SSM_GROUPS):
        yg = y[:, g * gw:(g + 1) * gw]
        ms = jnp.mean(yg * yg, axis=-1, keepdims=True)
        outs.append(yg * lax.rsqrt(ms + EPS) * nrm_ref[:, g * gw:(g + 1) * gw])
    y_ref[...] = jnp.concatenate(outs, axis=1).astype(BF16)
    st_ref[0] = ht[...]


def ssd_call(proj, conv_w, conv_b, dt_bias, a_log, d_skip, ssm_norm, bsz, seq):
    q = SSM_CHUNK
    nchunk = seq // q
    n_heads = dt_bias.shape[0]
    d_inner = n_heads * SSM_HEAD_DIM
    conv_dim = conv_w.shape[1]
    gw = d_inner // SSM_GROUPS
    small_blk = proj.shape[1] // LANES - 1
    rexp = jnp.asarray(np.repeat(np.eye(n_heads, dtype=np.float32), SSM_HEAD_DIM, axis=1), BF16)
    ltri = jnp.asarray(np.tril(np.ones((q, q), np.float32)), BF16)
    utri = jnp.asarray(np.triu(np.ones((q, q), np.float32)), BF16)
    const = lambda shape: pl.BlockSpec(shape, lambda b, c: (0,) * len(shape))
    return pl.pallas_call(
        _ssd_body,
        grid=(bsz, nchunk),
        in_specs=[pl.BlockSpec((q, d_inner), lambda b, c: (b * nchunk + c, 0)),
                  pl.BlockSpec((q, conv_dim), lambda b, c: (b * nchunk + c, 1)),
                  pl.BlockSpec((q, LANES), lambda b, c: (b * nchunk + c, small_blk)),
                  const((SSM_CONV, conv_dim)), const((1, conv_dim)),
                  const((1, n_heads)), const((n_heads, 1)), const((1, n_heads)), const((n_heads, 1)),
                  const((1, d_inner)), const((1, d_inner)),
                  const((n_heads, d_inner)), const((q, q)), const((q, q))],
        out_specs=[pl.BlockSpec((q, d_inner), lambda b, c: (b * nchunk + c, 0)),
                   pl.BlockSpec((1, SSM_GROUPS, SSM_STATE, gw), lambda b, c: (b, 0, 0, 0))],
        out_shape=[jax.ShapeDtypeStruct((bsz * seq, d_inner), BF16),
                   jax.ShapeDtypeStruct((bsz, SSM_GROUPS, SSM_STATE, gw), F32)],
        scratch_shapes=[pltpu.VMEM((8 + q, conv_dim), F32),
                        pltpu.VMEM((SSM_GROUPS, SSM_STATE, gw), F32)],
        compiler_params=_params("parallel", "arbitrary"),
        name="ssd",
    )(proj, proj, proj, conv_w, conv_b.reshape(1, -1),
      dt_bias.reshape(1, -1), dt_bias.reshape(-1, 1), a_log.reshape(1, -1), a_log.reshape(-1, 1),
      jnp.repeat(d_skip, SSM_HEAD_DIM).reshape(1, -1), ssm_norm.reshape(1, -1), rexp, ltri, utri)


CONF_HALO = 32


def _conf_body(u_ref, w_ref, b_ref, g_ref, beta_ref, o_ref, tail_ref, buf):
    t = pl.program_id(1)
    tm, d = o_ref.shape

    @pl.when(t == 0)
    def _():
        buf[0:CONF_HALO, :] = jnp.zeros((CONF_HALO, d), F32)

    u = u_ref[...]
    a = u[:, :d] * jax.nn.sigmoid(u[:, d:])
    buf[CONF_HALO:CONF_HALO + tm, :] = a
    w = w_ref[...]
    off = CONF_HALO - (CONF_CONV - 1)
    acc = b_ref[...] + buf[off:off + tm, :] * w[0:1]
    for k in range(1, CONF_CONV):
        acc = acc + buf[off + k:off + k + tm, :] * w[k:k + 1]
    buf[0:CONF_HALO, :] = a[tm - CONF_HALO:tm]
    tail_ref[0] = a[tm - CONF_HALO:tm]
    mu = jnp.mean(acc, axis=-1, keepdims=True)
    cen = acc - mu
    var = jnp.mean(cen * cen, axis=-1, keepdims=True)
    y = cen * lax.rsqrt(var + EPS) * g_ref[...] + beta_ref[...]
    o_ref[...] = (y * jax.nn.sigmoid(y)).astype(BF16)


def conformer_mid_call(u, dw_w, dw_b, ln_g, ln_b, bsz, seq, tm):
    d = dw_w.shape[1]
    nt = seq // tm
    const = lambda shape: pl.BlockSpec(shape, lambda b, t: (0,) * len(shape))
    return pl.pallas_call(
        _conf_body,
        grid=(bsz, nt),
        in_specs=[pl.BlockSpec((tm, 2 * d), lambda b, t: (b * nt + t, 0)),
                  const((CONF_CONV, d)), const((1, d)), const((1, d)), const((1, d))],
        out_specs=[pl.BlockSpec((tm, d), lambda b, t: (b * nt + t, 0)),
                   pl.BlockSpec((1, CONF_HALO, d), lambda b, t: (b, 0, 0))],
        out_shape=[jax.ShapeDtypeStruct((bsz * seq, d), BF16),
                   jax.ShapeDtypeStruct((bsz, CONF_HALO, d), F32)],
        scratch_shapes=[pltpu.VMEM((CONF_HALO + tm, d), F32)],
        compiler_params=_params("parallel", "arbitrary"),
        name="conformer_mid",
    )(u, dw_w, dw_b.reshape(1, d), ln_g.reshape(1, d), ln_b.reshape(1, d))


def _ffn_act_body(u_ref, w_ref, b_ref, o_ref, buf):
    t = pl.program_id(1)
    tm, dff = o_ref.shape

    @pl.when(t == 0)
    def _():
        buf[0:8, :] = jnp.zeros((8, 2 * dff), F32)

    u = u_ref[...]
    buf[8:8 + tm, :] = u
    w = w_ref[...]
    y = u * w[2:3] + buf[7:7 + tm, :] * w[1:2] + buf[6:6 + tm, :] * w[0:1] + b_ref[...]
    buf[0:8, :] = u[tm - 8:tm]
    o_ref[...] = (jax.nn.gelu(y[:, :dff], approximate=True) * y[:, dff:]).astype(BF16)


def ffn_act_call(u, conv_w, conv_b, bsz, seq, tm):
    d2 = conv_w.shape[1]
    nt = seq // tm
    const = lambda shape: pl.BlockSpec(shape, lambda b, t: (0,) * len(shape))
    return pl.pallas_call(
        _ffn_act_body,
        grid=(bsz, nt),
        in_specs=[pl.BlockSpec((tm, d2), lambda b, t: (b * nt + t, 0)),
                  const((FFN_CONV, d2)), const((1, d2))],
        out_specs=pl.BlockSpec((tm, d2 // 2), lambda b, t: (b * nt + t, 0)),
        out_shape=jax.ShapeDtypeStruct((bsz * seq, d2 // 2), BF16),
        scratch_shapes=[pltpu.VMEM((8 + tm, d2), F32)],
        compiler_params=_params("parallel", "arbitrary"),
        name="ffn_act",
    )(u, conv_w, conv_b.reshape(1, d2))


def _rms(x, g):
    return x * lax.rsqrt(jnp.mean(x * x, axis=-1, keepdims=True) + EPS) * g


def _masked_softmax(s, mask):
    s = jnp.where(mask, s, -jnp.inf)
    m = jnp.max(s, axis=-1, keepdims=True)
    m = jnp.where(jnp.isfinite(m), m, 0.0)
    e = jnp.exp(s - m)
    d = jnp.sum(e, axis=-1, keepdims=True)
    return e / jnp.where(d > 0, d, 1.0)


def _sample_ssd_step(z, xbc, dt_raw, conv_hist, h0, conv_w, conv_b, dt_bias, a_log, d_skip, ssm_norm):
    bsz = z.shape[0]
    n_heads = dt_bias.shape[0]
    d_inner = n_heads * SSM_HEAD_DIM
    hpg = n_heads // SSM_GROUPS
    xh = jnp.concatenate([conv_hist, xbc[:, None, :]], axis=1)
    new_hist = xh[:, 1:]
    conv = jnp.sum(xh * conv_w[None], axis=1) + conv_b
    xc = conv * jax.nn.sigmoid(conv)
    xs = xc[:, :d_inner].reshape(bsz, SSM_GROUPS, hpg, SSM_HEAD_DIM)
    bm = xc[:, d_inner:d_inner + SSM_GROUPS * SSM_STATE].reshape(bsz, SSM_GROUPS, SSM_STATE)
    cm = xc[:, d_inner + SSM_GROUPS * SSM_STATE:].reshape(bsz, SSM_GROUPS, SSM_STATE)
    dt = jax.nn.softplus(dt_raw + dt_bias).reshape(bsz, SSM_GROUPS, hpg)
    a = (-jnp.exp(a_log)).reshape(SSM_GROUPS, hpg)
    dec = jnp.exp(dt * a)
    h0g = h0.reshape(bsz, SSM_GROUPS, hpg, SSM_HEAD_DIM, SSM_STATE)
    h_new = dec[..., None, None] * h0g + (dt[..., None] * xs)[..., None] * bm[:, :, None, None, :]
    y = jnp.sum(h_new * cm[:, :, None, None, :], axis=-1)
    y = y + d_skip.reshape(SSM_GROUPS, hpg)[None, :, :, None] * xs
    y = y.reshape(bsz, d_inner) * (z * jax.nn.sigmoid(z))
    y = _rms(y.reshape(bsz, SSM_GROUPS, d_inner // SSM_GROUPS),
             ssm_norm.reshape(SSM_GROUPS, d_inner // SSM_GROUPS)).reshape(bsz, d_inner)
    return y, new_hist, h_new.reshape(h0.shape)


def _compress_rows_jnp(rows, pe, w1, b1, w2, b2):
    bsz, t = rows.shape[:2]
    r = CMP_LEN // CMP_STRIDE
    nch = t // CMP_STRIDE
    n_cmp = nch - r + 1
    ch = rows[:, :nch * CMP_STRIDE].reshape(bsz, nch, CMP_STRIDE, NSA_KV, NSA_HEAD_DIM)
    blk = jnp.concatenate([ch[:, k:k + n_cmp] for k in range(r)], axis=2)
    blk = blk + pe[None, None, :, None, :]
    flat = blk.transpose(0, 1, 3, 2, 4).reshape(bsz, n_cmp, NSA_KV, CMP_LEN * NSA_HEAD_DIM)
    hid = jax.nn.silu(flat @ w1 + b1)
    return hid @ w2 + b2


def _sample_nsa(q, gates, kvc, kvs, kvw, cache_cmp, cache_slc, cache_win, page_table, cmp_w):
    db = q.shape[0]
    past_len = page_table.shape[1] * cache_cmp.shape[1]
    pe, w1, b1, w2, b2 = cmp_w
    gather = lambda cache: cache[page_table].reshape((db, past_len) + cache.shape[2:])
    rows_cmp = jnp.concatenate([gather(cache_cmp), kvc], axis=1)
    rows_slc = jnp.concatenate([gather(cache_slc), kvs], axis=1)
    kc = _compress_rows_jnp(rows_cmp[:, :, 0], pe[0], w1[0], b1[0], w2[0], b2[0])
    vc = _compress_rows_jnp(rows_cmp[:, :, 1], pe[1], w1[1], b1[1], w2[1], b2[1])
    n_cmp = kc.shape[1]
    cmp_end = jnp.arange(n_cmp) * CMP_STRIDE + CMP_LEN - 1
    t_all = rows_slc.shape[1]
    n_slc = -(-t_all // SLC_LEN)
    slc = jnp.pad(rows_slc, [(0, 0), (0, n_slc * SLC_LEN - t_all), (0, 0), (0, 0), (0, 0)])
    slc = slc.reshape(db, n_slc, SLC_LEN, 2, NSA_KV, NSA_HEAD_DIM).transpose(0, 4, 1, 2, 3, 5)
    wkv = jnp.concatenate([cache_win, kvw], axis=1)
    w_rows = cache_win.shape[1]
    wpos = past_len - w_rows + jnp.arange(w_rows + 1)
    qpos = past_len
    qg = (q[:, 0] * NSA_HEAD_DIM ** -0.5).reshape(db, NSA_KV, NSA_HPG, NSA_HEAD_DIM)
    s_c = jnp.einsum('bghd,bngd->bghn', qg, kc)
    p_c = _masked_softmax(s_c, (cmp_end <= qpos)[None, None, None, :])
    o_c = jnp.einsum('bghn,bngd->bghd', p_c, vc)
    imp = jnp.einsum('bghn,nj->bgj', p_c, jnp.asarray(_overlap_np(n_cmp, n_slc)),
                     precision=lax.Precision.HIGHEST)
    j = jnp.arange(n_slc)
    qblk = qpos // SLC_LEN
    forced = (j == 0) | (j == qblk) | (j == qblk - 1)
    score = jnp.where((j * SLC_LEN <= qpos)[None, None, :], imp + FORCE_SCORE * forced[None, None, :], -jnp.inf)
    _, idx = lax.top_k(score, min(SLC_TOPN, n_slc))
    bi = jnp.arange(db)[:, None, None]
    gi = jnp.arange(NSA_KV)[None, :, None]
    sel = slc[bi, gi, idx]
    s_s = jnp.einsum('bghd,bgnld->bghnl', qg, sel[..., 0, :])
    kpos = idx[..., None] * SLC_LEN + jnp.arange(SLC_LEN)
    m_s = (kpos <= qpos)[:, :, None]
    p_s = _masked_softmax(s_s.reshape(db, NSA_KV, NSA_HPG, -1), m_s.reshape(db, NSA_KV, 1, -1)).reshape(s_s.shape)
    o_s = jnp.einsum('bghnl,bgnld->bghd', p_s, sel[..., 1, :])
    s_w = jnp.einsum('bghd,bsgd->bghs', qg, wkv[:, :, 0])
    dpos = qpos - wpos
    p_w = _masked_softmax(s_w, ((dpos >= 0) & (dpos < WINDOW) & (wpos >= 0))[None, None, None, :])
    o_w = jnp.einsum('bghs,bsgd->bghd', p_w, wkv[:, :, 1])
    gt = jax.nn.sigmoid(gates).reshape(db, NSA_KV, NSA_HPG, 3)
    o = gt[..., 0:1] * o_c + gt[..., 1:2] * o_s + gt[..., 2:3] * o_w
    return o.reshape(db, NSA_HEADS * NSA_HEAD_DIM)


def _prep_w_in(w_in, d_inner, conv_dim, n_heads):
    nq = NSA_HEADS * NSA_HEAD_DIM
    nkv = 2 * NSA_KV * NSA_HEAD_DIM
    o = np.cumsum([0, d_inner, conv_dim, n_heads, nq, nkv, nkv, nkv, 3 * NSA_HEADS])
    z, xbc, dt, q, kvc, kvs, kvw, gates = (w_in[:, o[k]:o[k + 1]] for k in range(8))
    small = jnp.concatenate([dt, gates], axis=1)
    small = jnp.pad(small, ((0, 0), (0, LANES - small.shape[1])))
    return jnp.concatenate([z, q, xbc, kvc, kvs, kvw, small], axis=1).astype(BF16)


def _pick_tile(m, pref):
    return pref if m % pref == 0 else m


def kernel(x_prompt, x_sample, state_ssm, state_ssm_conv, cache_cmp, cache_slc, cache_win, state_conf_conv, state_ffn_conv, page_table, norm_g, ab_w_in, ab_conv_w, ab_conv_b, ab_dt_bias, ab_a_log, ab_d_skip, ab_ssm_norm, ab_cmp_pe, ab_cmp_w1, ab_cmp_b1, ab_cmp_w2, ab_cmp_b2, ab_w_out, c_w_pw1, c_b_pw1, c_dw_w, c_dw_b, c_ln_g, c_ln_b, c_w_pw2, c_b_pw2, ffn_w_up, ffn_conv_w, ffn_conv_b, ffn_w_down):
    bsz, seq, d_model = x_prompt.shape
    db, dseq, _ = x_sample.shape
    assert dseq == 1
    depth = norm_g.shape[0]
    n_heads = ab_dt_bias.shape[1]
    d_inner = n_heads * SSM_HEAD_DIM
    conv_dim = ab_conv_w.shape[2]
    d_ff2 = ffn_w_up.shape[2]
    d_conf = c_dw_w.shape[2] if c_dw_w.shape[0] else d_model
    past_len = page_table.shape[1] * cache_cmp.shape[2]
    mp, ms = bsz * seq, db
    hp = x_prompt.reshape(mp, d_model)
    hs = x_sample.reshape(ms, d_model)
    tm_p = _pick_tile(mp, 1024)
    tm_post = _pick_tile(mp, 512)
    tm_seq = _pick_tile(seq, 256)
    nkv = 2 * NSA_KV * NSA_HEAD_DIM

    cos_p, sin_p = _rope_tables(jnp.arange(seq, dtype=jnp.int32))
    cos_s, sin_s = _rope_tables(jnp.full((ms,), past_len, jnp.int32))
    nchk = seq // CMP_STRIDE
    n_cmp = nchk - CMP_LEN // CMP_STRIDE + 1

    outs = {k: [] for k in ("ssm_p", "ssm_s", "sconv_p", "sconv_s", "cmp_p", "cmp_s", "slc_p", "slc_s",
                            "win_p", "win_s", "conf_p", "conf_s", "ffn_p", "ffn_s")}
    kvshape = (2, NSA_KV, NSA_HEAD_DIM)
    for layer in range(depth):
        g = norm_g[layer]
        if layer % 2 == 0:
            a = layer // 2
            w_in = _prep_w_in(ab_w_in[a], d_inner, conv_dim, n_heads)
            n_proj = w_in.shape[1]
            zero_b = jnp.zeros((n_proj,), F32)
            w_out = ab_w_out[a].astype(BF16)
            zero_d = jnp.zeros((d_model,), F32)
            proj = norm_matmul(hp, g[0], w_in, zero_b, tm_p, 640)
            q_rot, q_swp, kv_pad, kvc, kvs, kvw = rope_call(proj, cos_p, sin_p, _pick_tile(seq, 512))
            ch = kvc.reshape(bsz, nchk, CMP_STRIDE, 4, NSA_HEAD_DIM).transpose(0, 3, 1, 2, 4)
            ch = ch.reshape(bsz, 4, nchk, CMP_STRIDE * NSA_HEAD_DIM)
            pe_flat = ab_cmp_pe[a].reshape(2, 1, CMP_LEN * NSA_HEAD_DIM)
            w2p = jnp.pad(ab_cmp_w2[a], ((0, 0), (0, 0), (0, LANES - NSA_HEAD_DIM))).astype(BF16)
            b2p = jnp.pad(ab_cmp_b2[a], ((0, 0), (0, LANES - NSA_HEAD_DIM))).reshape(2, 1, LANES)
            ckv = compress_call(ch, pe_flat, ab_cmp_w1[a].astype(BF16), ab_cmp_b1[a].reshape(2, 1, -1), w2p, b2p)
            y_nsa = nsa_prompt_call(q_rot, q_swp, kv_pad, ckv, proj, bsz, seq, n_cmp)
            y_ssm, st = ssd_call(proj, ab_conv_w[a], ab_conv_b[a], ab_dt_bias[a], ab_a_log[a],
                                 ab_d_skip[a], ab_ssm_norm[a], bsz, seq)
            hp = matmul_post([y_ssm, y_nsa], [w_out[:d_inner], w_out[d_inner:]], zero_d, hp, g[1], tm_post)
            hpg = n_heads // SSM_GROUPS
            outs["ssm_p"].append(st.reshape(bsz, SSM_GROUPS, SSM_STATE, hpg, SSM_HEAD_DIM)
                                 .transpose(0, 1, 3, 4, 2).reshape(bsz, n_heads, SSM_HEAD_DIM, SSM_STATE))
            tail_p = proj.reshape(bsz, seq, n_proj)[:, seq - (SSM_CONV - 1):]
            outs["sconv_p"].append(tail_p[:, :, d_inner + 1024:d_inner + 1024 + conv_dim])
            outs["cmp_p"].append(kvc.reshape((bsz, seq) + kvshape))
            outs["slc_p"].append(kvs.reshape((bsz, seq) + kvshape))
            outs["win_p"].append(kvw.reshape((bsz, seq) + kvshape)[:, seq - min(WINDOW, seq):])
            proj_s = norm_matmul(hs, g[0], w_in, zero_b, ms, 640)
            q_s, _, _, kvc_s, kvs_s, kvw_s = rope_call(proj_s, cos_s, sin_s, ms)
            z_s = proj_s[:, :d_inner]
            xbc_s = proj_s[:, d_inner + 1024:d_inner + 1024 + conv_dim]
            dt_s = proj_s[:, n_proj - LANES:n_proj - LANES + n_heads]
            gates_s = proj_s[:, n_proj - LANES + n_heads:n_proj - LANES + n_heads + 3 * NSA_HEADS]
            ys_ssm, hist_s, h_s = _sample_ssd_step(z_s, xbc_s, dt_s, state_ssm_conv[a], state_ssm[a],
                                                   ab_conv_w[a], ab_conv_b[a], ab_dt_bias[a], ab_a_log[a],
                                                   ab_d_skip[a], ab_ssm_norm[a])
            ys_nsa = sample_nsa_call(page_table, cache_cmp[a], cache_slc[a], cache_win[a], q_s, kvs_s, kvw_s,
                                     gates_s, ab_cmp_pe[a], ab_cmp_w1[a], ab_cmp_b1[a], ab_cmp_w2[a], ab_cmp_b2[a])
            kvc_s = kvc_s.reshape((ms, 1) + kvshape)
            kvs_s = kvs_s.reshape((ms, 1) + kvshape)
            kvw_s = kvw_s.reshape((ms, 1) + kvshape)
            hs = matmul_post([ys_ssm.astype(BF16), ys_nsa.astype(BF16)], [w_out[:d_inner], w_out[d_inner:]],
                             zero_d, hs, g[1], ms)
            outs["ssm_s"].append(h_s)
            outs["sconv_s"].append(hist_s)
            outs["cmp_s"].append(kvc_s)
            outs["slc_s"].append(kvs_s)
            outs["win_s"].append(kvw_s)
        else:
            c = layer // 2
            w1 = c_w_pw1[c].astype(BF16)
            w2 = c_w_pw2[c].astype(BF16)
            u = norm_matmul(hp, g[0], w1, c_b_pw1[c], tm_p, 512)
            act, tail = conformer_mid_call(u, c_dw_w[c], c_dw_b[c], c_ln_g[c], c_ln_b[c], bsz, seq, tm_seq)
            hp = matmul_post([act], [w2], c_b_pw2[c], hp, g[1], tm_post)
            outs["conf_p"].append(tail[:, CONF_HALO - (CONF_CONV - 1):])
            u_s = norm_matmul(hs, g[0], w1, c_b_pw1[c], ms, 512)
            a_s = u_s[:, :d_conf] * jax.nn.sigmoid(u_s[:, d_conf:])
            ah = jnp.concatenate([state_conf_conv[c], a_s[:, None, :]], axis=1)
            conv = jnp.sum(ah * c_dw_w[c][None], axis=1) + c_dw_b[c]
            mu = jnp.mean(conv, axis=-1, keepdims=True)
            var = jnp.mean(jnp.square(conv - mu), axis=-1, keepdims=True)
            yln = (conv - mu) * lax.rsqrt(var + EPS) * c_ln_g[c] + c_ln_b[c]
            hs = matmul_post([(yln * jax.nn.sigmoid(yln)).astype(BF16)], [w2], c_b_pw2[c], hs, g[1], ms)
            outs["conf_s"].append(ah[:, 1:])
        w_up = ffn_w_up[layer].astype(BF16)
        w_dn = ffn_w_down[layer].astype(BF16)
        zero_up = jnp.zeros((d_ff2,), F32)
        zero_d = jnp.zeros((d_model,), F32)
        u = norm_matmul(hp, g[2], w_up, zero_up, tm_p, 512)
        act = ffn_act_call(u, ffn_conv_w[layer], ffn_conv_b[layer], bsz, seq, tm_seq)
        hp = matmul_post([act], [w_dn], zero_d, hp, g[3], tm_post)
        outs["ffn_p"].append(u.reshape(bsz, seq, d_ff2)[:, seq - (FFN_CONV - 1):])
        u_s = norm_matmul(hs, g[2], w_up, zero_up, ms, 512)
        uh = jnp.concatenate([state_ffn_conv[layer], u_s[:, None, :]], axis=1)
        y_s = jnp.sum(uh * ffn_conv_w[layer][None], axis=1) + ffn_conv_b[layer]
        act_s = jax.nn.gelu(y_s[:, :d_ff2 // 2], approximate=True) * y_s[:, d_ff2 // 2:]
        hs = matmul_post([act_s.astype(BF16)], [w_dn], zero_d, hs, g[3], ms)
        outs["ffn_s"].append(uh[:, 1:])

    st = lambda k: jnp.stack(outs[k])
    return (hp.reshape(bsz, seq, d_model), hs.reshape(db, dseq, d_model),
            st("ssm_p"), st("ssm_s"), st("sconv_p"), st("sconv_s"), st("cmp_p"), st("cmp_s"),
            st("slc_p"), st("slc_s"), st("win_p"), st("win_s"), st("conf_p"), st("conf_s"),
            st("ffn_p"), st("ffn_s"))
```

```python
import functools
import math

import jax
import jax.numpy as jnp
import numpy as np
from jax import lax
from jax.experimental import pallas as pl
from jax.experimental.pallas import tpu as pltpu

F32 = jnp.float32
BF16 = jnp.bfloat16

EPS = 1e-6
LANES = 128
VMEM_LIMIT = 48 * 1024 * 1024

SSM_HEAD_DIM = 64
SSM_GROUPS = 4
SSM_STATE = 128
SSM_CONV = 4
SSM_CHUNK = 128
NSA_HEADS = 16
NSA_KV = 2
NSA_HPG = NSA_HEADS // NSA_KV
NSA_HEAD_DIM = 64
CMP_LEN = 32
CMP_STRIDE = 16
SLC_LEN = 64
SLC_TOPN = 16
WINDOW = 512
Q_BLOCK = 128
ROPE_THETA = 10000.0
FORCE_SCORE = 1e4
CONF_CONV = 31
FFN_CONV = 3


def _params(*sem):
    return pltpu.CompilerParams(dimension_semantics=sem, vmem_limit_bytes=VMEM_LIMIT)


def _split3(v):
    hi = v.astype(BF16)
    r1 = v - hi.astype(F32)
    mid = r1.astype(BF16)
    lo = (r1 - mid.astype(F32)).astype(BF16)
    return hi, mid, lo


def _dot(a, b):
    return jnp.dot(a, b, preferred_element_type=F32)


def _dot_nt(a, b):
    return lax.dot_general(a, b, (((1,), (1,)), ((), ())), preferred_element_type=F32)


def _dot3(v, m_bf16):
    hi, mid, lo = _split3(v)
    return _dot(hi, m_bf16) + _dot(mid, m_bf16) + _dot(lo, m_bf16)


def _dot3_left(m_bf16, v):
    hi, mid, lo = _split3(v)
    return _dot(m_bf16, hi) + _dot(m_bf16, mid) + _dot(m_bf16, lo)


def _norm_matmul_body(x_ref, g_ref, w_ref, b_ref, o_ref, xn_ref):
    @pl.when(pl.program_id(1) == 0)
    def _():
        x = x_ref[...]
        ms = jnp.mean(x * x, axis=-1, keepdims=True)
        xn_ref[...] = (x * lax.rsqrt(ms + EPS) * g_ref[...]).astype(BF16)

    o_ref[...] = _dot(xn_ref[...], w_ref[...]) + b_ref[...]


def norm_matmul(x, g, w, b, tm, tn):
    m, k = x.shape
    n = w.shape[1]
    return pl.pallas_call(
        _norm_matmul_body,
        grid=(m // tm, n // tn),
        in_specs=[pl.BlockSpec((tm, k), lambda i, j: (i, 0)),
                  pl.BlockSpec((1, k), lambda i, j: (0, 0)),
                  pl.BlockSpec((k, tn), lambda i, j: (0, j)),
                  pl.BlockSpec((1, tn), lambda i, j: (0, j))],
        out_specs=pl.BlockSpec((tm, tn), lambda i, j: (i, j)),
        out_shape=jax.ShapeDtypeStruct((m, n), F32),
        scratch_shapes=[pltpu.VMEM((tm, k), BF16)],
        compiler_params=_params("parallel", "arbitrary"),
        name="norm_matmul",
    )(x, g.reshape(1, k), w, b.reshape(1, n))


def _matmul_post_body(n_pairs, *refs):
    a_refs = refs[:n_pairs]
    w_refs = refs[n_pairs:2 * n_pairs]
    b_ref, r_ref, g_ref, o_ref = refs[2 * n_pairs:]
    acc = _dot(a_refs[0][...], w_refs[0][...])
    for a_ref, w_ref in zip(a_refs[1:], w_refs[1:]):
        acc = acc + _dot(a_ref[...], w_ref[...])
    acc = acc + b_ref[...]
    ms = jnp.mean(acc * acc, axis=-1, keepdims=True)
    o_ref[...] = r_ref[...] + acc * lax.rsqrt(ms + EPS) * g_ref[...]


def matmul_post(a_list, w_list, b, resid, g, tm):
    m, n = resid.shape
    n_pairs = len(a_list)
    in_specs = [pl.BlockSpec((tm, a.shape[1]), lambda i: (i, 0)) for a in a_list]
    in_specs += [pl.BlockSpec(w.shape, lambda i: (0, 0)) for w in w_list]
    in_specs += [pl.BlockSpec((1, n), lambda i: (0, 0)),
                 pl.BlockSpec((tm, n), lambda i: (i, 0)),
                 pl.BlockSpec((1, n), lambda i: (0, 0))]
    return pl.pallas_call(
        functools.partial(_matmul_post_body, n_pairs),
        grid=(m // tm,),
        in_specs=in_specs,
        out_specs=pl.BlockSpec((tm, n), lambda i: (i, 0)),
        out_shape=jax.ShapeDtypeStruct((m, n), F32),
        compiler_params=_params("parallel"),
        name="matmul_post",
    )(*a_list, *w_list, b.reshape(1, n), resid, g.reshape(1, n))


def _rope_tables(pos):
    half = NSA_HEAD_DIM // 2
    inv = ROPE_THETA ** (-jnp.arange(half, dtype=F32) / half)
    ang = pos.astype(F32)[:, None] * inv[None, :]
    cos, sin = jnp.cos(ang), jnp.sin(ang)
    reps = LANES // NSA_HEAD_DIM
    cosf = jnp.tile(jnp.concatenate([cos, cos], axis=-1), (1, reps))
    sinf = jnp.tile(jnp.concatenate([-sin, sin], axis=-1), (1, reps))
    return cosf, sinf


def _rotate_half_partner(x):
    w = x.shape[-1]
    lane = lax.broadcasted_iota(jnp.int32, x.shape, x.ndim - 1)
    first = (lane % NSA_HEAD_DIM) < (NSA_HEAD_DIM // 2)
    return jnp.where(first, pltpu.roll(x, w - NSA_HEAD_DIM // 2, x.ndim - 1),
                     pltpu.roll(x, NSA_HEAD_DIM // 2, x.ndim - 1))


def _swap_head_pair(x):
    w = x.shape[-1]
    lane = lax.broadcasted_iota(jnp.int32, x.shape, x.ndim - 1)
    first = (lane % LANES) < NSA_HEAD_DIM
    return jnp.where(first, pltpu.roll(x, w - NSA_HEAD_DIM, x.ndim - 1),
                     pltpu.roll(x, NSA_HEAD_DIM, x.ndim - 1))


def _rope_body(q_ref, kv_ref, cos_ref, sin_ref, q_o, qs_o, kv_o, kvc_o, kvs_o, kvw_o):
    cos = cos_ref[...]
    sin = sin_ref[...]
    q = q_ref[...]
    nq = q.shape[1] // LANES
    cq = jnp.concatenate([cos] * nq, axis=1)
    sq = jnp.concatenate([sin] * nq, axis=1)
    qr = (q * cq + _rotate_half_partner(q) * sq) * (NSA_HEAD_DIM ** -0.5)
    q_o[...] = qr.astype(BF16)
    qs_o[...] = _swap_head_pair(qr).astype(BF16)

    kv = kv_ref[...]
    lane = lax.broadcasted_iota(jnp.int32, (kv.shape[0], LANES), 1)
    lo = lane < NSA_HEAD_DIM
    zero = jnp.zeros((kv.shape[0], LANES), F32)
    padded = []
    for c, full_o in enumerate((kvc_o, kvs_o, kvw_o)):
        k = kv[:, c * 256:c * 256 + LANES]
        v = kv[:, c * 256 + LANES:(c + 1) * 256]
        kr = k * cos + _rotate_half_partner(k) * sin
        full_o[...] = jnp.concatenate([kr, v], axis=1)
        if c > 0:
            kr_sw = pltpu.roll(kr, NSA_HEAD_DIM, 1)
            v_sw = pltpu.roll(v, NSA_HEAD_DIM, 1)
            padded += [jnp.where(lo, kr, zero), jnp.where(lo, kr_sw, zero),
                       jnp.where(lo, v, zero), jnp.where(lo, v_sw, zero)]
    kv_o[...] = jnp.concatenate(padded, axis=1).astype(BF16)


def rope_call(proj, cosf, sinf, tm):
    m = proj.shape[0]
    nblk = cosf.shape[0] // tm
    f32s = jax.ShapeDtypeStruct((m, 256), F32)
    return pl.pallas_call(
        _rope_body,
        grid=(m // tm,),
        in_specs=[pl.BlockSpec((tm, 1024), lambda i: (i, 2)),
                  pl.BlockSpec((tm, 768), lambda i: (i, 8)),
                  pl.BlockSpec((tm, LANES), lambda i: (i % nblk, 0)),
                  pl.BlockSpec((tm, LANES), lambda i: (i % nblk, 0))],
        out_specs=[pl.BlockSpec((tm, 1024), lambda i: (i, 0)),
                   pl.BlockSpec((tm, 1024), lambda i: (i, 0)),
                   pl.BlockSpec((tm, 1024), lambda i: (i, 0)),
                   pl.BlockSpec((tm, 256), lambda i: (i, 0)),
                   pl.BlockSpec((tm, 256), lambda i: (i, 0)),
                   pl.BlockSpec((tm, 256), lambda i: (i, 0))],
        out_shape=[jax.ShapeDtypeStruct((m, 1024), BF16), jax.ShapeDtypeStruct((m, 1024), BF16),
                   jax.ShapeDtypeStruct((m, 1024), BF16), f32s, f32s, f32s],
        compiler_params=_params("parallel"),
        name="rope",
    )(proj, proj, cosf, sinf)


def _compress_body(ch_ref, pe_ref, w1_ref, b1_ref, w2_ref, b2_ref, o_ref):
    ch = ch_ref[0, 0]
    nc, half = ch.shape
    pe = pe_ref[0]
    top = _dot((ch + pe[:, :half]).astype(BF16), w1_ref[0, :half, :])
    bot = _dot((ch + pe[:, half:]).astype(BF16), w1_ref[0, half:, :])
    hid = top + pltpu.roll(bot, nc - 1, 0) + b1_ref[0]
    hid = hid * jax.nn.sigmoid(hid)
    o_ref[0, 0] = _dot(hid.astype(BF16), w2_ref[0]) + b2_ref[0]


def compress_call(ch, pe_flat, w1, b1, w2p, b2p):
    bsz, _, nc, half = ch.shape
    hidden = w1.shape[-1]
    return pl.pallas_call(
        _compress_body,
        grid=(bsz, 4),
        in_specs=[pl.BlockSpec((1, 1, nc, half), lambda b, q: (b, q, 0, 0)),
                  pl.BlockSpec((1, 1, 2 * half), lambda b, q: (q // 2, 0, 0)),
                  pl.BlockSpec((1, 2 * half, hidden), lambda b, q: (q // 2, 0, 0)),
                  pl.BlockSpec((1, 1, hidden), lambda b, q: (q // 2, 0, 0)),
                  pl.BlockSpec((1, hidden, LANES), lambda b, q: (q // 2, 0, 0)),
                  pl.BlockSpec((1, 1, LANES), lambda b, q: (q // 2, 0, 0))],
        out_specs=pl.BlockSpec((1, 1, nc, LANES), lambda b, q: (b, q, 0, 0)),
        out_shape=jax.ShapeDtypeStruct((bsz, 4, nc, LANES), F32),
        compiler_params=_params("parallel", "parallel"),
        name="compress",
    )(ch, pe_flat, w1, b1, w2p, b2p)


def _overlap_np(nc, ns):
    i = np.arange(nc)[:, None]
    j = np.arange(ns)[None, :]
    return ((i * CMP_STRIDE < (j + 1) * SLC_LEN) & (i * CMP_STRIDE + CMP_LEN > j * SLC_LEN)).astype(np.float32)


def _nsa_body(seq, n_cmp, ck, wk, n_dt, q_ref, qs_ref, kvs_ref, kvw_ref, ckv_ref, small_ref, ovt_ref, o_ref):
    i = pl.program_id(1)
    tq = Q_BLOCK
    n_slc = seq // SLC_LEN
    nsp = -(-n_slc // LANES) * LANES
    nc = ckv_ref.shape[2]
    npair = NSA_HPG // 2
    rows = NSA_HPG * tq
    tpos = i * tq + lax.broadcasted_iota(jnp.int32, (tq, 1), 0)

    def mask_heads(s, mask):
        n = s.shape[-1]
        return jnp.where(mask[None], s.reshape(NSA_HPG, tq, n), -jnp.inf).reshape(rows, n)

    tpos_t = i * tq + lax.broadcasted_iota(jnp.int32, (1, tq), 1)
    q = q_ref[...]
    qs = qs_ref[...]
    gates = jax.nn.sigmoid(small_ref[...])

    def gated_pair_tiles(o, g, branch):
        tiles = []
        for p in range(npair):
            c_even = n_dt + 3 * (g * NSA_HPG + 2 * p) + branch
            even = o[p * tq:(p + 1) * tq] * gates[:, c_even:c_even + 1]
            odd = o[(npair + p) * tq:(npair + p + 1) * tq] * gates[:, c_even + 3:c_even + 4]
            tiles.append(even + pltpu.roll(odd, NSA_HEAD_DIM, 1))
        return tiles

    out_tiles, qgs, sels = [], [], []
    for g in range(NSA_KV):
        qg = jnp.concatenate([q[:, (g * npair + p) * LANES:(g * npair + p + 1) * LANES] for p in range(npair)]
                             + [qs[:, (g * npair + p) * LANES:(g * npair + p + 1) * LANES] for p in range(npair)],
                             axis=0)
        kc = ckv_ref[0, g].astype(BF16)
        vc = ckv_ref[0, NSA_KV + g].astype(BF16)
        s_c = _dot_nt(qg, kc)
        cidx = lax.broadcasted_iota(jnp.int32, (1, nc), 1)
        m_c = (cidx * CMP_STRIDE + (CMP_LEN - 1) <= tpos) & (cidx < n_cmp)
        s_c = mask_heads(s_c, m_c)
        mx = jnp.max(s_c, axis=-1, keepdims=True)
        mx = jnp.where(mx == -jnp.inf, 0.0, mx)
        e = jnp.exp(s_c - mx)
        d = jnp.sum(e, axis=-1, keepdims=True)
        p_c = e / jnp.where(d > 0, d, 1.0)
        tiles = gated_pair_tiles(_dot(p_c.astype(BF16), vc), g, 0)
        p_sum = p_c[0:tq]
        for h in range(1, NSA_HPG):
            p_sum = p_sum + p_c[h * tq:(h + 1) * tq]
        ovt = ovt_ref[...]
        imp_t = sum(_dot_nt(ovt, part) for part in _split3(p_sum))
        jidx = lax.broadcasted_iota(jnp.int32, (n_slc, 1), 0)
        qblk = tpos_t // SLC_LEN
        valid = jidx * SLC_LEN <= tpos_t
        forced = (jidx == 0) | (jidx == qblk) | (jidx == qblk - 1)
        score = jnp.where(valid, imp_t + FORCE_SCORE * forced.astype(F32), -jnp.inf)
        rank = jnp.zeros((n_slc, tq), F32)
        for k in range(n_slc):
            row = score[k:k + 1, :]
            ahead = (row > score) | ((row == score) & (jidx > k))
            rank = rank + ahead.astype(F32)
        sel_t = (rank < float(min(SLC_TOPN, n_slc))).astype(F32)
        if nsp > n_slc:
            sel_t = jnp.concatenate([sel_t, jnp.zeros((nsp - n_slc, tq), F32)], axis=0)
        qgs.append(qg)
        sels.append(sel_t.T.astype(BF16))

        w_start = jnp.clip((i - WINDOW // tq) * tq, 0, seq - wk)
        w_start = pl.multiple_of(w_start, tq)
        k_w = kvw_ref[pl.ds(w_start, wk), g * LANES:(g + 1) * LANES]
        v_w = kvw_ref[pl.ds(w_start, wk), (NSA_KV + g) * LANES:(NSA_KV + g + 1) * LANES]
        s_w = _dot_nt(qg, k_w)
        dpos = tpos - (w_start + lax.broadcasted_iota(jnp.int32, (1, wk), 1))
        s_w = mask_heads(s_w, (dpos >= 0) & (dpos < WINDOW))
        e_w = jnp.exp(s_w - jnp.max(s_w, axis=-1, keepdims=True))
        o_w = _dot(e_w.astype(BF16), v_w) / jnp.sum(e_w, axis=-1, keepdims=True)
        out_tiles.append([t + u for t, u in zip(tiles, gated_pair_tiles(o_w, g, 2))])

    def slc_step(c, carry):
        start = pl.multiple_of(c * ck, ck)
        blk = c * (ck // SLC_LEN) + lax.broadcasted_iota(jnp.int32, (nsp, ck), 1) // SLC_LEN
        expand = (lax.broadcasted_iota(jnp.int32, (nsp, ck), 0) == blk).astype(BF16)
        causal = start + lax.broadcasted_iota(jnp.int32, (1, ck), 1) <= tpos
        new = []
        for g in range(NSA_KV):
            m_i, l_i, acc = carry[g]
            k_c = kvs_ref[pl.ds(start, ck), g * LANES:(g + 1) * LANES]
            v_c = kvs_ref[pl.ds(start, ck), (NSA_KV + g) * LANES:(NSA_KV + g + 1) * LANES]
            s = _dot_nt(qgs[g], k_c)
            mb = (_dot(sels[g], expand) > 0.5) & causal
            s = mask_heads(s, mb)
            m_new = jnp.maximum(m_i, jnp.max(s, axis=-1, keepdims=True))
            alpha = jnp.exp(m_i - m_new)
            p = jnp.exp(s - m_new)
            l_new = alpha * l_i + jnp.sum(p, axis=-1, keepdims=True)
            new.append((m_new, l_new, alpha * acc + _dot(p.astype(BF16), v_c)))
        return tuple(new)

    n_chunks = (i * tq + tq - 1) // ck + 1
    init = (jnp.full((rows, 1), -jnp.inf, F32), jnp.zeros((rows, 1), F32), jnp.zeros((rows, LANES), F32))
    final = lax.fori_loop(0, n_chunks, slc_step, (init,) * NSA_KV)
    tiles_all = []
    for g in range(NSA_KV):
        _, l_s, acc_s = final[g]
        tiles_all += [t + u for t, u in zip(out_tiles[g], gated_pair_tiles(acc_s / l_s, g, 1))]
    o_ref[...] = jnp.concatenate(tiles_all, axis=1).astype(BF16)


def nsa_prompt_call(q_rot, q_swp, kv_pad, ckv, proj, bsz, seq, n_cmp, n_dt):
    nq = seq // Q_BLOCK
    nc = ckv.shape[2]
    n_slc = seq // SLC_LEN
    ck = min(512, seq)
    wk = min(WINDOW + Q_BLOCK, seq)
    ovt = jnp.asarray(np.pad(_overlap_np(n_cmp, n_slc), ((0, nc - n_cmp), (0, 0))).T, BF16)
    small_blk = proj.shape[1] // LANES - 1
    return pl.pallas_call(
        functools.partial(_nsa_body, seq, n_cmp, ck, wk, n_dt),
        grid=(bsz, nq),
        in_specs=[pl.BlockSpec((Q_BLOCK, 1024), lambda b, i: (b * nq + i, 0)),
                  pl.BlockSpec((Q_BLOCK, 1024), lambda b, i: (b * nq + i, 0)),
                  pl.BlockSpec((seq, 512), lambda b, i: (b, 0)),
                  pl.BlockSpec((seq, 512), lambda b, i: (b, 1)),
                  pl.BlockSpec((1, 4, nc, LANES), lambda b, i: (b, 0, 0, 0)),
                  pl.BlockSpec((Q_BLOCK, LANES), lambda b, i: (b * nq + i, small_blk)),
                  pl.BlockSpec((n_slc, nc), lambda b, i: (0, 0))],
        out_specs=pl.BlockSpec((Q_BLOCK, 1024), lambda b, i: (b * nq + i, 0)),
        out_shape=jax.ShapeDtypeStruct((bsz * seq, 1024), BF16),
        compiler_params=_params("parallel", "arbitrary"),
        name="nsa_prompt",
    )(q_rot, q_swp, kv_pad, kv_pad, ckv, proj, ovt)


def _page_copy(cache_hbm, pt_ref, buf, sem, b, p, half, slot):
    src = cache_hbm.at[pt_ref[b, p], :, pl.ds(half * LANES, LANES)]
    return pltpu.make_async_copy(src, buf.at[slot, p, half], sem.at[slot])


def _sample_nsa_body(n_pages, n_cmp, qpos, w_rows,
                     pt_ref, ccmp_hbm, cslc_hbm, q16_ref, qpad_ref, new_ref, win_ref, gates_ref,
                     pet_ref, w1t_ref, b1_ref, w2t_ref, b2_ref, ov_ref,
                     y_ref, buf, sem, sel_scr, oc_scr):
    s = pl.program_id(0)
    nb = pl.num_programs(0) // 2
    b = s // 2
    page = buf.shape[3]
    n_rows = n_pages * page
    nch = n_rows // CMP_STRIDE
    nsp = ov_ref.shape[1]
    heads = NSA_HEADS
    row_g0 = lax.broadcasted_iota(jnp.int32, (heads, 1), 0) < NSA_HPG

    def fetch(cache_hbm, bb, slot):
        for p in range(n_pages):
            for half in range(2):
                _page_copy(cache_hbm, pt_ref, buf, sem, bb, p, half, slot).start()

    def wait(cache_hbm, bb, slot):
        for p in range(n_pages):
            for half in range(2):
                _page_copy(cache_hbm, pt_ref, buf, sem, bb, p, half, slot).wait()

    @pl.when(s == 0)
    def _():
        fetch(ccmp_hbm, 0, 0)

    @pl.when(s % 2 == 0)
    def _():
        fetch(cslc_hbm, b, 1)
        wait(ccmp_hbm, b, 0)
        tops = [[] for _ in range(4)]
        bots = [[] for _ in range(4)]
        for r in range(CMP_STRIDE):
            for kind in range(2):
                x_r = buf[pl.ds(0, 1), :, pl.ds(kind, 1), pl.ds(r, page // CMP_STRIDE, stride=CMP_STRIDE), :]
                xt = x_r.reshape(nch, LANES).T
                pe_rows = pl.ds(kind * LANES, LANES)
                top = (xt + pet_ref[pe_rows, r:r + 1]).astype(BF16)
                bot = (xt + pet_ref[pe_rows, CMP_STRIDE + r:CMP_STRIDE + r + 1]).astype(BF16)
                for g in range(NSA_KV):
                    tops[kind * NSA_KV + g].append(top[g * NSA_HEAD_DIM:(g + 1) * NSA_HEAD_DIM])
                    bots[kind * NSA_KV + g].append(bot[g * NSA_HEAD_DIM:(g + 1) * NSA_HEAD_DIM])
        half_w = CMP_STRIDE * NSA_HEAD_DIM
        ckv_t = []
        for qq in range(4):
            kind = qq // NSA_KV
            w1t = w1t_ref[kind]
            hid = (_dot(w1t[:, :half_w], jnp.concatenate(tops[qq], axis=0))
                   + pltpu.roll(_dot(w1t[:, half_w:], jnp.concatenate(bots[qq], axis=0)), nch - 1, 1)
                   + b1_ref[kind])
            hid = hid * jax.nn.sigmoid(hid)
            ckv_t.append((_dot(w2t_ref[kind], hid.astype(BF16)) + b2_ref[kind]).astype(BF16))
        q16 = q16_ref[0]
        cidx = lax.broadcasted_iota(jnp.int32, (1, nch), 1)
        m_c = (cidx * CMP_STRIDE + (CMP_LEN - 1) <= qpos) & (cidx < n_cmp)
        oc = []
        for g in range(NSA_KV):
            s_c = jnp.where(m_c, _dot(q16, ckv_t[g]), -jnp.inf)
            mx = jnp.max(s_c, axis=-1, keepdims=True)
            mx = jnp.where(mx == -jnp.inf, 0.0, mx)
            e = jnp.exp(s_c - mx)
            d = jnp.sum(e, axis=-1, keepdims=True)
            p_c = e / jnp.where(d > 0, d, 1.0)
            oc.append(_dot_nt(p_c.astype(BF16), ckv_t[NSA_KV + g]))
            in_group = row_g0 if g == 0 else jnp.logical_not(row_g0)
            p_sum = jnp.sum(jnp.where(in_group, p_c, 0.0), axis=0, keepdims=True)
            imp = _dot3(jnp.broadcast_to(p_sum, (8, nch)), ov_ref[...])[0:1]
            jidx = lax.broadcasted_iota(jnp.int32, (1, nsp), 1)
            qblk = qpos // SLC_LEN
            forced = (jidx == 0) | (jidx == qblk) | (jidx == qblk - 1)
            score = jnp.where(jidx * SLC_LEN <= qpos, imp + FORCE_SCORE * forced.astype(F32), -jnp.inf)
            s_j = jnp.broadcast_to(score, (nsp, nsp))
            s_k = s_j.T
            kk = lax.broadcasted_iota(jnp.int32, (nsp, nsp), 0)
            jj = lax.broadcasted_iota(jnp.int32, (nsp, nsp), 1)
            ahead = (s_k > s_j) | ((s_k == s_j) & (kk < jj))
            rank = jnp.sum(ahead.astype(F32), axis=0, keepdims=True)
            sel_scr[g:g + 1, :] = (rank < float(SLC_TOPN)).astype(F32)
        oc_scr[...] = jnp.where(row_g0, oc[0], oc[1])

    @pl.when(s % 2 == 1)
    def _():
        @pl.when(b + 1 < nb)
        def _():
            fetch(ccmp_hbm, b + 1, 0)

        wait(cslc_hbm, b, 1)
        qpad = qpad_ref[0]
        sel16 = jnp.where(row_g0, jnp.broadcast_to(sel_scr[0:1, :], (heads, nsp)),
                          jnp.broadcast_to(sel_scr[1:2, :], (heads, nsp)))
        sel16_bf = sel16.astype(BF16)
        ppc = 8
        ck = ppc * page
        n_ck = n_rows // ck
        vals, scores = [], []
        for c in range(n_ck):
            k_c = buf[1, c * ppc:(c + 1) * ppc, 0].reshape(ck, LANES).astype(BF16)
            vals.append(buf[1, c * ppc:(c + 1) * ppc, 1].reshape(ck, LANES).astype(BF16))
            blk = c * (ck // SLC_LEN) + lax.broadcasted_iota(jnp.int32, (nsp, ck), 1) // SLC_LEN
            expand = (lax.broadcasted_iota(jnp.int32, (nsp, ck), 0) == blk).astype(BF16)
            kpos = c * ck + lax.broadcasted_iota(jnp.int32, (1, ck), 1)
            mask = (_dot(sel16_bf, expand) > 0.5) & (kpos <= qpos)
            scores.append(jnp.where(mask, _dot_nt(qpad, k_c), -jnp.inf))
        new = new_ref[0]
        ks_new = new[:, 0:LANES].astype(BF16)
        vs_new = new[:, LANES:2 * LANES].astype(BF16)
        kw_new = new[:, 2 * LANES:3 * LANES].astype(BF16)
        vw_new = new[:, 3 * LANES:].astype(BF16)
        first = lax.broadcasted_iota(jnp.int32, (1, new.shape[0]), 1) == 0
        nblk = n_rows // SLC_LEN
        s_new = jnp.where(first & (sel16[:, nblk:nblk + 1] > 0.5), _dot_nt(qpad, ks_new), -jnp.inf)
        mx = jnp.max(s_new, axis=-1, keepdims=True)
        for sc in scores:
            mx = jnp.maximum(mx, jnp.max(sc, axis=-1, keepdims=True))
        e_new = jnp.exp(s_new - mx)
        den = jnp.sum(e_new, axis=-1, keepdims=True)
        acc = _dot(e_new.astype(BF16), vs_new)
        for sc, v_c in zip(scores, vals):
            e = jnp.exp(sc - mx)
            den = den + jnp.sum(e, axis=-1, keepdims=True)
            acc = acc + _dot(e.astype(BF16), v_c)
        o_s = acc / den
        k_w = win_ref[0, :, 0:LANES].astype(BF16)
        v_w = win_ref[0, :, LANES:2 * LANES].astype(BF16)
        dpos = w_rows - lax.broadcasted_iota(jnp.int32, (1, w_rows), 1)
        m_w = (dpos >= 0) & (dpos < WINDOW) & (qpos - dpos >= 0)
        s_w = jnp.where(m_w, _dot_nt(qpad, k_w), -jnp.inf)
        s_wn = jnp.where(first, _dot_nt(qpad, kw_new), -jnp.inf)
        mxw = jnp.maximum(jnp.max(s_w, axis=-1, keepdims=True), jnp.max(s_wn, axis=-1, keepdims=True))
        e_w = jnp.exp(s_w - mxw)
        e_wn = jnp.exp(s_wn - mxw)
        o_w = ((_dot(e_w.astype(BF16), v_w) + _dot(e_wn.astype(BF16), vw_new))
               / (jnp.sum(e_w, axis=-1, keepdims=True) + jnp.sum(e_wn, axis=-1, keepdims=True)))

        def own_group(o):
            return jnp.where(row_g0, o, pltpu.roll(o, NSA_HEAD_DIM, 1))[:, :NSA_HEAD_DIM]

        gt = jax.nn.sigmoid(gates_ref[0])
        y_ref[0] = gt[:, 0:1] * oc_scr[...] + gt[:, 1:2] * own_group(o_s) + gt[:, 2:3] * own_group(o_w)


def sample_nsa_call(page_table, cache_cmp, cache_slc, cache_win, q_rot, kvs_new, kvw_new, gates, pe, w1, b1, w2, b2):
    db, n_pages = page_table.shape
    n_phys, page = cache_cmp.shape[:2]
    past_len = n_pages * page
    nch = past_len // CMP_STRIDE
    n_cmp = nch - CMP_LEN // CMP_STRIDE + 1
    n_slc = -(-(past_len + 1) // SLC_LEN)
    nsp = -(-n_slc // LANES) * LANES
    w_rows = cache_win.shape[1]
    lanes = 4 * NSA_HEAD_DIM
    q16 = q_rot.reshape(db, NSA_HEADS, NSA_HEAD_DIM)
    q4 = q_rot.reshape(db, NSA_KV, NSA_HPG, NSA_HEAD_DIM)
    qpad = jnp.concatenate([jnp.pad(q4[:, g], ((0, 0), (0, 0), (g * NSA_HEAD_DIM, LANES - (g + 1) * NSA_HEAD_DIM)))
                            for g in range(NSA_KV)], axis=1)
    new = jnp.pad(jnp.concatenate([kvs_new, kvw_new], axis=1)[:, None, :], ((0, 0), (0, 15), (0, 0)))
    pet = jnp.repeat(pe, NSA_KV, axis=0).transpose(0, 2, 1).reshape(lanes, CMP_LEN)
    ov = jnp.asarray(np.pad(_overlap_np(n_cmp, n_slc), ((0, nch - n_cmp), (0, nsp - n_slc))), BF16)
    hidden = w1.shape[-1]
    full = lambda shape: pl.BlockSpec(shape, lambda s, pt: (0,) * len(shape))
    per_b = lambda shape: pl.BlockSpec((1,) + shape, lambda s, pt: (s // 2,) + (0,) * len(shape))
    grid_spec = pltpu.PrefetchScalarGridSpec(
        num_scalar_prefetch=1,
        grid=(2 * db,),
        in_specs=[pl.BlockSpec(memory_space=pl.ANY), pl.BlockSpec(memory_space=pl.ANY),
                  per_b((NSA_HEADS, NSA_HEAD_DIM)), per_b((NSA_HEADS, LANES)), per_b((16, 2 * lanes)),
                  per_b((w_rows, lanes)), per_b((NSA_HEADS, 3)),
                  full((lanes, CMP_LEN)), full((2, hidden, CMP_LEN * NSA_HEAD_DIM)), full((2, hidden, 1)),
                  full((2, NSA_HEAD_DIM, hidden)), full((2, NSA_HEAD_DIM, 1)), full((nch, nsp))],
        out_specs=per_b((NSA_HEADS, NSA_HEAD_DIM)),
        scratch_shapes=[pltpu.VMEM((2, n_pages, 2, page, LANES), F32),
                        pltpu.SemaphoreType.DMA((2,)),
                        pltpu.VMEM((8, nsp), F32),
                        pltpu.VMEM((NSA_HEADS, NSA_HEAD_DIM), F32)])
    y = pl.pallas_call(
        functools.partial(_sample_nsa_body, n_pages, n_cmp, past_len, w_rows),
        grid_spec=grid_spec,
        out_shape=jax.ShapeDtypeStruct((db, NSA_HEADS, NSA_HEAD_DIM), F32),
        compiler_params=_params("arbitrary"),
        name="sample_nsa",
    )(page_table, cache_cmp.reshape(n_phys, page, lanes), cache_slc.reshape(n_phys, page, lanes),
      q16, qpad, new, cache_win.reshape(db, w_rows, lanes), gates.reshape(db, NSA_HEADS, 3),
      pet, w1.transpose(0, 2, 1).astype(BF16), b1.reshape(2, hidden, 1),
      w2.transpose(0, 2, 1).astype(BF16), b2.reshape(2, NSA_HEAD_DIM, 1), ov)
    return y.reshape(db, NSA_HEADS * NSA_HEAD_DIM)


def _softplus(x):
    return jnp.maximum(x, 0.0) + jnp.log1p(jnp.exp(-jnp.abs(x)))


def _ssd_body(z_ref, xbc_ref, small_ref, cw_ref, cb_ref, dtb_ref, dtbt_ref, alog_ref, alogt_ref,
              dskip_ref, nrm_ref, rexp_ref, ltri_ref, utri_ref, y_ref, st_ref, xbuf, ht):
    c = pl.program_id(1)
    q = SSM_CHUNK
    d_inner = z_ref.shape[1]
    n_heads = d_inner // SSM_HEAD_DIM
    gw = d_inner // SSM_GROUPS
    n = SSM_STATE

    @pl.when(c == 0)
    def _():
        xbuf[0:8, :] = jnp.zeros((8, xbuf.shape[1]), F32)
        ht[...] = jnp.zeros(ht.shape, F32)

    x = xbc_ref[...]
    xbuf[8:8 + q, :] = x
    cw = cw_ref[...]
    conv = (x * cw[3:4] + xbuf[7:7 + q, :] * cw[2:3] + xbuf[6:6 + q, :] * cw[1:2]
            + xbuf[5:5 + q, :] * cw[0:1] + cb_ref[...])
    xbuf[0:8, :] = x[q - 8:q]
    xc = conv * jax.nn.sigmoid(conv)
    xs = xc[:, :d_inner]
    bm = xc[:, d_inner:d_inner + SSM_GROUPS * n]
    cm = xc[:, d_inner + SSM_GROUPS * n:]

    small = small_ref[...]
    dt = _softplus(small[:, 0:n_heads] + dtb_ref[...])
    dtt = _softplus(small.T[0:n_heads, :] + dtbt_ref[...])
    a = -jnp.exp(alog_ref[...])
    at = -jnp.exp(alogt_ref[...])
    cum = _dot3_left(ltri_ref[...], dt * a)
    cumt = _dot3(dtt * at, utri_ref[...])
    cum_last = cum[q - 1:q, :]
    rexp = rexp_ref[...]
    expcum_f = _dot3(jnp.exp(cum), rexp)
    toend_f = _dot3(jnp.exp(cum_last - cum) * dt, rexp)
    cdec_f = _dot3(jnp.broadcast_to(jnp.exp(cum_last), (8, n_heads)), rexp)[0:1]

    causal = (lax.broadcasted_iota(jnp.int32, (q, q), 0) >= lax.broadcasted_iota(jnp.int32, (q, q), 1))
    lane_lo = lax.broadcasted_iota(jnp.int32, (q, LANES), 1) < SSM_HEAD_DIM
    xs_bf = xs.astype(BF16)
    zero_bf = jnp.zeros((q, LANES), BF16)
    hpg = n_heads // SSM_GROUPS
    y_tiles, yoff_tiles = [], []
    for g in range(SSM_GROUPS):
        bg = bm[:, g * n:(g + 1) * n]
        cg_bf = cm[:, g * n:(g + 1) * n].astype(BF16)
        cb = _dot_nt(cg_bf, bg.astype(BF16))
        h_old = ht[g]
        yoff_tiles.append(_dot(cg_bf, h_old.astype(BF16)))
        for pr in range(hpg // 2):
            h0 = g * hpg + 2 * pr
            xp = xs_bf[:, h0 * SSM_HEAD_DIM:h0 * SSM_HEAD_DIM + LANES]
            acc = None
            for par, xm in ((0, jnp.where(lane_lo, xp, zero_bf)), (1, jnp.where(lane_lo, zero_bf, xp))):
                h = h0 + par
                seg = cum[:, h:h + 1] - cumt[h:h + 1, :]
                w = cb * jnp.exp(jnp.where(causal, seg, -jnp.inf)) * dtt[h:h + 1, :]
                part = _dot(w.astype(BF16), xm)
                acc = part if acc is None else acc + part
            y_tiles.append(acc)
        xw = (xs[:, g * gw:(g + 1) * gw] * toend_f[:, g * gw:(g + 1) * gw]).astype(BF16)
        ht[g] = h_old * cdec_f[:, g * gw:(g + 1) * gw] + _dot(bg.T.astype(BF16), xw)

    y = (jnp.concatenate(y_tiles, axis=1) + jnp.concatenate(yoff_tiles, axis=1) * expcum_f
         + dskip_ref[...] * xs)
    z = z_ref[...]
    y = y * (z * jax.nn.sigmoid(z))
    outs = []
    for g in range(SSM_GROUPS):
        yg = y[:, g * gw:(g + 1) * gw]
        ms = jnp.mean(yg * yg, axis=-1, keepdims=True)
        outs.append(yg * lax.rsqrt(ms + EPS) * nrm_ref[:, g * gw:(g + 1) * gw])
    y_ref[...] = jnp.concatenate(outs, axis=1).astype(BF16)
    st_ref[0] = ht[...]


def ssd_call(proj, conv_w, conv_b, dt_bias, a_log, d_skip, ssm_norm, bsz, seq):
    q = SSM_CHUNK
    nchunk = seq // q
    n_heads = dt_bias.shape[0]
    d_inner = n_heads * SSM_HEAD_DIM
    conv_dim = conv_w.shape[1]
    gw = d_inner // SSM_GROUPS
    small_blk = proj.shape[1] // LANES - 1
    rexp = jnp.asarray(np.repeat(np.eye(n_heads, dtype=np.float32), SSM_HEAD_DIM, axis=1), BF16)
    ltri = jnp.asarray(np.tril(np.ones((q, q), np.float32)), BF16)
    utri = jnp.asarray(np.triu(np.ones((q, q), np.float32)), BF16)
    const = lambda shape: pl.BlockSpec(shape, lambda b, c: (0,) * len(shape))
    return pl.pallas_call(
        _ssd_body,
        grid=(bsz, nchunk),
        in_specs=[pl.BlockSpec((q, d_inner), lambda b, c: (b * nchunk + c, 0)),
                  pl.BlockSpec((q, conv_dim), lambda b, c: (b * nchunk + c, 1)),
                  pl.BlockSpec((q, LANES), lambda b, c: (b * nchunk + c, small_blk)),
                  const((SSM_CONV, conv_dim)), const((1, conv_dim)),
                  const((1, n_heads)), const((n_heads, 1)), const((1, n_heads)), const((n_heads, 1)),
                  const((1, d_inner)), const((1, d_inner)),
                  const((n_heads, d_inner)), const((q, q)), const((q, q))],
        out_specs=[pl.BlockSpec((q, d_inner), lambda b, c: (b * nchunk + c, 0)),
                   pl.BlockSpec((1, SSM_GROUPS, SSM_STATE, gw), lambda b, c: (b, 0, 0, 0))],
        out_shape=[jax.ShapeDtypeStruct((bsz * seq, d_inner), BF16),
                   jax.ShapeDtypeStruct((bsz, SSM_GROUPS, SSM_STATE, gw), F32)],
        scratch_shapes=[pltpu.VMEM((8 + q, conv_dim), F32),
                        pltpu.VMEM((SSM_GROUPS, SSM_STATE, gw), F32)],
        compiler_params=_params("parallel", "arbitrary"),
        name="ssd",
    )(proj, proj, proj, conv_w, conv_b.reshape(1, -1),
      dt_bias.reshape(1, -1), dt_bias.reshape(-1, 1), a_log.reshape(1, -1), a_log.reshape(-1, 1),
      jnp.repeat(d_skip, SSM_HEAD_DIM).reshape(1, -1), ssm_norm.reshape(1, -1), rexp, ltri, utri)


CONF_HALO = 32


def _conf_body(u_ref, w_ref, b_ref, g_ref, beta_ref, o_ref, tail_ref, buf):
    t = pl.program_id(1)
    tm, d = o_ref.shape

    @pl.when(t == 0)
    def _():
        buf[0:CONF_HALO, :] = jnp.zeros((CONF_HALO, d), F32)

    u = u_ref[...]
    a = u[:, :d] * jax.nn.sigmoid(u[:, d:])
    buf[CONF_HALO:CONF_HALO + tm, :] = a
    w = w_ref[...]
    off = CONF_HALO - (CONF_CONV - 1)
    acc = b_ref[...] + buf[off:off + tm, :] * w[0:1]
    for k in range(1, CONF_CONV):
        acc = acc + buf[off + k:off + k + tm, :] * w[k:k + 1]
    buf[0:CONF_HALO, :] = a[tm - CONF_HALO:tm]
    tail_ref[0] = a[tm - CONF_HALO:tm]
    mu = jnp.mean(acc, axis=-1, keepdims=True)
    cen = acc - mu
    var = jnp.mean(cen * cen, axis=-1, keepdims=True)
    y = cen * lax.rsqrt(var + EPS) * g_ref[...] + beta_ref[...]
    o_ref[...] = (y * jax.nn.sigmoid(y)).astype(BF16)


def conformer_mid_call(u, dw_w, dw_b, ln_g, ln_b, bsz, seq, tm):
    d = dw_w.shape[1]
    nt = seq // tm
    const = lambda shape: pl.BlockSpec(shape, lambda b, t: (0,) * len(shape))
    return pl.pallas_call(
        _conf_body,
        grid=(bsz, nt),
        in_specs=[pl.BlockSpec((tm, 2 * d), lambda b, t: (b * nt + t, 0)),
                  const((CONF_CONV, d)), const((1, d)), const((1, d)), const((1, d))],
        out_specs=[pl.BlockSpec((tm, d), lambda b, t: (b * nt + t, 0)),
                   pl.BlockSpec((1, CONF_HALO, d), lambda b, t: (b, 0, 0))],
        out_shape=[jax.ShapeDtypeStruct((bsz * seq, d), BF16),
                   jax.ShapeDtypeStruct((bsz, CONF_HALO, d), F32)],
        scratch_shapes=[pltpu.VMEM((CONF_HALO + tm, d), F32)],
        compiler_params=_params("parallel", "arbitrary"),
        name="conformer_mid",
    )(u, dw_w, dw_b.reshape(1, d), ln_g.reshape(1, d), ln_b.reshape(1, d))


FFN_HALO = 8


def _ffn_up_body(x_ref, g_ref, wg_ref, wv_ref, cwg_ref, cwv_ref, cbg_ref, cbv_ref,
                 o_ref, tg_ref, tv_ref, xn_ref, ubuf, carry):
    t = pl.program_id(1)
    j = pl.program_id(2)
    tm = x_ref.shape[0]

    @pl.when(j == 0)
    def _():
        x = x_ref[...]
        ms = jnp.mean(x * x, axis=-1, keepdims=True)
        xn_ref[...] = (x * lax.rsqrt(ms + EPS) * g_ref[...]).astype(BF16)

    @pl.when(t == 0)
    def _():
        carry[j] = jnp.zeros(carry.shape[1:], F32)

    xn = xn_ref[...]
    halves = []
    for part, (w_ref, cw_ref, cb_ref, tail_ref) in enumerate(((wg_ref, cwg_ref, cbg_ref, tg_ref),
                                                              (wv_ref, cwv_ref, cbv_ref, tv_ref))):
        u = _dot(xn, w_ref[...])
        ubuf[part, 0:FFN_HALO, :] = carry[j, part]
        ubuf[part, FFN_HALO:FFN_HALO + tm, :] = u
        cw = cw_ref[...]
        halves.append(u * cw[2:3] + ubuf[part, FFN_HALO - 1:FFN_HALO - 1 + tm, :] * cw[1:2]
                      + ubuf[part, FFN_HALO - 2:FFN_HALO - 2 + tm, :] * cw[0:1] + cb_ref[...])
        carry[j, part] = u[tm - FFN_HALO:tm]
        tail_ref[0, j] = u[tm - FFN_HALO:tm]
    o_ref[...] = (jax.nn.gelu(halves[0], approximate=True) * halves[1]).astype(BF16)


def ffn_up_call(x, g, w_up, conv_w, conv_b, bsz, seq, tm, tn):
    m, k = x.shape
    d2 = w_up.shape[1]
    dff = d2 // 2
    nt = seq // tm
    nj = dff // tn
    cb = conv_b.reshape(1, d2)
    row = lambda b, t, j: b * nt + t
    act, tail_g, tail_v = pl.pallas_call(
        _ffn_up_body,
        grid=(bsz, nt, nj),
        in_specs=[pl.BlockSpec((tm, k), lambda b, t, j: (row(b, t, j), 0)),
                  pl.BlockSpec((1, k), lambda b, t, j: (0, 0)),
                  pl.BlockSpec((k, tn), lambda b, t, j: (0, j)),
                  pl.BlockSpec((k, tn), lambda b, t, j: (0, j + nj)),
                  pl.BlockSpec((FFN_CONV, tn), lambda b, t, j: (0, j)),
                  pl.BlockSpec((FFN_CONV, tn), lambda b, t, j: (0, j + nj)),
                  pl.BlockSpec((1, tn), lambda b, t, j: (0, j)),
                  pl.BlockSpec((1, tn), lambda b, t, j: (0, j + nj))],
        out_specs=[pl.BlockSpec((tm, tn), lambda b, t, j: (row(b, t, j), j)),
                   pl.BlockSpec((1, nj, FFN_HALO, tn), lambda b, t, j: (b, 0, 0, 0)),
                   pl.BlockSpec((1, nj, FFN_HALO, tn), lambda b, t, j: (b, 0, 0, 0))],
        out_shape=[jax.ShapeDtypeStruct((m, dff), BF16),
                   jax.ShapeDtypeStruct((bsz, nj, FFN_HALO, tn), F32),
                   jax.ShapeDtypeStruct((bsz, nj, FFN_HALO, tn), F32)],
        scratch_shapes=[pltpu.VMEM((tm, k), BF16),
                        pltpu.VMEM((2, FFN_HALO + tm, tn), F32),
                        pltpu.VMEM((nj, 2, FFN_HALO, tn), F32)],
        compiler_params=_params("parallel", "arbitrary", "arbitrary"),
        name="ffn_up",
    )(x, g.reshape(1, k), w_up, w_up, conv_w, conv_w, cb, cb)
    flat = lambda tail: tail.transpose(0, 2, 1, 3).reshape(bsz, FFN_HALO, dff)
    return act, jnp.concatenate([flat(tail_g), flat(tail_v)], axis=-1)


def _rms(x, g):
    return x * lax.rsqrt(jnp.mean(x * x, axis=-1, keepdims=True) + EPS) * g


def _masked_softmax(s, mask):
    s = jnp.where(mask, s, -jnp.inf)
    m = jnp.max(s, axis=-1, keepdims=True)
    m = jnp.where(jnp.isfinite(m), m, 0.0)
    e = jnp.exp(s - m)
    d = jnp.sum(e, axis=-1, keepdims=True)
    return e / jnp.where(d > 0, d, 1.0)


def _sample_ssd_step(z, xbc, dt_raw, conv_hist, h0, conv_w, conv_b, dt_bias, a_log, d_skip, ssm_norm):
    bsz = z.shape[0]
    n_heads = dt_bias.shape[0]
    d_inner = n_heads * SSM_HEAD_DIM
    hpg = n_heads // SSM_GROUPS
    xh = jnp.concatenate([conv_hist, xbc[:, None, :]], axis=1)
    new_hist = xh[:, 1:]
    conv = jnp.sum(xh * conv_w[None], axis=1) + conv_b
    xc = conv * jax.nn.sigmoid(conv)
    xs = xc[:, :d_inner].reshape(bsz, SSM_GROUPS, hpg, SSM_HEAD_DIM)
    bm = xc[:, d_inner:d_inner + SSM_GROUPS * SSM_STATE].reshape(bsz, SSM_GROUPS, SSM_STATE)
    cm = xc[:, d_inner + SSM_GROUPS * SSM_STATE:].reshape(bsz, SSM_GROUPS, SSM_STATE)
    dt = jax.nn.softplus(dt_raw + dt_bias).reshape(bsz, SSM_GROUPS, hpg)
    a = (-jnp.exp(a_log)).reshape(SSM_GROUPS, hpg)
    dec = jnp.exp(dt * a)
    h0g = h0.reshape(bsz, SSM_GROUPS, hpg, SSM_HEAD_DIM, SSM_STATE)
    h_new = dec[..., None, None] * h0g + (dt[..., None] * xs)[..., None] * bm[:, :, None, None, :]
    y = jnp.sum(h_new * cm[:, :, None, None, :], axis=-1)
    y = y + d_skip.reshape(SSM_GROUPS, hpg)[None, :, :, None] * xs
    y = y.reshape(bsz, d_inner) * (z * jax.nn.sigmoid(z))
    y = _rms(y.reshape(bsz, SSM_GROUPS, d_inner // SSM_GROUPS),
             ssm_norm.reshape(SSM_GROUPS, d_inner // SSM_GROUPS)).reshape(bsz, d_inner)
    return y, new_hist, h_new.reshape(h0.shape)


def _compress_rows_jnp(rows, pe, w1, b1, w2, b2):
    bsz, t = rows.shape[:2]
    r = CMP_LEN // CMP_STRIDE
    nch = t // CMP_STRIDE
    n_cmp = nch - r + 1
    ch = rows[:, :nch * CMP_STRIDE].reshape(bsz, nch, CMP_STRIDE, NSA_KV, NSA_HEAD_DIM)
    blk = jnp.concatenate([ch[:, k:k + n_cmp] for k in range(r)], axis=2)
    blk = blk + pe[None, None, :, None, :]
    flat = blk.transpose(0, 1, 3, 2, 4).reshape(bsz, n_cmp, NSA_KV, CMP_LEN * NSA_HEAD_DIM)
    hid = jax.nn.silu(flat @ w1 + b1)
    return hid @ w2 + b2


def _sample_nsa(q, gates, kvc, kvs, kvw, cache_cmp, cache_slc, cache_win, page_table, cmp_w):
    db = q.shape[0]
    past_len = page_table.shape[1] * cache_cmp.shape[1]
    pe, w1, b1, w2, b2 = cmp_w
    gather = lambda cache: cache[page_table].reshape((db, past_len) + cache.shape[2:])
    rows_cmp = jnp.concatenate([gather(cache_cmp), kvc], axis=1)
    rows_slc = jnp.concatenate([gather(cache_slc), kvs], axis=1)
    kc = _compress_rows_jnp(rows_cmp[:, :, 0], pe[0], w1[0], b1[0], w2[0], b2[0])
    vc = _compress_rows_jnp(rows_cmp[:, :, 1], pe[1], w1[1], b1[1], w2[1], b2[1])
    n_cmp = kc.shape[1]
    cmp_end = jnp.arange(n_cmp) * CMP_STRIDE + CMP_LEN - 1
    t_all = rows_slc.shape[1]
    n_slc = -(-t_all // SLC_LEN)
    slc = jnp.pad(rows_slc, [(0, 0), (0, n_slc * SLC_LEN - t_all), (0, 0), (0, 0), (0, 0)])
    slc = slc.reshape(db, n_slc, SLC_LEN, 2, NSA_KV, NSA_HEAD_DIM).transpose(0, 4, 1, 2, 3, 5)
    wkv = jnp.concatenate([cache_win, kvw], axis=1)
    w_rows = cache_win.shape[1]
    wpos = past_len - w_rows + jnp.arange(w_rows + 1)
    qpos = past_len
    qg = (q[:, 0] * NSA_HEAD_DIM ** -0.5).reshape(db, NSA_KV, NSA_HPG, NSA_HEAD_DIM)
    s_c = jnp.einsum('bghd,bngd->bghn', qg, kc)
    p_c = _masked_softmax(s_c, (cmp_end <= qpos)[None, None, None, :])
    o_c = jnp.einsum('bghn,bngd->bghd', p_c, vc)
    imp = jnp.einsum('bghn,nj->bgj', p_c, jnp.asarray(_overlap_np(n_cmp, n_slc)),
                     precision=lax.Precision.HIGHEST)
    j = jnp.arange(n_slc)
    qblk = qpos // SLC_LEN
    forced = (j == 0) | (j == qblk) | (j == qblk - 1)
    score = jnp.where((j * SLC_LEN <= qpos)[None, None, :], imp + FORCE_SCORE * forced[None, None, :], -jnp.inf)
    _, idx = lax.top_k(score, min(SLC_TOPN, n_slc))
    bi = jnp.arange(db)[:, None, None]
    gi = jnp.arange(NSA_KV)[None, :, None]
    sel = slc[bi, gi, idx]
    s_s = jnp.einsum('bghd,bgnld->bghnl', qg, sel[..., 0, :])
    kpos = idx[..., None] * SLC_LEN + jnp.arange(SLC_LEN)
    m_s = (kpos <= qpos)[:, :, None]
    p_s = _masked_softmax(s_s.reshape(db, NSA_KV, NSA_HPG, -1), m_s.reshape(db, NSA_KV, 1, -1)).reshape(s_s.shape)
    o_s = jnp.einsum('bghnl,bgnld->bghd', p_s, sel[..., 1, :])
    s_w = jnp.einsum('bghd,bsgd->bghs', qg, wkv[:, :, 0])
    dpos = qpos - wpos
    p_w = _masked_softmax(s_w, ((dpos >= 0) & (dpos < WINDOW) & (wpos >= 0))[None, None, None, :])
    o_w = jnp.einsum('bghs,bsgd->bghd', p_w, wkv[:, :, 1])
    gt = jax.nn.sigmoid(gates).reshape(db, NSA_KV, NSA_HPG, 3)
    o = gt[..., 0:1] * o_c + gt[..., 1:2] * o_s + gt[..., 2:3] * o_w
    return o.reshape(db, NSA_HEADS * NSA_HEAD_DIM)


def _prep_w_in(w_in, d_inner, conv_dim, n_heads):
    nq = NSA_HEADS * NSA_HEAD_DIM
    nkv = 2 * NSA_KV * NSA_HEAD_DIM
    o = np.cumsum([0, d_inner, conv_dim, n_heads, nq, nkv, nkv, nkv, 3 * NSA_HEADS])
    z, xbc, dt, q, kvc, kvs, kvw, gates = (w_in[:, o[k]:o[k + 1]] for k in range(8))
    small = jnp.concatenate([dt, gates], axis=1)
    small = jnp.pad(small, ((0, 0), (0, LANES - small.shape[1])))
    return jnp.concatenate([z, q, xbc, kvc, kvs, kvw, small], axis=1).astype(BF16)


def _pick_tile(m, pref):
    return pref if m % pref == 0 else m


def _col_tile(n, target):
    best = None
    for t in range(LANES, min(n, target) + 1, LANES):
        if n % t == 0:
            best = t
    return best or n


def kernel(x_prompt, x_sample, state_ssm, state_ssm_conv, cache_cmp, cache_slc, cache_win, state_conf_conv, state_ffn_conv, page_table, norm_g, ab_w_in, ab_conv_w, ab_conv_b, ab_dt_bias, ab_a_log, ab_d_skip, ab_ssm_norm, ab_cmp_pe, ab_cmp_w1, ab_cmp_b1, ab_cmp_w2, ab_cmp_b2, ab_w_out, c_w_pw1, c_b_pw1, c_dw_w, c_dw_b, c_ln_g, c_ln_b, c_w_pw2, c_b_pw2, ffn_w_up, ffn_conv_w, ffn_conv_b, ffn_w_down):
    bsz, seq, d_model = x_prompt.shape
    db, dseq, _ = x_sample.shape
    assert dseq == 1
    depth = norm_g.shape[0]
    n_heads = ab_dt_bias.shape[1]
    d_inner = n_heads * SSM_HEAD_DIM
    conv_dim = ab_conv_w.shape[2]
    d_ff2 = ffn_w_up.shape[2]
    d_conf = c_dw_w.shape[2] if c_dw_w.shape[0] else d_model
    past_len = page_table.shape[1] * cache_cmp.shape[2]
    mp, ms = bsz * seq, db
    hp = x_prompt.reshape(mp, d_model)
    hs = x_sample.reshape(ms, d_model)
    tm_p = _pick_tile(mp, 1024)
    tm_post = _pick_tile(mp, 512)
    tm_seq = _pick_tile(seq, 256)
    tn_ff = _col_tile(d_ff2 // 2, 1536)

    cos_p, sin_p = _rope_tables(jnp.arange(seq, dtype=jnp.int32))
    cos_s, sin_s = _rope_tables(jnp.full((ms,), past_len, jnp.int32))
    nchk = seq // CMP_STRIDE
    n_cmp = nchk - CMP_LEN // CMP_STRIDE + 1

    outs = {k: [] for k in ("ssm_p", "ssm_s", "sconv_p", "sconv_s", "cmp_p", "cmp_s", "slc_p", "slc_s",
                            "win_p", "win_s", "conf_p", "conf_s", "ffn_p", "ffn_s")}
    kvshape = (2, NSA_KV, NSA_HEAD_DIM)
    for layer in range(depth):
        g = norm_g[layer]
        if layer % 2 == 0:
            a = layer // 2
            w_in = _prep_w_in(ab_w_in[a], d_inner, conv_dim, n_heads)
            n_proj = w_in.shape[1]
            zero_b = jnp.zeros((n_proj,), F32)
            w_out = ab_w_out[a].astype(BF16)
            zero_d = jnp.zeros((d_model,), F32)
            tn_in = _col_tile(n_proj, 1536)
            proj = norm_matmul(hp, g[0], w_in, zero_b, tm_p, tn_in)
            q_rot, q_swp, kv_pad, kvc, kvs, kvw = rope_call(proj, cos_p, sin_p, _pick_tile(seq, 512))
            ch = kvc.reshape(bsz, nchk, CMP_STRIDE, 4, NSA_HEAD_DIM).transpose(0, 3, 1, 2, 4)
            ch = ch.reshape(bsz, 4, nchk, CMP_STRIDE * NSA_HEAD_DIM)
            pe_flat = ab_cmp_pe[a].reshape(2, 1, CMP_LEN * NSA_HEAD_DIM)
            w2p = jnp.pad(ab_cmp_w2[a], ((0, 0), (0, 0), (0, LANES - NSA_HEAD_DIM))).astype(BF16)
            b2p = jnp.pad(ab_cmp_b2[a], ((0, 0), (0, LANES - NSA_HEAD_DIM))).reshape(2, 1, LANES)
            ckv = compress_call(ch, pe_flat, ab_cmp_w1[a].astype(BF16), ab_cmp_b1[a].reshape(2, 1, -1), w2p, b2p)
            y_nsa = nsa_prompt_call(q_rot, q_swp, kv_pad, ckv, proj, bsz, seq, n_cmp, n_heads)
            y_ssm, st = ssd_call(proj, ab_conv_w[a], ab_conv_b[a], ab_dt_bias[a], ab_a_log[a],
                                 ab_d_skip[a], ab_ssm_norm[a], bsz, seq)
            hp = matmul_post([y_ssm, y_nsa], [w_out[:d_inner], w_out[d_inner:]], zero_d, hp, g[1], tm_post)
            hpg = n_heads // SSM_GROUPS
            outs["ssm_p"].append(st.reshape(bsz, SSM_GROUPS, SSM_STATE, hpg, SSM_HEAD_DIM)
                                 .transpose(0, 1, 3, 4, 2).reshape(bsz, n_heads, SSM_HEAD_DIM, SSM_STATE))
            tail_p = proj.reshape(bsz, seq, n_proj)[:, seq - (SSM_CONV - 1):]
            outs["sconv_p"].append(tail_p[:, :, d_inner + 1024:d_inner + 1024 + conv_dim])
            outs["cmp_p"].append(kvc.reshape((bsz, seq) + kvshape))
            outs["slc_p"].append(kvs.reshape((bsz, seq) + kvshape))
            outs["win_p"].append(kvw.reshape((bsz, seq) + kvshape)[:, seq - min(WINDOW, seq):])
            proj_s = norm_matmul(hs, g[0], w_in, zero_b, ms, tn_in)
            q_s, _, _, kvc_s, kvs_s, kvw_s = rope_call(proj_s, cos_s, sin_s, ms)
            z_s = proj_s[:, :d_inner]
            xbc_s = proj_s[:, d_inner + 1024:d_inner + 1024 + conv_dim]
            dt_s = proj_s[:, n_proj - LANES:n_proj - LANES + n_heads]
            gates_s = proj_s[:, n_proj - LANES + n_heads:n_proj - LANES + n_heads + 3 * NSA_HEADS]
            ys_ssm, hist_s, h_s = _sample_ssd_step(z_s, xbc_s, dt_s, state_ssm_conv[a], state_ssm[a],
                                                   ab_conv_w[a], ab_conv_b[a], ab_dt_bias[a], ab_a_log[a],
                                                   ab_d_skip[a], ab_ssm_norm[a])
            ys_nsa = sample_nsa_call(page_table, cache_cmp[a], cache_slc[a], cache_win[a], q_s, kvs_s, kvw_s,
                                     gates_s, ab_cmp_pe[a], ab_cmp_w1[a], ab_cmp_b1[a], ab_cmp_w2[a], ab_cmp_b2[a])
            kvc_s = kvc_s.reshape((ms, 1) + kvshape)
            kvs_s = kvs_s.reshape((ms, 1) + kvshape)
            kvw_s = kvw_s.reshape((ms, 1) + kvshape)
            hs = matmul_post([ys_ssm.astype(BF16), ys_nsa.astype(BF16)], [w_out[:d_inner], w_out[d_inner:]],
                             zero_d, hs, g[1], ms)
            outs["ssm_s"].append(h_s)
            outs["sconv_s"].append(hist_s)
            outs["cmp_s"].append(kvc_s)
            outs["slc_s"].append(kvs_s)
            outs["win_s"].append(kvw_s)
        else:
            c = layer // 2
            w1 = c_w_pw1[c].astype(BF16)
            w2 = c_w_pw2[c].astype(BF16)
            u = norm_matmul(hp, g[0], w1, c_b_pw1[c], tm_p, 1024)
            act, tail = conformer_mid_call(u, c_dw_w[c], c_dw_b[c], c_ln_g[c], c_ln_b[c], bsz, seq, tm_seq)
            hp = matmul_post([act], [w2], c_b_pw2[c], hp, g[1], tm_post)
            outs["conf_p"].append(tail[:, CONF_HALO - (CONF_CONV - 1):])
            u_s = norm_matmul(hs, g[0], w1, c_b_pw1[c], ms, 512)
            a_s = u_s[:, :d_conf] * jax.nn.sigmoid(u_s[:, d_conf:])
            ah = jnp.concatenate([state_conf_conv[c], a_s[:, None, :]], axis=1)
            conv = jnp.sum(ah * c_dw_w[c][None], axis=1) + c_dw_b[c]
            mu = jnp.mean(conv, axis=-1, keepdims=True)
            var = jnp.mean(jnp.square(conv - mu), axis=-1, keepdims=True)
            yln = (conv - mu) * lax.rsqrt(var + EPS) * c_ln_g[c] + c_ln_b[c]
            hs = matmul_post([(yln * jax.nn.sigmoid(yln)).astype(BF16)], [w2], c_b_pw2[c], hs, g[1], ms)
            outs["conf_s"].append(ah[:, 1:])
        w_up = ffn_w_up[layer].astype(BF16)
        w_dn = ffn_w_down[layer].astype(BF16)
        zero_up = jnp.zeros((d_ff2,), F32)
        zero_d = jnp.zeros((d_model,), F32)
        act, tail_u = ffn_up_call(hp, g[2], w_up, ffn_conv_w[layer], ffn_conv_b[layer], bsz, seq,
                                  _pick_tile(seq, 512), tn_ff)
        hp = matmul_post([act], [w_dn], zero_d, hp, g[3], tm_post)
        outs["ffn_p"].append(tail_u[:, FFN_HALO - (FFN_CONV - 1):])
        u_s = norm_matmul(hs, g[2], w_up, zero_up, ms, tn_ff)
        uh = jnp.concatenate([state_ffn_conv[layer], u_s[:, None, :]], axis=1)
        y_s = jnp.sum(uh * ffn_conv_w[layer][None], axis=1) + ffn_conv_b[layer]
        act_s = jax.nn.gelu(y_s[:, :d_ff2 // 2], approximate=True) * y_s[:, d_ff2 // 2:]
        hs = matmul_post([act_s.astype(BF16)], [w_dn], zero_d, hs, g[3], ms)
        outs["ffn_s"].append(uh[:, 1:])

    st = lambda k: jnp.stack(outs[k])
    return (hp.reshape(bsz, seq, d_model), hs.reshape(db, dseq, d_model),
            st("ssm_p"), st("ssm_s"), st("sconv_p"), st("sconv_s"), st("cmp_p"), st("cmp_s"),
            st("slc_p"), st("slc_s"), st("win_p"), st("win_s"), st("conf_p"), st("conf_s"),
            st("ffn_p"), st("ffn_s"))
```

```python
import functools
import math

import jax
import jax.numpy as jnp
import numpy as np
from jax import lax
from jax.experimental import pallas as pl
from jax.experimental.pallas import tpu as pltpu

F32 = jnp.float32
BF16 = jnp.bfloat16

EPS = 1e-6
LANES = 128
VMEM_LIMIT = 48 * 1024 * 1024

SSM_HEAD_DIM = 64
SSM_GROUPS = 4
SSM_STATE = 128
SSM_CONV = 4
SSM_CHUNK = 128
NSA_HEADS = 16
NSA_KV = 2
NSA_HPG = NSA_HEADS // NSA_KV
NSA_HEAD_DIM = 64
CMP_LEN = 32
CMP_STRIDE = 16
SLC_LEN = 64
SLC_TOPN = 16
WINDOW = 512
Q_BLOCK = 128
ROPE_THETA = 10000.0
FORCE_SCORE = 1e4
CONF_CONV = 31
FFN_CONV = 3


def _params(*sem):
    return pltpu.CompilerParams(dimension_semantics=sem, vmem_limit_bytes=VMEM_LIMIT)


def _split3(v):
    hi = v.astype(BF16)
    r1 = v - hi.astype(F32)
    mid = r1.astype(BF16)
    lo = (r1 - mid.astype(F32)).astype(BF16)
    return hi, mid, lo


def _dot(a, b):
    return jnp.dot(a, b, preferred_element_type=F32)


def _dot_nt(a, b):
    return lax.dot_general(a, b, (((1,), (1,)), ((), ())), preferred_element_type=F32)


def _dot3(v, m_bf16):
    hi, mid, lo = _split3(v)
    return _dot(hi, m_bf16) + _dot(mid, m_bf16) + _dot(lo, m_bf16)


def _dot3_left(m_bf16, v):
    hi, mid, lo = _split3(v)
    return _dot(m_bf16, hi) + _dot(m_bf16, mid) + _dot(m_bf16, lo)


def _norm_matmul_body(x_ref, g_ref, w_ref, b_ref, o_ref, xn_ref):
    @pl.when(pl.program_id(1) == 0)
    def _():
        x = x_ref[...]
        ms = jnp.mean(x * x, axis=-1, keepdims=True)
        xn_ref[...] = (x * lax.rsqrt(ms + EPS) * g_ref[...]).astype(BF16)

    o_ref[...] = _dot(xn_ref[...], w_ref[...]) + b_ref[...]


def norm_matmul(x, g, w, b, tm, tn):
    m, k = x.shape
    n = w.shape[1]
    return pl.pallas_call(
        _norm_matmul_body,
        grid=(m // tm, n // tn),
        in_specs=[pl.BlockSpec((tm, k), lambda i, j: (i, 0)),
                  pl.BlockSpec((1, k), lambda i, j: (0, 0)),
                  pl.BlockSpec((k, tn), lambda i, j: (0, j)),
                  pl.BlockSpec((1, tn), lambda i, j: (0, j))],
        out_specs=pl.BlockSpec((tm, tn), lambda i, j: (i, j)),
        out_shape=jax.ShapeDtypeStruct((m, n), F32),
        scratch_shapes=[pltpu.VMEM((tm, k), BF16)],
        compiler_params=_params("parallel", "arbitrary"),
        name="norm_matmul",
    )(x, g.reshape(1, k), w, b.reshape(1, n))


def _matmul_post_body(n_pairs, *refs):
    a_refs = refs[:n_pairs]
    w_refs = refs[n_pairs:2 * n_pairs]
    b_ref, r_ref, g_ref, o_ref = refs[2 * n_pairs:]
    acc = _dot(a_refs[0][...], w_refs[0][...])
    for a_ref, w_ref in zip(a_refs[1:], w_refs[1:]):
        acc = acc + _dot(a_ref[...], w_ref[...])
    acc = acc + b_ref[...]
    ms = jnp.mean(acc * acc, axis=-1, keepdims=True)
    o_ref[...] = r_ref[...] + acc * lax.rsqrt(ms + EPS) * g_ref[...]


def matmul_post(a_list, w_list, b, resid, g, tm):
    m, n = resid.shape
    n_pairs = len(a_list)
    in_specs = [pl.BlockSpec((tm, a.shape[1]), lambda i: (i, 0)) for a in a_list]
    in_specs += [pl.BlockSpec(w.shape, lambda i: (0, 0)) for w in w_list]
    in_specs += [pl.BlockSpec((1, n), lambda i: (0, 0)),
                 pl.BlockSpec((tm, n), lambda i: (i, 0)),
                 pl.BlockSpec((1, n), lambda i: (0, 0))]
    return pl.pallas_call(
        functools.partial(_matmul_post_body, n_pairs),
        grid=(m // tm,),
        in_specs=in_specs,
        out_specs=pl.BlockSpec((tm, n), lambda i: (i, 0)),
        out_shape=jax.ShapeDtypeStruct((m, n), F32),
        compiler_params=_params("parallel"),
        name="matmul_post",
    )(*a_list, *w_list, b.reshape(1, n), resid, g.reshape(1, n))


def _rope_tables(pos):
    half = NSA_HEAD_DIM // 2
    inv = ROPE_THETA ** (-jnp.arange(half, dtype=F32) / half)
    ang = pos.astype(F32)[:, None] * inv[None, :]
    cos, sin = jnp.cos(ang), jnp.sin(ang)
    reps = LANES // NSA_HEAD_DIM
    cosf = jnp.tile(jnp.concatenate([cos, cos], axis=-1), (1, reps))
    sinf = jnp.tile(jnp.concatenate([-sin, sin], axis=-1), (1, reps))
    return cosf, sinf


def _rotate_half_partner(x):
    w = x.shape[-1]
    lane = lax.broadcasted_iota(jnp.int32, x.shape, x.ndim - 1)
    first = (lane % NSA_HEAD_DIM) < (NSA_HEAD_DIM // 2)
    return jnp.where(first, pltpu.roll(x, w - NSA_HEAD_DIM // 2, x.ndim - 1),
                     pltpu.roll(x, NSA_HEAD_DIM // 2, x.ndim - 1))


def _swap_head_pair(x):
    w = x.shape[-1]
    lane = lax.broadcasted_iota(jnp.int32, x.shape, x.ndim - 1)
    first = (lane % LANES) < NSA_HEAD_DIM
    return jnp.where(first, pltpu.roll(x, w - NSA_HEAD_DIM, x.ndim - 1),
                     pltpu.roll(x, NSA_HEAD_DIM, x.ndim - 1))


def _rope_body(q_ref, kv_ref, cos_ref, sin_ref, q_o, qs_o, kv_o, kvc_o, kvs_o, kvw_o):
    cos = cos_ref[...]
    sin = sin_ref[...]
    q = q_ref[...]
    nq = q.shape[1] // LANES
    cq = jnp.concatenate([cos] * nq, axis=1)
    sq = jnp.concatenate([sin] * nq, axis=1)
    qr = (q * cq + _rotate_half_partner(q) * sq) * (NSA_HEAD_DIM ** -0.5)
    q_o[...] = qr.astype(BF16)
    qs_o[...] = _swap_head_pair(qr).astype(BF16)

    kv = kv_ref[...]
    lane = lax.broadcasted_iota(jnp.int32, (kv.shape[0], LANES), 1)
    lo = lane < NSA_HEAD_DIM
    zero = jnp.zeros((kv.shape[0], LANES), F32)
    padded = []
    for c, full_o in enumerate((kvc_o, kvs_o, kvw_o)):
        k = kv[:, c * 256:c * 256 + LANES]
        v = kv[:, c * 256 + LANES:(c + 1) * 256]
        kr = k * cos + _rotate_half_partner(k) * sin
        full_o[...] = jnp.concatenate([kr, v], axis=1)
        if c > 0:
            kr_sw = pltpu.roll(kr, NSA_HEAD_DIM, 1)
            v_sw = pltpu.roll(v, NSA_HEAD_DIM, 1)
            padded += [jnp.where(lo, kr, zero), jnp.where(lo, kr_sw, zero),
                       jnp.where(lo, v, zero), jnp.where(lo, v_sw, zero)]
    kv_o[...] = jnp.concatenate(padded, axis=1).astype(BF16)


def rope_call(proj, cosf, sinf, tm):
    m = proj.shape[0]
    nblk = cosf.shape[0] // tm
    f32s = jax.ShapeDtypeStruct((m, 256), F32)
    return pl.pallas_call(
        _rope_body,
        grid=(m // tm,),
        in_specs=[pl.BlockSpec((tm, 1024), lambda i: (i, 2)),
                  pl.BlockSpec((tm, 768), lambda i: (i, 8)),
                  pl.BlockSpec((tm, LANES), lambda i: (i % nblk, 0)),
                  pl.BlockSpec((tm, LANES), lambda i: (i % nblk, 0))],
        out_specs=[pl.BlockSpec((tm, 1024), lambda i: (i, 0)),
                   pl.BlockSpec((tm, 1024), lambda i: (i, 0)),
                   pl.BlockSpec((tm, 1024), lambda i: (i, 0)),
                   pl.BlockSpec((tm, 256), lambda i: (i, 0)),
                   pl.BlockSpec((tm, 256), lambda i: (i, 0)),
                   pl.BlockSpec((tm, 256), lambda i: (i, 0))],
        out_shape=[jax.ShapeDtypeStruct((m, 1024), BF16), jax.ShapeDtypeStruct((m, 1024), BF16),
                   jax.ShapeDtypeStruct((m, 1024), BF16), f32s, f32s, f32s],
        compiler_params=_params("parallel"),
        name="rope",
    )(proj, proj, cosf, sinf)


def _compress_body(ch_ref, pe_ref, w1_ref, b1_ref, w2_ref, b2_ref, o_ref):
    ch = ch_ref[0, 0]
    nc, half = ch.shape
    pe = pe_ref[0]
    top = _dot((ch + pe[:, :half]).astype(BF16), w1_ref[0, :half, :])
    bot = _dot((ch + pe[:, half:]).astype(BF16), w1_ref[0, half:, :])
    hid = top + pltpu.roll(bot, nc - 1, 0) + b1_ref[0]
    hid = hid * jax.nn.sigmoid(hid)
    o_ref[0, 0] = _dot(hid.astype(BF16), w2_ref[0]) + b2_ref[0]


def compress_call(ch, pe_flat, w1, b1, w2p, b2p):
    bsz, _, nc, half = ch.shape
    hidden = w1.shape[-1]
    return pl.pallas_call(
        _compress_body,
        grid=(bsz, 4),
        in_specs=[pl.BlockSpec((1, 1, nc, half), lambda b, q: (b, q, 0, 0)),
                  pl.BlockSpec((1, 1, 2 * half), lambda b, q: (q // 2, 0, 0)),
                  pl.BlockSpec((1, 2 * half, hidden), lambda b, q: (q // 2, 0, 0)),
                  pl.BlockSpec((1, 1, hidden), lambda b, q: (q // 2, 0, 0)),
                  pl.BlockSpec((1, hidden, LANES), lambda b, q: (q // 2, 0, 0)),
                  pl.BlockSpec((1, 1, LANES), lambda b, q: (q // 2, 0, 0))],
        out_specs=pl.BlockSpec((1, 1, nc, LANES), lambda b, q: (b, q, 0, 0)),
        out_shape=jax.ShapeDtypeStruct((bsz, 4, nc, LANES), F32),
        compiler_params=_params("parallel", "parallel"),
        name="compress",
    )(ch, pe_flat, w1, b1, w2p, b2p)


def _overlap_np(nc, ns):
    i = np.arange(nc)[:, None]
    j = np.arange(ns)[None, :]
    return ((i * CMP_STRIDE < (j + 1) * SLC_LEN) & (i * CMP_STRIDE + CMP_LEN > j * SLC_LEN)).astype(np.float32)


def _nsa_t_body(seq, n_cmp, ck, wk, n_dt, q_ref, qs_ref, kvs_ref, kvw_ref, ckv_ref, small_ref, ovt_ref, o_ref):
    i = pl.program_id(1)
    tq = Q_BLOCK
    dh = NSA_HEAD_DIM
    n_slc = seq // SLC_LEN
    nsp = -(-n_slc // LANES) * LANES
    nc = ckv_ref.shape[2]
    npair = NSA_HPG // 2
    rows = NSA_HPG * tq
    neg = -jnp.inf
    tpos_t = i * tq + lax.broadcasted_iota(jnp.int32, (1, tq), 1)
    q = q_ref[...]
    qs = qs_ref[...]
    gates_t = jax.nn.sigmoid(small_ref[...]).T

    def mask_heads(s, mask):
        return jnp.concatenate([jnp.where(mask, s[:, h * tq:(h + 1) * tq], neg) for h in range(NSA_HPG)], axis=1)

    def values_t(v):
        return v.astype(F32).T[0:dh].astype(BF16)

    def gated_pair_tiles(o_t, g, branch):
        tiles = []
        for p in range(npair):
            c_even = n_dt + 3 * (g * NSA_HPG + 2 * p) + branch
            even = o_t[:, p * tq:(p + 1) * tq] * gates_t[c_even:c_even + 1, :]
            odd = o_t[:, (npair + p) * tq:(npair + p + 1) * tq] * gates_t[c_even + 3:c_even + 4, :]
            tiles.append(jnp.concatenate([even, odd], axis=0))
        return tiles

    out_tiles, qgs, sels = [], [], []
    for g in range(NSA_KV):
        qg = jnp.concatenate([q[:, (g * npair + p) * LANES:(g * npair + p + 1) * LANES] for p in range(npair)]
                             + [qs[:, (g * npair + p) * LANES:(g * npair + p + 1) * LANES] for p in range(npair)],
                             axis=0)
        s_c = _dot_nt(ckv_ref[0, g].astype(BF16), qg)
        cidx = lax.broadcasted_iota(jnp.int32, (nc, 1), 0)
        s_c = mask_heads(s_c, (cidx * CMP_STRIDE + (CMP_LEN - 1) <= tpos_t) & (cidx < n_cmp))
        mx = jnp.max(s_c, axis=0, keepdims=True)
        e = jnp.exp(s_c - jnp.where(mx == neg, 0.0, mx))
        d = jnp.sum(e, axis=0, keepdims=True)
        p_c = e / jnp.where(d > 0, d, 1.0)
        vc_t = ckv_ref[0, NSA_KV + g].T[0:dh].astype(BF16)
        tiles = gated_pair_tiles(_dot(vc_t, p_c.astype(BF16)), g, 0)
        p_sum = p_c[:, 0:tq]
        for h in range(1, NSA_HPG):
            p_sum = p_sum + p_c[:, h * tq:(h + 1) * tq]
        ovt = ovt_ref[...]
        imp_t = sum(_dot(ovt, part) for part in _split3(p_sum))
        jidx = lax.broadcasted_iota(jnp.int32, (n_slc, 1), 0)
        qblk = tpos_t // SLC_LEN
        valid = jidx * SLC_LEN <= tpos_t
        forced = (jidx == 0) | (jidx == qblk) | (jidx == qblk - 1)
        score = jnp.where(valid, imp_t + FORCE_SCORE * forced.astype(F32), neg)
        rank = jnp.zeros((n_slc, tq), F32)
        for k in range(n_slc):
            row = score[k:k + 1, :]
            ahead = (row > score) | ((row == score) & (jidx > k))
            rank = rank + ahead.astype(F32)
        sel_t = (rank < float(min(SLC_TOPN, n_slc))).astype(BF16)
        if nsp > n_slc:
            sel_t = jnp.concatenate([sel_t, jnp.zeros((nsp - n_slc, tq), BF16)], axis=0)
        qgs.append(qg)
        sels.append(sel_t)

        w_start = jnp.clip((i - WINDOW // tq) * tq, 0, seq - wk)
        w_start = pl.multiple_of(w_start, tq)
        k_w = kvw_ref[pl.ds(w_start, wk), g * LANES:(g + 1) * LANES]
        v_w = kvw_ref[pl.ds(w_start, wk), (NSA_KV + g) * LANES:(NSA_KV + g + 1) * LANES]
        dpos = tpos_t - (w_start + lax.broadcasted_iota(jnp.int32, (wk, 1), 0))
        s_w = mask_heads(_dot_nt(k_w, qg), (dpos >= 0) & (dpos < WINDOW))
        e_w = jnp.exp(s_w - jnp.max(s_w, axis=0, keepdims=True))
        o_w = _dot(values_t(v_w), e_w.astype(BF16)) / jnp.sum(e_w, axis=0, keepdims=True)
        out_tiles.append([t + u for t, u in zip(tiles, gated_pair_tiles(o_w, g, 2))])

    def slc_step(c, carry):
        start = pl.multiple_of(c * ck, ck)
        blk = c * (ck // SLC_LEN) + lax.broadcasted_iota(jnp.int32, (ck, nsp), 0) // SLC_LEN
        expand_t = (lax.broadcasted_iota(jnp.int32, (ck, nsp), 1) == blk).astype(BF16)
        causal = start + lax.broadcasted_iota(jnp.int32, (ck, 1), 0) <= tpos_t
        new = []
        for g in range(NSA_KV):
            m_i, l_i, acc = carry[g]
            k_c = kvs_ref[pl.ds(start, ck), g * LANES:(g + 1) * LANES]
            v_c = kvs_ref[pl.ds(start, ck), (NSA_KV + g) * LANES:(NSA_KV + g + 1) * LANES]
            mb = (_dot(expand_t, sels[g]) > 0.5) & causal
            s = mask_heads(_dot_nt(k_c, qgs[g]), mb)
            m_new = jnp.maximum(m_i, jnp.max(s, axis=0, keepdims=True))
            alpha = jnp.exp(m_i - m_new)
            p = jnp.exp(s - m_new)
            l_new = alpha * l_i + jnp.sum(p, axis=0, keepdims=True)
            new.append((m_new, l_new, alpha * acc + _dot(values_t(v_c), p.astype(BF16))))
        return tuple(new)

    n_chunks = (i * tq + tq - 1) // ck + 1
    init = (jnp.full((1, rows), neg, F32), jnp.zeros((1, rows), F32), jnp.zeros((dh, rows), F32))
    final = lax.fori_loop(0, n_chunks, slc_step, (init,) * NSA_KV)
    tiles_all = []
    for g in range(NSA_KV):
        _, l_s, acc_s = final[g]
        tiles_all += [(t + u).T for t, u in zip(out_tiles[g], gated_pair_tiles(acc_s / l_s, g, 1))]
    o_ref[...] = jnp.concatenate(tiles_all, axis=1).astype(BF16)


def nsa_prompt_call(q_rot, q_swp, kv_pad, ckv, proj, bsz, seq, n_cmp, n_dt):
    nq = seq // Q_BLOCK
    nc = ckv.shape[2]
    n_slc = seq // SLC_LEN
    ck = min(512, seq)
    wk = min(WINDOW + Q_BLOCK, seq)
    ovt = jnp.asarray(np.pad(_overlap_np(n_cmp, n_slc), ((0, nc - n_cmp), (0, 0))).T, BF16)
    small_blk = proj.shape[1] // LANES - 1
    return pl.pallas_call(
        functools.partial(_nsa_t_body, seq, n_cmp, ck, wk, n_dt),
        grid=(bsz, nq),
        in_specs=[pl.BlockSpec((Q_BLOCK, 1024), lambda b, i: (b * nq + i, 0)),
                  pl.BlockSpec((Q_BLOCK, 1024), lambda b, i: (b * nq + i, 0)),
                  pl.BlockSpec((seq, 512), lambda b, i: (b, 0)),
                  pl.BlockSpec((seq, 512), lambda b, i: (b, 1)),
                  pl.BlockSpec((1, 4, nc, LANES), lambda b, i: (b, 0, 0, 0)),
                  pl.BlockSpec((Q_BLOCK, LANES), lambda b, i: (b * nq + i, small_blk)),
                  pl.BlockSpec((n_slc, nc), lambda b, i: (0, 0))],
        out_specs=pl.BlockSpec((Q_BLOCK, 1024), lambda b, i: (b * nq + i, 0)),
        out_shape=jax.ShapeDtypeStruct((bsz * seq, 1024), BF16),
        compiler_params=_params("parallel", "arbitrary"),
        name="nsa_prompt",
    )(q_rot, q_swp, kv_pad, kv_pad, ckv, proj, ovt)


def _page_copy(cache_hbm, pt_ref, buf, sem, b, p, half, slot):
    src = cache_hbm.at[pt_ref[b, p], :, pl.ds(half * LANES, LANES)]
    return pltpu.make_async_copy(src, buf.at[slot, p, half], sem.at[slot])


def _sample_nsa_body(n_pages, n_cmp, qpos, w_rows,
                     pt_ref, ccmp_hbm, cslc_hbm, q16_ref, qpad_ref, new_ref, win_ref, gates_ref,
                     pet_ref, w1t_ref, b1_ref, w2t_ref, b2_ref, ov_ref,
                     y_ref, buf, sem, sel_scr, oc_scr):
    s = pl.program_id(0)
    nb = pl.num_programs(0) // 2
    b = s // 2
    page = buf.shape[3]
    n_rows = n_pages * page
    nch = n_rows // CMP_STRIDE
    nsp = ov_ref.shape[1]
    heads = NSA_HEADS
    row_g0 = lax.broadcasted_iota(jnp.int32, (heads, 1), 0) < NSA_HPG

    def fetch(cache_hbm, bb, slot):
        for p in range(n_pages):
            for half in range(2):
                _page_copy(cache_hbm, pt_ref, buf, sem, bb, p, half, slot).start()

    def wait(cache_hbm, bb, slot):
        for p in range(n_pages):
            for half in range(2):
                _page_copy(cache_hbm, pt_ref, buf, sem, bb, p, half, slot).wait()

    @pl.when(s == 0)
    def _():
        fetch(ccmp_hbm, 0, 0)

    @pl.when(s % 2 == 0)
    def _():
        fetch(cslc_hbm, b, 1)
        wait(ccmp_hbm, b, 0)
        tops = [[] for _ in range(4)]
        bots = [[] for _ in range(4)]
        for r in range(CMP_STRIDE):
            for kind in range(2):
                x_r = buf[pl.ds(0, 1), :, pl.ds(kind, 1), pl.ds(r, page // CMP_STRIDE, stride=CMP_STRIDE), :]
                xt = x_r.reshape(nch, LANES).T
                pe_rows = pl.ds(kind * LANES, LANES)
                top = (xt + pet_ref[pe_rows, r:r + 1]).astype(BF16)
                bot = (xt + pet_ref[pe_rows, CMP_STRIDE + r:CMP_STRIDE + r + 1]).astype(BF16)
                for g in range(NSA_KV):
                    tops[kind * NSA_KV + g].append(top[g * NSA_HEAD_DIM:(g + 1) * NSA_HEAD_DIM])
                    bots[kind * NSA_KV + g].append(bot[g * NSA_HEAD_DIM:(g + 1) * NSA_HEAD_DIM])
        half_w = CMP_STRIDE * NSA_HEAD_DIM
        ckv_t = []
        for qq in range(4):
            kind = qq // NSA_KV
            w1t = w1t_ref[kind]
            hid = (_dot(w1t[:, :half_w], jnp.concatenate(tops[qq], axis=0))
                   + pltpu.roll(_dot(w1t[:, half_w:], jnp.concatenate(bots[qq], axis=0)), nch - 1, 1)
                   + b1_ref[kind])
            hid = hid * jax.nn.sigmoid(hid)
            ckv_t.append((_dot(w2t_ref[kind], hid.astype(BF16)) + b2_ref[kind]).astype(BF16))
        q16 = q16_ref[0]
        cidx = lax.broadcasted_iota(jnp.int32, (1, nch), 1)
        m_c = (cidx * CMP_STRIDE + (CMP_LEN - 1) <= qpos) & (cidx < n_cmp)
        oc = []
        for g in range(NSA_KV):
            s_c = jnp.where(m_c, _dot(q16, ckv_t[g]), -jnp.inf)
            mx = jnp.max(s_c, axis=-1, keepdims=True)
            mx = jnp.where(mx == -jnp.inf, 0.0, mx)
            e = jnp.exp(s_c - mx)
            d = jnp.sum(e, axis=-1, keepdims=True)
            p_c = e / jnp.where(d > 0, d, 1.0)
            oc.append(_dot_nt(p_c.astype(BF16), ckv_t[NSA_KV + g]))
            in_group = row_g0 if g == 0 else jnp.logical_not(row_g0)
            p_sum = jnp.sum(jnp.where(in_group, p_c, 0.0), axis=0, keepdims=True)
            imp = _dot3(jnp.broadcast_to(p_sum, (8, nch)), ov_ref[...])[0:1]
            jidx = lax.broadcasted_iota(jnp.int32, (1, nsp), 1)
            qblk = qpos // SLC_LEN
            forced = (jidx == 0) | (jidx == qblk) | (jidx == qblk - 1)
            score = jnp.where(jidx * SLC_LEN <= qpos, imp + FORCE_SCORE * forced.astype(F32), -jnp.inf)
            s_j = jnp.broadcast_to(score, (nsp, nsp))
            s_k = s_j.T
            kk = lax.broadcasted_iota(jnp.int32, (nsp, nsp), 0)
            jj = lax.broadcasted_iota(jnp.int32, (nsp, nsp), 1)
            ahead = (s_k > s_j) | ((s_k == s_j) & (kk < jj))
            rank = jnp.sum(ahead.astype(F32), axis=0, keepdims=True)
            sel_scr[g:g + 1, :] = (rank < float(SLC_TOPN)).astype(F32)
        oc_scr[...] = jnp.where(row_g0, oc[0], oc[1])

    @pl.when(s % 2 == 1)
    def _():
        @pl.when(b + 1 < nb)
        def _():
            fetch(ccmp_hbm, b + 1, 0)

        wait(cslc_hbm, b, 1)
        qpad = qpad_ref[0]
        sel16 = jnp.where(row_g0, jnp.broadcast_to(sel_scr[0:1, :], (heads, nsp)),
                          jnp.broadcast_to(sel_scr[1:2, :], (heads, nsp)))
        sel16_bf = sel16.astype(BF16)
        ppc = 8
        ck = ppc * page
        n_ck = n_rows // ck
        vals, scores = [], []
        for c in range(n_ck):
            k_c = buf[1, c * ppc:(c + 1) * ppc, 0].reshape(ck, LANES).astype(BF16)
            vals.append(buf[1, c * ppc:(c + 1) * ppc, 1].reshape(ck, LANES).astype(BF16))
            blk = c * (ck // SLC_LEN) + lax.broadcasted_iota(jnp.int32, (nsp, ck), 1) // SLC_LEN
            expand = (lax.broadcasted_iota(jnp.int32, (nsp, ck), 0) == blk).astype(BF16)
            kpos = c * ck + lax.broadcasted_iota(jnp.int32, (1, ck), 1)
            mask = (_dot(sel16_bf, expand) > 0.5) & (kpos <= qpos)
            scores.append(jnp.where(mask, _dot_nt(qpad, k_c), -jnp.inf))
        new = new_ref[0]
        ks_new = new[:, 0:LANES].astype(BF16)
        vs_new = new[:, LANES:2 * LANES].astype(BF16)
        kw_new = new[:, 2 * LANES:3 * LANES].astype(BF16)
        vw_new = new[:, 3 * LANES:].astype(BF16)
        first = lax.broadcasted_iota(jnp.int32, (1, new.shape[0]), 1) == 0
        nblk = n_rows // SLC_LEN
        s_new = jnp.where(first & (sel16[:, nblk:nblk + 1] > 0.5), _dot_nt(qpad, ks_new), -jnp.inf)
        mx = jnp.max(s_new, axis=-1, keepdims=True)
        for sc in scores:
            mx = jnp.maximum(mx, jnp.max(sc, axis=-1, keepdims=True))
        e_new = jnp.exp(s_new - mx)
        den = jnp.sum(e_new, axis=-1, keepdims=True)
        acc = _dot(e_new.astype(BF16), vs_new)
        for sc, v_c in zip(scores, vals):
            e = jnp.exp(sc - mx)
            den = den + jnp.sum(e, axis=-1, keepdims=True)
            acc = acc + _dot(e.astype(BF16), v_c)
        o_s = acc / den
        k_w = win_ref[0, :, 0:LANES].astype(BF16)
        v_w = win_ref[0, :, LANES:2 * LANES].astype(BF16)
        dpos = w_rows - lax.broadcasted_iota(jnp.int32, (1, w_rows), 1)
        m_w = (dpos >= 0) & (dpos < WINDOW) & (qpos - dpos >= 0)
        s_w = jnp.where(m_w, _dot_nt(qpad, k_w), -jnp.inf)
        s_wn = jnp.where(first, _dot_nt(qpad, kw_new), -jnp.inf)
        mxw = jnp.maximum(jnp.max(s_w, axis=-1, keepdims=True), jnp.max(s_wn, axis=-1, keepdims=True))
        e_w = jnp.exp(s_w - mxw)
        e_wn = jnp.exp(s_wn - mxw)
        o_w = ((_dot(e_w.astype(BF16), v_w) + _dot(e_wn.astype(BF16), vw_new))
               / (jnp.sum(e_w, axis=-1, keepdims=True) + jnp.sum(e_wn, axis=-1, keepdims=True)))

        def own_group(o):
            return jnp.where(row_g0, o, pltpu.roll(o, NSA_HEAD_DIM, 1))[:, :NSA_HEAD_DIM]

        gt = jax.nn.sigmoid(gates_ref[0])
        y_ref[0] = gt[:, 0:1] * oc_scr[...] + gt[:, 1:2] * own_group(o_s) + gt[:, 2:3] * own_group(o_w)


def sample_nsa_call(page_table, cache_cmp, cache_slc, cache_win, q_rot, kvs_new, kvw_new, gates, pe, w1, b1, w2, b2):
    db, n_pages = page_table.shape
    n_phys, page = cache_cmp.shape[:2]
    past_len = n_pages * page
    nch = past_len // CMP_STRIDE
    n_cmp = nch - CMP_LEN // CMP_STRIDE + 1
    n_slc = -(-(past_len + 1) // SLC_LEN)
    nsp = -(-n_slc // LANES) * LANES
    w_rows = cache_win.shape[1]
    lanes = 4 * NSA_HEAD_DIM
    q16 = q_rot.reshape(db, NSA_HEADS, NSA_HEAD_DIM)
    q4 = q_rot.reshape(db, NSA_KV, NSA_HPG, NSA_HEAD_DIM)
    qpad = jnp.concatenate([jnp.pad(q4[:, g], ((0, 0), (0, 0), (g * NSA_HEAD_DIM, LANES - (g + 1) * NSA_HEAD_DIM)))
                            for g in range(NSA_KV)], axis=1)
    new = jnp.pad(jnp.concatenate([kvs_new, kvw_new], axis=1)[:, None, :], ((0, 0), (0, 15), (0, 0)))
    pet = jnp.repeat(pe, NSA_KV, axis=0).transpose(0, 2, 1).reshape(lanes, CMP_LEN)
    ov = jnp.asarray(np.pad(_overlap_np(n_cmp, n_slc), ((0, nch - n_cmp), (0, nsp - n_slc))), BF16)
    hidden = w1.shape[-1]
    full = lambda shape: pl.BlockSpec(shape, lambda s, pt: (0,) * len(shape))
    per_b = lambda shape: pl.BlockSpec((1,) + shape, lambda s, pt: (s // 2,) + (0,) * len(shape))
    grid_spec = pltpu.PrefetchScalarGridSpec(
        num_scalar_prefetch=1,
        grid=(2 * db,),
        in_specs=[pl.BlockSpec(memory_space=pl.ANY), pl.BlockSpec(memory_space=pl.ANY),
                  per_b((NSA_HEADS, NSA_HEAD_DIM)), per_b((NSA_HEADS, LANES)), per_b((16, 2 * lanes)),
                  per_b((w_rows, lanes)), per_b((NSA_HEADS, 3)),
                  full((lanes, CMP_LEN)), full((2, hidden, CMP_LEN * NSA_HEAD_DIM)), full((2, hidden, 1)),
                  full((2, NSA_HEAD_DIM, hidden)), full((2, NSA_HEAD_DIM, 1)), full((nch, nsp))],
        out_specs=per_b((NSA_HEADS, NSA_HEAD_DIM)),
        scratch_shapes=[pltpu.VMEM((2, n_pages, 2, page, LANES), F32),
                        pltpu.SemaphoreType.DMA((2,)),
                        pltpu.VMEM((8, nsp), F32),
                        pltpu.VMEM((NSA_HEADS, NSA_HEAD_DIM), F32)])
    y = pl.pallas_call(
        functools.partial(_sample_nsa_body, n_pages, n_cmp, past_len, w_rows),
        grid_spec=grid_spec,
        out_shape=jax.ShapeDtypeStruct((db, NSA_HEADS, NSA_HEAD_DIM), F32),
        compiler_params=_params("arbitrary"),
        name="sample_nsa",
    )(page_table, cache_cmp.reshape(n_phys, page, lanes), cache_slc.reshape(n_phys, page, lanes),
      q16, qpad, new, cache_win.reshape(db, w_rows, lanes), gates.reshape(db, NSA_HEADS, 3),
      pet, w1.transpose(0, 2, 1).astype(BF16), b1.reshape(2, hidden, 1),
      w2.transpose(0, 2, 1).astype(BF16), b2.reshape(2, NSA_HEAD_DIM, 1), ov)
    return y.reshape(db, NSA_HEADS * NSA_HEAD_DIM)


def _softplus(x):
    return jnp.maximum(x, 0.0) + jnp.log1p(jnp.exp(-jnp.abs(x)))


def _ssd_body(z_ref, xbc_ref, small_ref, cw_ref, cb_ref, dtb_ref, dtbt_ref, alog_ref, alogt_ref,
              dskip_ref, nrm_ref, rexp_ref, ltri_ref, utri_ref, y_ref, st_ref, xbuf, ht):
    c = pl.program_id(1)
    q = SSM_CHUNK
    d_inner = z_ref.shape[1]
    n_heads = d_inner // SSM_HEAD_DIM
    gw = d_inner // SSM_GROUPS
    n = SSM_STATE

    @pl.when(c == 0)
    def _():
        xbuf[0:8, :] = jnp.zeros((8, xbuf.shape[1]), F32)
        ht[...] = jnp.zeros(ht.shape, F32)

    x = xbc_ref[...]
    xbuf[8:8 + q, :] = x
    cw = cw_ref[...]
    conv = (x * cw[3:4] + xbuf[7:7 + q, :] * cw[2:3] + xbuf[6:6 + q, :] * cw[1:2]
            + xbuf[5:5 + q, :] * cw[0:1] + cb_ref[...])
    xbuf[0:8, :] = x[q - 8:q]
    xc = conv * jax.nn.sigmoid(conv)
    xs = xc[:, :d_inner]
    bm = xc[:, d_inner:d_inner + SSM_GROUPS * n]
    cm = xc[:, d_inner + SSM_GROUPS * n:]

    small = small_ref[...]
    dt = _softplus(small[:, 0:n_heads] + dtb_ref[...])
    dtt = _softplus(small.T[0:n_heads, :] + dtbt_ref[...])
    a = -jnp.exp(alog_ref[...])
    at = -jnp.exp(alogt_ref[...])
    cum = _dot3_left(ltri_ref[...], dt * a)
    cumt = _dot3(dtt * at, utri_ref[...])
    cum_last = cum[q - 1:q, :]
    rexp = rexp_ref[...]
    expcum_f = _dot3(jnp.exp(cum), rexp)
    toend_f = _dot3(jnp.exp(cum_last - cum) * dt, rexp)
    cdec_f = _dot3(jnp.broadcast_to(jnp.exp(cum_last), (8, n_heads)), rexp)[0:1]

    causal = (lax.broadcasted_iota(jnp.int32, (q, q), 0) >= lax.broadcasted_iota(jnp.int32, (q, q), 1))
    lane_lo = lax.broadcasted_iota(jnp.int32, (q, LANES), 1) < SSM_HEAD_DIM
    xs_bf = xs.astype(BF16)
    zero_bf = jnp.zeros((q, LANES), BF16)
    hpg = n_heads // SSM_GROUPS
    y_tiles, yoff_tiles = [], []
    for g in range(SSM_GROUPS):
        bg = bm[:, g * n:(g + 1) * n]
        cg_bf = cm[:, g * n:(g + 1) * n].astype(BF16)
        cb = _dot_nt(cg_bf, bg.astype(BF16))
        h_old = ht[g]
        yoff_tiles.append(_dot(cg_bf, h_old.astype(BF16)))
        for pr in range(hpg // 2):
            h0 = g * hpg + 2 * pr
            xp = xs_bf[:, h0 * SSM_HEAD_DIM:h0 * SSM_HEAD_DIM + LANES]
            acc = None
            for par, xm in ((0, jnp.where(lane_lo, xp, zero_bf)), (1, jnp.where(lane_lo, zero_bf, xp))):
                h = h0 + par
                seg = cum[:, h:h + 1] - cumt[h:h + 1, :]
                w = cb * jnp.exp(jnp.where(causal, seg, -jnp.inf)) * dtt[h:h + 1, :]
                part = _dot(w.astype(BF16), xm)
                acc = part if acc is None else acc + part
            y_tiles.append(acc)
        xw = (xs[:, g * gw:(g + 1) * gw] * toend_f[:, g * gw:(g + 1) * gw]).astype(BF16)
        ht[g] = h_old * cdec_f[:, g * gw:(g + 1) * gw] + _dot(bg.T.astype(BF16), xw)

    y = (jnp.concatenate(y_tiles, axis=1) + jnp.concatenate(yoff_tiles, axis=1) * expcum_f
         + dskip_ref[...] * xs)
    z = z_ref[...]
    y = y * (z * jax.nn.sigmoid(z))
    outs = []
    for g in range(SSM_GROUPS):
        yg = y[:, g * gw:(g + 1) * gw]
        ms = jnp.mean(yg * yg, axis=-1, keepdims=True)
        outs.append(yg * lax.rsqrt(ms + EPS) * nrm_ref[:, g * gw:(g + 1) * gw])
    y_ref[...] = jnp.concatenate(outs, axis=1).astype(BF16)
    st_ref[0] = ht[...]


def ssd_call(proj, conv_w, conv_b, dt_bias, a_log, d_skip, ssm_norm, bsz, seq):
    q = SSM_CHUNK
    nchunk = seq // q
    n_heads = dt_bias.shape[0]
    d_inner = n_heads * SSM_HEAD_DIM
    conv_dim = conv_w.shape[1]
    gw = d_inner // SSM_GROUPS
    small_blk = proj.shape[1] // LANES - 1
    rexp = jnp.asarray(np.repeat(np.eye(n_heads, dtype=np.float32), SSM_HEAD_DIM, axis=1), BF16)
    ltri = jnp.asarray(np.tril(np.ones((q, q), np.float32)), BF16)
    utri = jnp.asarray(np.triu(np.ones((q, q), np.float32)), BF16)
    const = lambda shape: pl.BlockSpec(shape, lambda b, c: (0,) * len(shape))
    return pl.pallas_call(
        _ssd_body,
        grid=(bsz, nchunk),
        in_specs=[pl.BlockSpec((q, d_inner), lambda b, c: (b * nchunk + c, 0)),
                  pl.BlockSpec((q, conv_dim), lambda b, c: (b * nchunk + c, 1)),
                  pl.BlockSpec((q, LANES), lambda b, c: (b * nchunk + c, small_blk)),
                  const((SSM_CONV, conv_dim)), const((1, conv_dim)),
                  const((1, n_heads)), const((n_heads, 1)), const((1, n_heads)), const((n_heads, 1)),
                  const((1, d_inner)), const((1, d_inner)),
                  const((n_heads, d_inner)), const((q, q)), const((q, q))],
        out_specs=[pl.BlockSpec((q, d_inner), lambda b, c: (b * nchunk + c, 0)),
                   pl.BlockSpec((1, SSM_GROUPS, SSM_STATE, gw), lambda b, c: (b, 0, 0, 0))],
        out_shape=[jax.ShapeDtypeStruct((bsz * seq, d_inner), BF16),
                   jax.ShapeDtypeStruct((bsz, SSM_GROUPS, SSM_STATE, gw), F32)],
        scratch_shapes=[pltpu.VMEM((8 + q, conv_dim), F32),
                        pltpu.VMEM((SSM_GROUPS, SSM_STATE, gw), F32)],
        compiler_params=_params("parallel", "arbitrary"),
        name="ssd",
    )(proj, proj, proj, conv_w, conv_b.reshape(1, -1),
      dt_bias.reshape(1, -1), dt_bias.reshape(-1, 1), a_log.reshape(1, -1), a_log.reshape(-1, 1),
      jnp.repeat(d_skip, SSM_HEAD_DIM).reshape(1, -1), ssm_norm.reshape(1, -1), rexp, ltri, utri)


CONF_HALO = 32


def _conf_body(u_ref, w_ref, b_ref, g_ref, beta_ref, o_ref, tail_ref, buf, shifted):
    t = pl.program_id(1)
    tm, d = o_ref.shape
    sub = 8

    @pl.when(t == 0)
    def _():
        buf[0:CONF_HALO, :] = jnp.zeros((CONF_HALO, d), F32)

    u = u_ref[...]
    a = u[:, :d] * jax.nn.sigmoid(u[:, d:])
    buf[CONF_HALO:CONF_HALO + tm, :] = a
    span = shifted.shape[1]
    for s in range(1, sub):
        shifted[s - 1] = buf[s:s + span, :]
    w = w_ref[...]
    off = CONF_HALO - (CONF_CONV - 1)
    acc = b_ref[...]
    for k in range(CONF_CONV):
        s, base = (off + k) % sub, (off + k) // sub * sub
        rows = buf[base:base + tm, :] if s == 0 else shifted[s - 1, base:base + tm, :]
        acc = acc + rows * w[k:k + 1]
    buf[0:CONF_HALO, :] = a[tm - CONF_HALO:tm]
    tail_ref[0] = a[tm - CONF_HALO:tm]
    mu = jnp.mean(acc, axis=-1, keepdims=True)
    cen = acc - mu
    var = jnp.mean(cen * cen, axis=-1, keepdims=True)
    y = cen * lax.rsqrt(var + EPS) * g_ref[...] + beta_ref[...]
    o_ref[...] = (y * jax.nn.sigmoid(y)).astype(BF16)


def conformer_mid_call(u, dw_w, dw_b, ln_g, ln_b, bsz, seq, tm):
    d = dw_w.shape[1]
    nt = seq // tm
    const = lambda shape: pl.BlockSpec(shape, lambda b, t: (0,) * len(shape))
    return pl.pallas_call(
        _conf_body,
        grid=(bsz, nt),
        in_specs=[pl.BlockSpec((tm, 2 * d), lambda b, t: (b * nt + t, 0)),
                  const((CONF_CONV, d)), const((1, d)), const((1, d)), const((1, d))],
        out_specs=[pl.BlockSpec((tm, d), lambda b, t: (b * nt + t, 0)),
                   pl.BlockSpec((1, CONF_HALO, d), lambda b, t: (b, 0, 0))],
        out_shape=[jax.ShapeDtypeStruct((bsz * seq, d), BF16),
                   jax.ShapeDtypeStruct((bsz, CONF_HALO, d), F32)],
        scratch_shapes=[pltpu.VMEM((CONF_HALO + tm, d), F32),
                        pltpu.VMEM((7, CONF_HALO - 8 + tm, d), F32)],
        compiler_params=_params("parallel", "arbitrary"),
        name="conformer_mid",
    )(u, dw_w, dw_b.reshape(1, d), ln_g.reshape(1, d), ln_b.reshape(1, d))


FFN_HALO = 8


def _ffn_up_body(x_ref, g_ref, wg_ref, wv_ref, cwg_ref, cwv_ref, cbg_ref, cbv_ref,
                 o_ref, tg_ref, tv_ref, xn_ref, ubuf, carry):
    t = pl.program_id(1)
    j = pl.program_id(2)
    tm = x_ref.shape[0]

    @pl.when(j == 0)
    def _():
        x = x_ref[...]
        ms = jnp.mean(x * x, axis=-1, keepdims=True)
        xn_ref[...] = (x * lax.rsqrt(ms + EPS) * g_ref[...]).astype(BF16)

    @pl.when(t == 0)
    def _():
        carry[j] = jnp.zeros(carry.shape[1:], F32)

    xn = xn_ref[...]
    halves = []
    for part, (w_ref, cw_ref, cb_ref, tail_ref) in enumerate(((wg_ref, cwg_ref, cbg_ref, tg_ref),
                                                              (wv_ref, cwv_ref, cbv_ref, tv_ref))):
        u = _dot(xn, w_ref[...])
        ubuf[part, 0:FFN_HALO, :] = carry[j, part]
        ubuf[part, FFN_HALO:FFN_HALO + tm, :] = u
        cw = cw_ref[...]
        halves.append(u * cw[2:3] + ubuf[part, FFN_HALO - 1:FFN_HALO - 1 + tm, :] * cw[1:2]
                      + ubuf[part, FFN_HALO - 2:FFN_HALO - 2 + tm, :] * cw[0:1] + cb_ref[...])
        carry[j, part] = u[tm - FFN_HALO:tm]
        tail_ref[0, j] = u[tm - FFN_HALO:tm]
    o_ref[...] = (jax.nn.gelu(halves[0], approximate=True) * halves[1]).astype(BF16)


def ffn_up_call(x, g, w_up, conv_w, conv_b, bsz, seq, tm, tn):
    m, k = x.shape
    d2 = w_up.shape[1]
    dff = d2 // 2
    nt = seq // tm
    nj = dff // tn
    cb = conv_b.reshape(1, d2)
    row = lambda b, t, j: b * nt + t
    act, tail_g, tail_v = pl.pallas_call(
        _ffn_up_body,
        grid=(bsz, nt, nj),
        in_specs=[pl.BlockSpec((tm, k), lambda b, t, j: (row(b, t, j), 0)),
                  pl.BlockSpec((1, k), lambda b, t, j: (0, 0)),
                  pl.BlockSpec((k, tn), lambda b, t, j: (0, j)),
                  pl.BlockSpec((k, tn), lambda b, t, j: (0, j + nj)),
                  pl.BlockSpec((FFN_CONV, tn), lambda b, t, j: (0, j)),
                  pl.BlockSpec((FFN_CONV, tn), lambda b, t, j: (0, j + nj)),
                  pl.BlockSpec((1, tn), lambda b, t, j: (0, j)),
                  pl.BlockSpec((1, tn), lambda b, t, j: (0, j + nj))],
        out_specs=[pl.BlockSpec((tm, tn), lambda b, t, j: (row(b, t, j), j)),
                   pl.BlockSpec((1, nj, FFN_HALO, tn), lambda b, t, j: (b, 0, 0, 0)),
                   pl.BlockSpec((1, nj, FFN_HALO, tn), lambda b, t, j: (b, 0, 0, 0))],
        out_shape=[jax.ShapeDtypeStruct((m, dff), BF16),
                   jax.ShapeDtypeStruct((bsz, nj, FFN_HALO, tn), F32),
                   jax.ShapeDtypeStruct((bsz, nj, FFN_HALO, tn), F32)],
        scratch_shapes=[pltpu.VMEM((tm, k), BF16),
                        pltpu.VMEM((2, FFN_HALO + tm, tn), F32),
                        pltpu.VMEM((nj, 2, FFN_HALO, tn), F32)],
        compiler_params=_params("parallel", "arbitrary", "arbitrary"),
        name="ffn_up",
    )(x, g.reshape(1, k), w_up, w_up, conv_w, conv_w, cb, cb)
    flat = lambda tail: tail.transpose(0, 2, 1, 3).reshape(bsz, FFN_HALO, dff)
    return act, jnp.concatenate([flat(tail_g), flat(tail_v)], axis=-1)


def _rms(x, g):
    return x * lax.rsqrt(jnp.mean(x * x, axis=-1, keepdims=True) + EPS) * g


def _masked_softmax(s, mask):
    s = jnp.where(mask, s, -jnp.inf)
    m = jnp.max(s, axis=-1, keepdims=True)
    m = jnp.where(jnp.isfinite(m), m, 0.0)
    e = jnp.exp(s - m)
    d = jnp.sum(e, axis=-1, keepdims=True)
    return e / jnp.where(d > 0, d, 1.0)


def _sample_ssd_step(z, xbc, dt_raw, conv_hist, h0, conv_w, conv_b, dt_bias, a_log, d_skip, ssm_norm):
    bsz = z.shape[0]
    n_heads = dt_bias.shape[0]
    d_inner = n_heads * SSM_HEAD_DIM
    hpg = n_heads // SSM_GROUPS
    xh = jnp.concatenate([conv_hist, xbc[:, None, :]], axis=1)
    new_hist = xh[:, 1:]
    conv = jnp.sum(xh * conv_w[None], axis=1) + conv_b
    xc = conv * jax.nn.sigmoid(conv)
    xs = xc[:, :d_inner].reshape(bsz, SSM_GROUPS, hpg, SSM_HEAD_DIM)
    bm = xc[:, d_inner:d_inner + SSM_GROUPS * SSM_STATE].reshape(bsz, SSM_GROUPS, SSM_STATE)
    cm = xc[:, d_inner + SSM_GROUPS * SSM_STATE:].reshape(bsz, SSM_GROUPS, SSM_STATE)
    dt = jax.nn.softplus(dt_raw + dt_bias).reshape(bsz, SSM_GROUPS, hpg)
    a = (-jnp.exp(a_log)).reshape(SSM_GROUPS, hpg)
    dec = jnp.exp(dt * a)
    h0g = h0.reshape(bsz, SSM_GROUPS, hpg, SSM_HEAD_DIM, SSM_STATE)
    h_new = dec[..., None, None] * h0g + (dt[..., None] * xs)[..., None] * bm[:, :, None, None, :]
    y = jnp.sum(h_new * cm[:, :, None, None, :], axis=-1)
    y = y + d_skip.reshape(SSM_GROUPS, hpg)[None, :, :, None] * xs
    y = y.reshape(bsz, d_inner) * (z * jax.nn.sigmoid(z))
    y = _rms(y.reshape(bsz, SSM_GROUPS, d_inner // SSM_GROUPS),
             ssm_norm.reshape(SSM_GROUPS, d_inner // SSM_GROUPS)).reshape(bsz, d_inner)
    return y, new_hist, h_new.reshape(h0.shape)


def _compress_rows_jnp(rows, pe, w1, b1, w2, b2):
    bsz, t = rows.shape[:2]
    r = CMP_LEN // CMP_STRIDE
    nch = t // CMP_STRIDE
    n_cmp = nch - r + 1
    ch = rows[:, :nch * CMP_STRIDE].reshape(bsz, nch, CMP_STRIDE, NSA_KV, NSA_HEAD_DIM)
    blk = jnp.concatenate([ch[:, k:k + n_cmp] for k in range(r)], axis=2)
    blk = blk + pe[None, None, :, None, :]
    flat = blk.transpose(0, 1, 3, 2, 4).reshape(bsz, n_cmp, NSA_KV, CMP_LEN * NSA_HEAD_DIM)
    hid = jax.nn.silu(flat @ w1 + b1)
    return hid @ w2 + b2


def _sample_nsa(q, gates, kvc, kvs, kvw, cache_cmp, cache_slc, cache_win, page_table, cmp_w):
    db = q.shape[0]
    past_len = page_table.shape[1] * cache_cmp.shape[1]
    pe, w1, b1, w2, b2 = cmp_w
    gather = lambda cache: cache[page_table].reshape((db, past_len) + cache.shape[2:])
    rows_cmp = jnp.concatenate([gather(cache_cmp), kvc], axis=1)
    rows_slc = jnp.concatenate([gather(cache_slc), kvs], axis=1)
    kc = _compress_rows_jnp(rows_cmp[:, :, 0], pe[0], w1[0], b1[0], w2[0], b2[0])
    vc = _compress_rows_jnp(rows_cmp[:, :, 1], pe[1], w1[1], b1[1], w2[1], b2[1])
    n_cmp = kc.shape[1]
    cmp_end = jnp.arange(n_cmp) * CMP_STRIDE + CMP_LEN - 1
    t_all = rows_slc.shape[1]
    n_slc = -(-t_all // SLC_LEN)
    slc = jnp.pad(rows_slc, [(0, 0), (0, n_slc * SLC_LEN - t_all), (0, 0), (0, 0), (0, 0)])
    slc = slc.reshape(db, n_slc, SLC_LEN, 2, NSA_KV, NSA_HEAD_DIM).transpose(0, 4, 1, 2, 3, 5)
    wkv = jnp.concatenate([cache_win, kvw], axis=1)
    w_rows = cache_win.shape[1]
    wpos = past_len - w_rows + jnp.arange(w_rows + 1)
    qpos = past_len
    qg = (q[:, 0] * NSA_HEAD_DIM ** -0.5).reshape(db, NSA_KV, NSA_HPG, NSA_HEAD_DIM)
    s_c = jnp.einsum('bghd,bngd->bghn', qg, kc)
    p_c = _masked_softmax(s_c, (cmp_end <= qpos)[None, None, None, :])
    o_c = jnp.einsum('bghn,bngd->bghd', p_c, vc)
    imp = jnp.einsum('bghn,nj->bgj', p_c, jnp.asarray(_overlap_np(n_cmp, n_slc)),
                     precision=lax.Precision.HIGHEST)
    j = jnp.arange(n_slc)
    qblk = qpos // SLC_LEN
    forced = (j == 0) | (j == qblk) | (j == qblk - 1)
    score = jnp.where((j * SLC_LEN <= qpos)[None, None, :], imp + FORCE_SCORE * forced[None, None, :], -jnp.inf)
    _, idx = lax.top_k(score, min(SLC_TOPN, n_slc))
    bi = jnp.arange(db)[:, None, None]
    gi = jnp.arange(NSA_KV)[None, :, None]
    sel = slc[bi, gi, idx]
    s_s = jnp.einsum('bghd,bgnld->bghnl', qg, sel[..., 0, :])
    kpos = idx[..., None] * SLC_LEN + jnp.arange(SLC_LEN)
    m_s = (kpos <= qpos)[:, :, None]
    p_s = _masked_softmax(s_s.reshape(db, NSA_KV, NSA_HPG, -1), m_s.reshape(db, NSA_KV, 1, -1)).reshape(s_s.shape)
    o_s = jnp.einsum('bghnl,bgnld->bghd', p_s, sel[..., 1, :])
    s_w = jnp.einsum('bghd,bsgd->bghs', qg, wkv[:, :, 0])
    dpos = qpos - wpos
    p_w = _masked_softmax(s_w, ((dpos >= 0) & (dpos < WINDOW) & (wpos >= 0))[None, None, None, :])
    o_w = jnp.einsum('bghs,bsgd->bghd', p_w, wkv[:, :, 1])
    gt = jax.nn.sigmoid(gates).reshape(db, NSA_KV, NSA_HPG, 3)
    o = gt[..., 0:1] * o_c + gt[..., 1:2] * o_s + gt[..., 2:3] * o_w
    return o.reshape(db, NSA_HEADS * NSA_HEAD_DIM)


def _prep_w_in(w_in, d_inner, conv_dim, n_heads):
    nq = NSA_HEADS * NSA_HEAD_DIM
    nkv = 2 * NSA_KV * NSA_HEAD_DIM
    o = np.cumsum([0, d_inner, conv_dim, n_heads, nq, nkv, nkv, nkv, 3 * NSA_HEADS])
    z, xbc, dt, q, kvc, kvs, kvw, gates = (w_in[:, o[k]:o[k + 1]] for k in range(8))
    small = jnp.concatenate([dt, gates], axis=1)
    small = jnp.pad(small, ((0, 0), (0, LANES - small.shape[1])))
    return jnp.concatenate([z, q, xbc, kvc, kvs, kvw, small], axis=1).astype(BF16)


def _pick_tile(m, pref):
    return pref if m % pref == 0 else m


def _col_tile(n, target):
    best = None
    for t in range(LANES, min(n, target) + 1, LANES):
        if n % t == 0:
            best = t
    return best or n


def kernel(x_prompt, x_sample, state_ssm, state_ssm_conv, cache_cmp, cache_slc, cache_win, state_conf_conv, state_ffn_conv, page_table, norm_g, ab_w_in, ab_conv_w, ab_conv_b, ab_dt_bias, ab_a_log, ab_d_skip, ab_ssm_norm, ab_cmp_pe, ab_cmp_w1, ab_cmp_b1, ab_cmp_w2, ab_cmp_b2, ab_w_out, c_w_pw1, c_b_pw1, c_dw_w, c_dw_b, c_ln_g, c_ln_b, c_w_pw2, c_b_pw2, ffn_w_up, ffn_conv_w, ffn_conv_b, ffn_w_down):
    bsz, seq, d_model = x_prompt.shape
    db, dseq, _ = x_sample.shape
    assert dseq == 1
    depth = norm_g.shape[0]
    n_heads = ab_dt_bias.shape[1]
    d_inner = n_heads * SSM_HEAD_DIM
    conv_dim = ab_conv_w.shape[2]
    d_ff2 = ffn_w_up.shape[2]
    d_conf = c_dw_w.shape[2] if c_dw_w.shape[0] else d_model
    past_len = page_table.shape[1] * cache_cmp.shape[2]
    mp, ms = bsz * seq, db
    hp = x_prompt.reshape(mp, d_model)
    hs = x_sample.reshape(ms, d_model)
    tm_p = _pick_tile(mp, 1024)
    tm_post = _pick_tile(mp, 512)
    tm_seq = _pick_tile(seq, 256)
    tn_ff = _col_tile(d_ff2 // 2, 1536)

    cos_p, sin_p = _rope_tables(jnp.arange(seq, dtype=jnp.int32))
    cos_s, sin_s = _rope_tables(jnp.full((ms,), past_len, jnp.int32))
    nchk = seq // CMP_STRIDE
    n_cmp = nchk - CMP_LEN // CMP_STRIDE + 1

    outs = {k: [] for k in ("ssm_p", "ssm_s", "sconv_p", "sconv_s", "cmp_p", "cmp_s", "slc_p", "slc_s",
                            "win_p", "win_s", "conf_p", "conf_s", "ffn_p", "ffn_s")}
    kvshape = (2, NSA_KV, NSA_HEAD_DIM)
    for layer in range(depth):
        g = norm_g[layer]
        if layer % 2 == 0:
            a = layer // 2
            w_in = _prep_w_in(ab_w_in[a], d_inner, conv_dim, n_heads)
            n_proj = w_in.shape[1]
            zero_b = jnp.zeros((n_proj,), F32)
            w_out = ab_w_out[a].astype(BF16)
            zero_d = jnp.zeros((d_model,), F32)
            tn_in = _col_tile(n_proj, 1536)
            proj = norm_matmul(hp, g[0], w_in, zero_b, tm_p, tn_in)
            q_rot, q_swp, kv_pad, kvc, kvs, kvw = rope_call(proj, cos_p, sin_p, _pick_tile(seq, 512))
            ch = kvc.reshape(bsz, nchk, CMP_STRIDE, 4, NSA_HEAD_DIM).transpose(0, 3, 1, 2, 4)
            ch = ch.reshape(bsz, 4, nchk, CMP_STRIDE * NSA_HEAD_DIM)
            pe_flat = ab_cmp_pe[a].reshape(2, 1, CMP_LEN * NSA_HEAD_DIM)
            w2p = jnp.pad(ab_cmp_w2[a], ((0, 0), (0, 0), (0, LANES - NSA_HEAD_DIM))).astype(BF16)
            b2p = jnp.pad(ab_cmp_b2[a], ((0, 0), (0, LANES - NSA_HEAD_DIM))).reshape(2, 1, LANES)
            ckv = compress_call(ch, pe_flat, ab_cmp_w1[a].astype(BF16), ab_cmp_b1[a].reshape(2, 1, -1), w2p, b2p)
            y_nsa = nsa_prompt_call(q_rot, q_swp, kv_pad, ckv, proj, bsz, seq, n_cmp, n_heads)
            y_ssm, st = ssd_call(proj, ab_conv_w[a], ab_conv_b[a], ab_dt_bias[a], ab_a_log[a],
                                 ab_d_skip[a], ab_ssm_norm[a], bsz, seq)
            hp = matmul_post([y_ssm, y_nsa], [w_out[:d_inner], w_out[d_inner:]], zero_d, hp, g[1], tm_post)
            hpg = n_heads // SSM_GROUPS
            outs["ssm_p"].append(st.reshape(bsz, SSM_GROUPS, SSM_STATE, hpg, SSM_HEAD_DIM)
                                 .transpose(0, 1, 3, 4, 2).reshape(bsz, n_heads, SSM_HEAD_DIM, SSM_STATE))
            tail_p = proj.reshape(bsz, seq, n_proj)[:, seq - (SSM_CONV - 1):]
            outs["sconv_p"].append(tail_p[:, :, d_inner + 1024:d_inner + 1024 + conv_dim])
            outs["cmp_p"].append(kvc.reshape((bsz, seq) + kvshape))
            outs["slc_p"].append(kvs.reshape((bsz, seq) + kvshape))
            outs["win_p"].append(kvw.reshape((bsz, seq) + kvshape)[:, seq - min(WINDOW, seq):])
            proj_s = norm_matmul(hs, g[0], w_in, zero_b, ms, tn_in)
            q_s, _, _, kvc_s, kvs_s, kvw_s = rope_call(proj_s, cos_s, sin_s, ms)
            z_s = proj_s[:, :d_inner]
            xbc_s = proj_s[:, d_inner + 1024:d_inner + 1024 + conv_dim]
            dt_s = proj_s[:, n_proj - LANES:n_proj - LANES + n_heads]
            gates_s = proj_s[:, n_proj - LANES + n_heads:n_proj - LANES + n_heads + 3 * NSA_HEADS]
            ys_ssm, hist_s, h_s = _sample_ssd_step(z_s, xbc_s, dt_s, state_ssm_conv[a], state_ssm[a],
                                                   ab_conv_w[a], ab_conv_b[a], ab_dt_bias[a], ab_a_log[a],
                                                   ab_d_skip[a], ab_ssm_norm[a])
            ys_nsa = sample_nsa_call(page_table, cache_cmp[a], cache_slc[a], cache_win[a], q_s, kvs_s, kvw_s,
                                     gates_s, ab_cmp_pe[a], ab_cmp_w1[a], ab_cmp_b1[a], ab_cmp_w2[a], ab_cmp_b2[a])
            kvc_s = kvc_s.reshape((ms, 1) + kvshape)
            kvs_s = kvs_s.reshape((ms, 1) + kvshape)
            kvw_s = kvw_s.reshape((ms, 1) + kvshape)
            hs = matmul_post([ys_ssm.astype(BF16), ys_nsa.astype(BF16)], [w_out[:d_inner], w_out[d_inner:]],
                             zero_d, hs, g[1], ms)
            outs["ssm_s"].append(h_s)
            outs["sconv_s"].append(hist_s)
            outs["cmp_s"].append(kvc_s)
            outs["slc_s"].append(kvs_s)
            outs["win_s"].append(kvw_s)
        else:
            c = layer // 2
            w1 = c_w_pw1[c].astype(BF16)
            w2 = c_w_pw2[c].astype(BF16)
            u = norm_matmul(hp, g[0], w1, c_b_pw1[c], tm_p, 1024)
            act, tail = conformer_mid_call(u, c_dw_w[c], c_dw_b[c], c_ln_g[c], c_ln_b[c], bsz, seq, tm_seq)
            hp = matmul_post([act], [w2], c_b_pw2[c], hp, g[1], tm_post)
            outs["conf_p"].append(tail[:, CONF_HALO - (CONF_CONV - 1):])
            u_s = norm_matmul(hs, g[0], w1, c_b_pw1[c], ms, 512)
            a_s = u_s[:, :d_conf] * jax.nn.sigmoid(u_s[:, d_conf:])
            ah = jnp.concatenate([state_conf_conv[c], a_s[:, None, :]], axis=1)
            conv = jnp.sum(ah * c_dw_w[c][None], axis=1) + c_dw_b[c]
            mu = jnp.mean(conv, axis=-1, keepdims=True)
            var = jnp.mean(jnp.square(conv - mu), axis=-1, keepdims=True)
            yln = (conv - mu) * lax.rsqrt(var + EPS) * c_ln_g[c] + c_ln_b[c]
            hs = matmul_post([(yln * jax.nn.sigmoid(yln)).astype(BF16)], [w2], c_b_pw2[c], hs, g[1], ms)
            outs["conf_s"].append(ah[:, 1:])
        w_up = ffn_w_up[layer].astype(BF16)
        w_dn = ffn_w_down[layer].astype(BF16)
        zero_up = jnp.zeros((d_ff2,), F32)
        zero_d = jnp.zeros((d_model,), F32)
        act, tail_u = ffn_up_call(hp, g[2], w_up, ffn_conv_w[layer], ffn_conv_b[layer], bsz, seq,
                                  _pick_tile(seq, 512), tn_ff)
        hp = matmul_post([act], [w_dn], zero_d, hp, g[3], tm_post)
        outs["ffn_p"].append(tail_u[:, FFN_HALO - (FFN_CONV - 1):])
        u_s = norm_matmul(hs, g[2], w_up, zero_up, ms, tn_ff)
        uh = jnp.concatenate([state_ffn_conv[layer], u_s[:, None, :]], axis=1)
        y_s = jnp.sum(uh * ffn_conv_w[layer][None], axis=1) + ffn_conv_b[layer]
        act_s = jax.nn.gelu(y_s[:, :d_ff2 // 2], approximate=True) * y_s[:, d_ff2 // 2:]
        hs = matmul_post([act_s.astype(BF16)], [w_dn], zero_d, hs, g[3], ms)
        outs["ffn_s"].append(uh[:, 1:])

    st = lambda k: jnp.stack(outs[k])
    return (hp.reshape(bsz, seq, d_model), hs.reshape(db, dseq, d_model),
            st("ssm_p"), st("ssm_s"), st("sconv_p"), st("sconv_s"), st("cmp_p"), st("cmp_s"),
            st("slc_p"), st("slc_s"), st("win_p"), st("win_s"), st("conf_p"), st("conf_s"),
            st("ffn_p"), st("ffn_s"))
```

```python
import functools
import math

import jax
import jax.numpy as jnp
import numpy as np
from jax import lax
from jax.experimental import pallas as pl
from jax.experimental.pallas import tpu as pltpu

F32 = jnp.float32
BF16 = jnp.bfloat16

EPS = 1e-6
LANES = 128
VMEM_LIMIT = 48 * 1024 * 1024

SSM_HEAD_DIM = 64
SSM_GROUPS = 4
SSM_STATE = 128
SSM_CONV = 4
SSM_CHUNK = 128
NSA_HEADS = 16
NSA_KV = 2
NSA_HPG = NSA_HEADS // NSA_KV
NSA_HEAD_DIM = 64
CMP_LEN = 32
CMP_STRIDE = 16
SLC_LEN = 64
SLC_TOPN = 16
WINDOW = 512
Q_BLOCK = 128
ROPE_THETA = 10000.0
FORCE_SCORE = 1e4
QK_SCALE_LOG2 = NSA_HEAD_DIM ** -0.5 * math.log2(math.e)
CONF_CONV = 31
FFN_CONV = 3


def _params(*sem):
    return pltpu.CompilerParams(dimension_semantics=sem, vmem_limit_bytes=VMEM_LIMIT)


def _split3(v):
    hi = v.astype(BF16)
    r1 = v - hi.astype(F32)
    mid = r1.astype(BF16)
    lo = (r1 - mid.astype(F32)).astype(BF16)
    return hi, mid, lo


def _dot(a, b):
    return jnp.dot(a, b, preferred_element_type=F32)


def _dot_nt(a, b):
    return lax.dot_general(a, b, (((1,), (1,)), ((), ())), preferred_element_type=F32)


def _dot3(v, m_bf16):
    hi, mid, lo = _split3(v)
    return _dot(hi, m_bf16) + _dot(mid, m_bf16) + _dot(lo, m_bf16)


def _dot3_left(m_bf16, v):
    hi, mid, lo = _split3(v)
    return _dot(m_bf16, hi) + _dot(m_bf16, mid) + _dot(m_bf16, lo)


def _norm_matmul_body(x_ref, g_ref, w_ref, b_ref, o_ref, xn_ref):
    @pl.when(pl.program_id(1) == 0)
    def _():
        x = x_ref[...]
        ms = jnp.mean(x * x, axis=-1, keepdims=True)
        xn_ref[...] = (x * lax.rsqrt(ms + EPS) * g_ref[...]).astype(BF16)

    o_ref[...] = _dot(xn_ref[...], w_ref[...]) + b_ref[...]


def norm_matmul(x, g, w, b, tm, tn):
    m, k = x.shape
    n = w.shape[1]
    return pl.pallas_call(
        _norm_matmul_body,
        grid=(m // tm, n // tn),
        in_specs=[pl.BlockSpec((tm, k), lambda i, j: (i, 0)),
                  pl.BlockSpec((1, k), lambda i, j: (0, 0)),
                  pl.BlockSpec((k, tn), lambda i, j: (0, j)),
                  pl.BlockSpec((1, tn), lambda i, j: (0, j))],
        out_specs=pl.BlockSpec((tm, tn), lambda i, j: (i, j)),
        out_shape=jax.ShapeDtypeStruct((m, n), F32),
        scratch_shapes=[pltpu.VMEM((tm, k), BF16)],
        compiler_params=_params("parallel", "arbitrary"),
        name="norm_matmul",
    )(x, g.reshape(1, k), w, b.reshape(1, n))


def _matmul_post_body(n_pairs, *refs):
    a_refs = refs[:n_pairs]
    w_refs = refs[n_pairs:2 * n_pairs]
    b_ref, r_ref, g_ref, o_ref = refs[2 * n_pairs:]
    acc = _dot(a_refs[0][...], w_refs[0][...])
    for a_ref, w_ref in zip(a_refs[1:], w_refs[1:]):
        acc = acc + _dot(a_ref[...], w_ref[...])
    acc = acc + b_ref[...]
    ms = jnp.mean(acc * acc, axis=-1, keepdims=True)
    o_ref[...] = r_ref[...] + acc * lax.rsqrt(ms + EPS) * g_ref[...]


def matmul_post(a_list, w_list, b, resid, g, tm):
    m, n = resid.shape
    n_pairs = len(a_list)
    in_specs = [pl.BlockSpec((tm, a.shape[1]), lambda i: (i, 0)) for a in a_list]
    in_specs += [pl.BlockSpec(w.shape, lambda i: (0, 0)) for w in w_list]
    in_specs += [pl.BlockSpec((1, n), lambda i: (0, 0)),
                 pl.BlockSpec((tm, n), lambda i: (i, 0)),
                 pl.BlockSpec((1, n), lambda i: (0, 0))]
    return pl.pallas_call(
        functools.partial(_matmul_post_body, n_pairs),
        grid=(m // tm,),
        in_specs=in_specs,
        out_specs=pl.BlockSpec((tm, n), lambda i: (i, 0)),
        out_shape=jax.ShapeDtypeStruct((m, n), F32),
        compiler_params=_params("parallel"),
        name="matmul_post",
    )(*a_list, *w_list, b.reshape(1, n), resid, g.reshape(1, n))


def _rope_tables(pos):
    half = NSA_HEAD_DIM // 2
    inv = ROPE_THETA ** (-jnp.arange(half, dtype=F32) / half)
    ang = pos.astype(F32)[:, None] * inv[None, :]
    cos, sin = jnp.cos(ang), jnp.sin(ang)
    reps = LANES // NSA_HEAD_DIM
    cosf = jnp.tile(jnp.concatenate([cos, cos], axis=-1), (1, reps))
    sinf = jnp.tile(jnp.concatenate([-sin, sin], axis=-1), (1, reps))
    return cosf, sinf


def _rotate_half_partner(x):
    w = x.shape[-1]
    lane = lax.broadcasted_iota(jnp.int32, x.shape, x.ndim - 1)
    first = (lane % NSA_HEAD_DIM) < (NSA_HEAD_DIM // 2)
    return jnp.where(first, pltpu.roll(x, w - NSA_HEAD_DIM // 2, x.ndim - 1),
                     pltpu.roll(x, NSA_HEAD_DIM // 2, x.ndim - 1))


def _swap_head_pair(x):
    w = x.shape[-1]
    lane = lax.broadcasted_iota(jnp.int32, x.shape, x.ndim - 1)
    first = (lane % LANES) < NSA_HEAD_DIM
    return jnp.where(first, pltpu.roll(x, w - NSA_HEAD_DIM, x.ndim - 1),
                     pltpu.roll(x, NSA_HEAD_DIM, x.ndim - 1))


def _rope_body(q_ref, kv_ref, cos_ref, sin_ref, q_o, qs_o, kv_o, kvc_o, kvs_o, kvw_o):
    cos = cos_ref[...]
    sin = sin_ref[...]
    q = q_ref[...]
    nq = q.shape[1] // LANES
    cq = jnp.concatenate([cos] * nq, axis=1)
    sq = jnp.concatenate([sin] * nq, axis=1)
    qr = (q * cq + _rotate_half_partner(q) * sq) * QK_SCALE_LOG2
    q_o[...] = qr.astype(BF16)
    qs_o[...] = _swap_head_pair(qr).astype(BF16)

    kv = kv_ref[...]
    lane = lax.broadcasted_iota(jnp.int32, (kv.shape[0], LANES), 1)
    lo = lane < NSA_HEAD_DIM
    zero = jnp.zeros((kv.shape[0], LANES), F32)
    padded = []
    for c, full_o in enumerate((kvc_o, kvs_o, kvw_o)):
        k = kv[:, c * 256:c * 256 + LANES]
        v = kv[:, c * 256 + LANES:(c + 1) * 256]
        kr = k * cos + _rotate_half_partner(k) * sin
        full_o[...] = jnp.concatenate([kr, v], axis=1)
        if c > 0:
            kr_sw = pltpu.roll(kr, NSA_HEAD_DIM, 1)
            v_sw = pltpu.roll(v, NSA_HEAD_DIM, 1)
            padded += [jnp.where(lo, kr, zero), jnp.where(lo, kr_sw, zero),
                       jnp.where(lo, v, zero), jnp.where(lo, v_sw, zero)]
    kv_o[...] = jnp.concatenate(padded, axis=1).astype(BF16)


def rope_call(proj, cosf, sinf, tm):
    m = proj.shape[0]
    nblk = cosf.shape[0] // tm
    f32s = jax.ShapeDtypeStruct((m, 256), F32)
    return pl.pallas_call(
        _rope_body,
        grid=(m // tm,),
        in_specs=[pl.BlockSpec((tm, 1024), lambda i: (i, 2)),
                  pl.BlockSpec((tm, 768), lambda i: (i, 8)),
                  pl.BlockSpec((tm, LANES), lambda i: (i % nblk, 0)),
                  pl.BlockSpec((tm, LANES), lambda i: (i % nblk, 0))],
        out_specs=[pl.BlockSpec((tm, 1024), lambda i: (i, 0)),
                   pl.BlockSpec((tm, 1024), lambda i: (i, 0)),
                   pl.BlockSpec((tm, 1024), lambda i: (i, 0)),
                   pl.BlockSpec((tm, 256), lambda i: (i, 0)),
                   pl.BlockSpec((tm, 256), lambda i: (i, 0)),
                   pl.BlockSpec((tm, 256), lambda i: (i, 0))],
        out_shape=[jax.ShapeDtypeStruct((m, 1024), BF16), jax.ShapeDtypeStruct((m, 1024), BF16),
                   jax.ShapeDtypeStruct((m, 1024), BF16), f32s, f32s, f32s],
        compiler_params=_params("parallel"),
        name="rope",
    )(proj, proj, cosf, sinf)


def _compress_body(ch_ref, pe_ref, w1_ref, b1_ref, w2_ref, b2_ref, o_ref):
    ch = ch_ref[0, 0]
    nc, half = ch.shape
    pe = pe_ref[0]
    top = _dot((ch + pe[:, :half]).astype(BF16), w1_ref[0, :half, :])
    bot = _dot((ch + pe[:, half:]).astype(BF16), w1_ref[0, half:, :])
    hid = top + pltpu.roll(bot, nc - 1, 0) + b1_ref[0]
    hid = hid * jax.nn.sigmoid(hid)
    o_ref[0, 0] = _dot(hid.astype(BF16), w2_ref[0]) + b2_ref[0]


def compress_call(ch, pe_flat, w1, b1, w2p, b2p):
    bsz, _, nc, half = ch.shape
    hidden = w1.shape[-1]
    return pl.pallas_call(
        _compress_body,
        grid=(bsz, 4),
        in_specs=[pl.BlockSpec((1, 1, nc, half), lambda b, q: (b, q, 0, 0)),
                  pl.BlockSpec((1, 1, 2 * half), lambda b, q: (q // 2, 0, 0)),
                  pl.BlockSpec((1, 2 * half, hidden), lambda b, q: (q // 2, 0, 0)),
                  pl.BlockSpec((1, 1, hidden), lambda b, q: (q // 2, 0, 0)),
                  pl.BlockSpec((1, hidden, LANES), lambda b, q: (q // 2, 0, 0)),
                  pl.BlockSpec((1, 1, LANES), lambda b, q: (q // 2, 0, 0))],
        out_specs=pl.BlockSpec((1, 1, nc, LANES), lambda b, q: (b, q, 0, 0)),
        out_shape=jax.ShapeDtypeStruct((bsz, 4, nc, LANES), F32),
        compiler_params=_params("parallel", "parallel"),
        name="compress",
    )(ch, pe_flat, w1, b1, w2p, b2p)


def _overlap_np(nc, ns):
    i = np.arange(nc)[:, None]
    j = np.arange(ns)[None, :]
    return ((i * CMP_STRIDE < (j + 1) * SLC_LEN) & (i * CMP_STRIDE + CMP_LEN > j * SLC_LEN)).astype(np.float32)


def _nsa_t_body(seq, n_cmp, ck, wk, n_dt, q_ref, qs_ref, kvs_ref, kvw_ref, ckv_ref, small_ref, ovt_ref, o_ref):
    i = pl.program_id(1)
    tq = Q_BLOCK
    dh = NSA_HEAD_DIM
    n_slc = seq // SLC_LEN
    nsp = -(-n_slc // LANES) * LANES
    nc = ckv_ref.shape[2]
    npair = NSA_HPG // 2
    rows = NSA_HPG * tq
    neg = -jnp.inf
    tpos_t = i * tq + lax.broadcasted_iota(jnp.int32, (1, tq), 1)
    q = q_ref[...]
    qs = qs_ref[...]
    gates_t = jax.nn.sigmoid(small_ref[...]).T

    def mask_heads(s, mask):
        return jnp.concatenate([jnp.where(mask, s[:, h * tq:(h + 1) * tq], neg) for h in range(NSA_HPG)], axis=1)

    def values_t(v):
        return v.astype(F32).T[0:dh].astype(BF16)

    def gated_pair_tiles(o_t, g, branch):
        tiles = []
        for p in range(npair):
            c_even = n_dt + 3 * (g * NSA_HPG + 2 * p) + branch
            even = o_t[:, p * tq:(p + 1) * tq] * gates_t[c_even:c_even + 1, :]
            odd = o_t[:, (npair + p) * tq:(npair + p + 1) * tq] * gates_t[c_even + 3:c_even + 4, :]
            tiles.append(jnp.concatenate([even, odd], axis=0))
        return tiles

    out_tiles, qgs, sels = [], [], []
    for g in range(NSA_KV):
        qg = jnp.concatenate([q[:, (g * npair + p) * LANES:(g * npair + p + 1) * LANES] for p in range(npair)]
                             + [qs[:, (g * npair + p) * LANES:(g * npair + p + 1) * LANES] for p in range(npair)],
                             axis=0)
        s_c = _dot_nt(ckv_ref[0, g].astype(BF16), qg)
        cidx = lax.broadcasted_iota(jnp.int32, (nc, 1), 0)
        s_c = mask_heads(s_c, (cidx * CMP_STRIDE + (CMP_LEN - 1) <= tpos_t) & (cidx < n_cmp))
        mx = jnp.max(s_c, axis=0, keepdims=True)
        e = jnp.exp2(s_c - jnp.where(mx == neg, 0.0, mx))
        d = jnp.sum(e, axis=0, keepdims=True)
        p_c = e / jnp.where(d > 0, d, 1.0)
        vc_t = ckv_ref[0, NSA_KV + g].T[0:dh].astype(BF16)
        tiles = gated_pair_tiles(_dot(vc_t, p_c.astype(BF16)), g, 0)
        p_sum = p_c[:, 0:tq]
        for h in range(1, NSA_HPG):
            p_sum = p_sum + p_c[:, h * tq:(h + 1) * tq]
        ovt = ovt_ref[...]
        imp_t = sum(_dot(ovt, part) for part in _split3(p_sum))
        jidx = lax.broadcasted_iota(jnp.int32, (n_slc, 1), 0)
        qblk = tpos_t // SLC_LEN
        valid = jidx * SLC_LEN <= tpos_t
        forced = (jidx == 0) | (jidx == qblk) | (jidx == qblk - 1)
        score = jnp.where(valid, imp_t + FORCE_SCORE * forced.astype(F32), neg)
        rank = jnp.zeros((n_slc, tq), F32)
        for k in range(n_slc):
            row = score[k:k + 1, :]
            ahead = (row > score) | ((row == score) & (jidx > k))
            rank = rank + ahead.astype(F32)
        sel_t = (rank < float(min(SLC_TOPN, n_slc))).astype(BF16)
        if nsp > n_slc:
            sel_t = jnp.concatenate([sel_t, jnp.zeros((nsp - n_slc, tq), BF16)], axis=0)
        qgs.append(qg)
        sels.append(sel_t)

        w_start = jnp.clip((i - WINDOW // tq) * tq, 0, seq - wk)
        w_start = pl.multiple_of(w_start, tq)
        k_w = kvw_ref[pl.ds(w_start, wk), g * LANES:(g + 1) * LANES]
        v_w = kvw_ref[pl.ds(w_start, wk), (NSA_KV + g) * LANES:(NSA_KV + g + 1) * LANES]
        dpos = tpos_t - (w_start + lax.broadcasted_iota(jnp.int32, (wk, 1), 0))
        s_w = mask_heads(_dot_nt(k_w, qg), (dpos >= 0) & (dpos < WINDOW))
        e_w = jnp.exp2(s_w - jnp.max(s_w, axis=0, keepdims=True))
        o_w = _dot(values_t(v_w), e_w.astype(BF16)) / jnp.sum(e_w, axis=0, keepdims=True)
        out_tiles.append([t + u for t, u in zip(tiles, gated_pair_tiles(o_w, g, 2))])

    def slc_step(c, carry):
        start = pl.multiple_of(c * ck, ck)
        blk = c * (ck // SLC_LEN) + lax.broadcasted_iota(jnp.int32, (ck, nsp), 0) // SLC_LEN
        expand_t = (lax.broadcasted_iota(jnp.int32, (ck, nsp), 1) == blk).astype(BF16)
        causal = start + lax.broadcasted_iota(jnp.int32, (ck, 1), 0) <= tpos_t
        new = []
        for g in range(NSA_KV):
            m_i, l_i, acc = carry[g]
            k_c = kvs_ref[pl.ds(start, ck), g * LANES:(g + 1) * LANES]
            v_c = kvs_ref[pl.ds(start, ck), (NSA_KV + g) * LANES:(NSA_KV + g + 1) * LANES]
            mb = (_dot(expand_t, sels[g]) > 0.5) & causal
            s = mask_heads(_dot_nt(k_c, qgs[g]), mb)
            m_new = jnp.maximum(m_i, jnp.max(s, axis=0, keepdims=True))
            alpha = jnp.exp2(m_i - m_new)
            p = jnp.exp2(s - m_new)
            l_new = alpha * l_i + jnp.sum(p, axis=0, keepdims=True)
            new.append((m_new, l_new, alpha * acc + _dot(values_t(v_c), p.astype(BF16))))
        return tuple(new)

    n_chunks = (i * tq + tq - 1) // ck + 1
    init = (jnp.full((1, rows), neg, F32), jnp.zeros((1, rows), F32), jnp.zeros((dh, rows), F32))
    final = lax.fori_loop(0, n_chunks, slc_step, (init,) * NSA_KV)
    tiles_all = []
    for g in range(NSA_KV):
        _, l_s, acc_s = final[g]
        tiles_all += [(t + u).T for t, u in zip(out_tiles[g], gated_pair_tiles(acc_s / l_s, g, 1))]
    o_ref[...] = jnp.concatenate(tiles_all, axis=1).astype(BF16)


def nsa_prompt_call(q_rot, q_swp, kv_pad, ckv, proj, bsz, seq, n_cmp, n_dt):
    nq = seq // Q_BLOCK
    nc = ckv.shape[2]
    n_slc = seq // SLC_LEN
    ck = min(512, seq)
    wk = min(WINDOW + Q_BLOCK, seq)
    ovt = jnp.asarray(np.pad(_overlap_np(n_cmp, n_slc), ((0, nc - n_cmp), (0, 0))).T, BF16)
    small_blk = proj.shape[1] // LANES - 1
    return pl.pallas_call(
        functools.partial(_nsa_t_body, seq, n_cmp, ck, wk, n_dt),
        grid=(bsz, nq),
        in_specs=[pl.BlockSpec((Q_BLOCK, 1024), lambda b, i: (b * nq + i, 0)),
                  pl.BlockSpec((Q_BLOCK, 1024), lambda b, i: (b * nq + i, 0)),
                  pl.BlockSpec((seq, 512), lambda b, i: (b, 0)),
                  pl.BlockSpec((seq, 512), lambda b, i: (b, 1)),
                  pl.BlockSpec((1, 4, nc, LANES), lambda b, i: (b, 0, 0, 0)),
                  pl.BlockSpec((Q_BLOCK, LANES), lambda b, i: (b * nq + i, small_blk)),
                  pl.BlockSpec((n_slc, nc), lambda b, i: (0, 0))],
        out_specs=pl.BlockSpec((Q_BLOCK, 1024), lambda b, i: (b * nq + i, 0)),
        out_shape=jax.ShapeDtypeStruct((bsz * seq, 1024), BF16),
        compiler_params=_params("parallel", "arbitrary"),
        name="nsa_prompt",
    )(q_rot, q_swp, kv_pad, kv_pad, ckv, proj, ovt)


def _page_copy(cache_hbm, pt_ref, buf, sem, b, p, half, slot):
    src = cache_hbm.at[pt_ref[b, p], :, pl.ds(half * LANES, LANES)]
    return pltpu.make_async_copy(src, buf.at[slot, p, half], sem.at[slot])


def _sample_nsa_body(n_pages, n_cmp, qpos, w_rows,
                     pt_ref, ccmp_hbm, cslc_hbm, q16_ref, qpad_ref, new_ref, win_ref, gates_ref,
                     pet_ref, w1t_ref, b1_ref, w2t_ref, b2_ref, ov_ref,
                     y_ref, buf, sem, sel_scr, oc_scr):
    s = pl.program_id(0)
    nb = pl.num_programs(0) // 2
    b = s // 2
    page = buf.shape[3]
    n_rows = n_pages * page
    nch = n_rows // CMP_STRIDE
    nsp = ov_ref.shape[1]
    heads = NSA_HEADS
    row_g0 = lax.broadcasted_iota(jnp.int32, (heads, 1), 0) < NSA_HPG

    def fetch(cache_hbm, bb, slot):
        for p in range(n_pages):
            for half in range(2):
                _page_copy(cache_hbm, pt_ref, buf, sem, bb, p, half, slot).start()

    def wait(cache_hbm, bb, slot):
        for p in range(n_pages):
            for half in range(2):
                _page_copy(cache_hbm, pt_ref, buf, sem, bb, p, half, slot).wait()

    @pl.when(s == 0)
    def _():
        fetch(ccmp_hbm, 0, 0)

    @pl.when(s % 2 == 0)
    def _():
        fetch(cslc_hbm, b, 1)
        wait(ccmp_hbm, b, 0)
        tops = [[] for _ in range(4)]
        bots = [[] for _ in range(4)]
        for r in range(CMP_STRIDE):
            for kind in range(2):
                x_r = buf[pl.ds(0, 1), :, pl.ds(kind, 1), pl.ds(r, page // CMP_STRIDE, stride=CMP_STRIDE), :]
                xt = x_r.reshape(nch, LANES).T
                pe_rows = pl.ds(kind * LANES, LANES)
                top = (xt + pet_ref[pe_rows, r:r + 1]).astype(BF16)
                bot = (xt + pet_ref[pe_rows, CMP_STRIDE + r:CMP_STRIDE + r + 1]).astype(BF16)
                for g in range(NSA_KV):
                    tops[kind * NSA_KV + g].append(top[g * NSA_HEAD_DIM:(g + 1) * NSA_HEAD_DIM])
                    bots[kind * NSA_KV + g].append(bot[g * NSA_HEAD_DIM:(g + 1) * NSA_HEAD_DIM])
        half_w = CMP_STRIDE * NSA_HEAD_DIM
        ckv_t = []
        for qq in range(4):
            kind = qq // NSA_KV
            w1t = w1t_ref[kind]
            hid = (_dot(w1t[:, :half_w], jnp.concatenate(tops[qq], axis=0))
                   + pltpu.roll(_dot(w1t[:, half_w:], jnp.concatenate(bots[qq], axis=0)), nch - 1, 1)
                   + b1_ref[kind])
            hid = hid * jax.nn.sigmoid(hid)
            ckv_t.append((_dot(w2t_ref[kind], hid.astype(BF16)) + b2_ref[kind]).astype(BF16))
        q16 = q16_ref[0]
        cidx = lax.broadcasted_iota(jnp.int32, (1, nch), 1)
        m_c = (cidx * CMP_STRIDE + (CMP_LEN - 1) <= qpos) & (cidx < n_cmp)
        oc = []
        for g in range(NSA_KV):
            s_c = jnp.where(m_c, _dot(q16, ckv_t[g]), -jnp.inf)
            mx = jnp.max(s_c, axis=-1, keepdims=True)
            mx = jnp.where(mx == -jnp.inf, 0.0, mx)
            e = jnp.exp2(s_c - mx)
            d = jnp.sum(e, axis=-1, keepdims=True)
            p_c = e / jnp.where(d > 0, d, 1.0)
            oc.append(_dot_nt(p_c.astype(BF16), ckv_t[NSA_KV + g]))
            in_group = row_g0 if g == 0 else jnp.logical_not(row_g0)
            p_sum = jnp.sum(jnp.where(in_group, p_c, 0.0), axis=0, keepdims=True)
            imp = _dot3(jnp.broadcast_to(p_sum, (8, nch)), ov_ref[...])[0:1]
            jidx = lax.broadcasted_iota(jnp.int32, (1, nsp), 1)
            qblk = qpos // SLC_LEN
            forced = (jidx == 0) | (jidx == qblk) | (jidx == qblk - 1)
            score = jnp.where(jidx * SLC_LEN <= qpos, imp + FORCE_SCORE * forced.astype(F32), -jnp.inf)
            s_j = jnp.broadcast_to(score, (nsp, nsp))
            s_k = s_j.T
            kk = lax.broadcasted_iota(jnp.int32, (nsp, nsp), 0)
            jj = lax.broadcasted_iota(jnp.int32, (nsp, nsp), 1)
            ahead = (s_k > s_j) | ((s_k == s_j) & (kk < jj))
            rank = jnp.sum(ahead.astype(F32), axis=0, keepdims=True)
            sel_scr[g:g + 1, :] = (rank < float(SLC_TOPN)).astype(F32)
        oc_scr[...] = jnp.where(row_g0, oc[0], oc[1])

    @pl.when(s % 2 == 1)
    def _():
        @pl.when(b + 1 < nb)
        def _():
            fetch(ccmp_hbm, b + 1, 0)

        wait(cslc_hbm, b, 1)
        qpad = qpad_ref[0]
        sel16 = jnp.where(row_g0, jnp.broadcast_to(sel_scr[0:1, :], (heads, nsp)),
                          jnp.broadcast_to(sel_scr[1:2, :], (heads, nsp)))
        sel16_bf = sel16.astype(BF16)
        ppc = 8
        ck = ppc * page
        n_ck = n_rows // ck
        vals, scores = [], []
        for c in range(n_ck):
            k_c = buf[1, c * ppc:(c + 1) * ppc, 0].reshape(ck, LANES).astype(BF16)
            vals.append(buf[1, c * ppc:(c + 1) * ppc, 1].reshape(ck, LANES).astype(BF16))
            blk = c * (ck // SLC_LEN) + lax.broadcasted_iota(jnp.int32, (nsp, ck), 1) // SLC_LEN
            expand = (lax.broadcasted_iota(jnp.int32, (nsp, ck), 0) == blk).astype(BF16)
            kpos = c * ck + lax.broadcasted_iota(jnp.int32, (1, ck), 1)
            mask = (_dot(sel16_bf, expand) > 0.5) & (kpos <= qpos)
            scores.append(jnp.where(mask, _dot_nt(qpad, k_c), -jnp.inf))
        new = new_ref[0]
        ks_new = new[:, 0:LANES].astype(BF16)
        vs_new = new[:, LANES:2 * LANES].astype(BF16)
        kw_new = new[:, 2 * LANES:3 * LANES].astype(BF16)
        vw_new = new[:, 3 * LANES:].astype(BF16)
        first = lax.broadcasted_iota(jnp.int32, (1, new.shape[0]), 1) == 0
        nblk = n_rows // SLC_LEN
        s_new = jnp.where(first & (sel16[:, nblk:nblk + 1] > 0.5), _dot_nt(qpad, ks_new), -jnp.inf)
        mx = jnp.max(s_new, axis=-1, keepdims=True)
        for sc in scores:
            mx = jnp.maximum(mx, jnp.max(sc, axis=-1, keepdims=True))
        e_new = jnp.exp2(s_new - mx)
        den = jnp.sum(e_new, axis=-1, keepdims=True)
        acc = _dot(e_new.astype(BF16), vs_new)
        for sc, v_c in zip(scores, vals):
            e = jnp.exp2(sc - mx)
            den = den + jnp.sum(e, axis=-1, keepdims=True)
            acc = acc + _dot(e.astype(BF16), v_c)
        o_s = acc / den
        k_w = win_ref[0, :, 0:LANES].astype(BF16)
        v_w = win_ref[0, :, LANES:2 * LANES].astype(BF16)
        dpos = w_rows - lax.broadcasted_iota(jnp.int32, (1, w_rows), 1)
        m_w = (dpos >= 0) & (dpos < WINDOW) & (qpos - dpos >= 0)
        s_w = jnp.where(m_w, _dot_nt(qpad, k_w), -jnp.inf)
        s_wn = jnp.where(first, _dot_nt(qpad, kw_new), -jnp.inf)
        mxw = jnp.maximum(jnp.max(s_w, axis=-1, keepdims=True), jnp.max(s_wn, axis=-1, keepdims=True))
        e_w = jnp.exp2(s_w - mxw)
        e_wn = jnp.exp2(s_wn - mxw)
        o_w = ((_dot(e_w.astype(BF16), v_w) + _dot(e_wn.astype(BF16), vw_new))
               / (jnp.sum(e_w, axis=-1, keepdims=True) + jnp.sum(e_wn, axis=-1, keepdims=True)))

        def own_group(o):
            return jnp.where(row_g0, o, pltpu.roll(o, NSA_HEAD_DIM, 1))[:, :NSA_HEAD_DIM]

        gt = jax.nn.sigmoid(gates_ref[0])
        y_ref[0] = gt[:, 0:1] * oc_scr[...] + gt[:, 1:2] * own_group(o_s) + gt[:, 2:3] * own_group(o_w)


def sample_nsa_call(page_table, cache_cmp, cache_slc, cache_win, q_rot, kvs_new, kvw_new, gates, pe, w1, b1, w2, b2):
    db, n_pages = page_table.shape
    n_phys, page = cache_cmp.shape[:2]
    past_len = n_pages * page
    nch = past_len // CMP_STRIDE
    n_cmp = nch - CMP_LEN // CMP_STRIDE + 1
    n_slc = -(-(past_len + 1) // SLC_LEN)
    nsp = -(-n_slc // LANES) * LANES
    w_rows = cache_win.shape[1]
    lanes = 4 * NSA_HEAD_DIM
    q16 = q_rot.reshape(db, NSA_HEADS, NSA_HEAD_DIM)
    q4 = q_rot.reshape(db, NSA_KV, NSA_HPG, NSA_HEAD_DIM)
    qpad = jnp.concatenate([jnp.pad(q4[:, g], ((0, 0), (0, 0), (g * NSA_HEAD_DIM, LANES - (g + 1) * NSA_HEAD_DIM)))
                            for g in range(NSA_KV)], axis=1)
    new = jnp.pad(jnp.concatenate([kvs_new, kvw_new], axis=1)[:, None, :], ((0, 0), (0, 15), (0, 0)))
    hidden = w1.shape[-1]
    pet = jnp.repeat(pe, NSA_KV, axis=0).transpose(0, 2, 1).reshape(lanes, CMP_LEN)
    ov = jnp.asarray(np.pad(_overlap_np(n_cmp, n_slc), ((0, nch - n_cmp), (0, nsp - n_slc))), BF16)
    full = lambda shape: pl.BlockSpec(shape, lambda s, pt: (0,) * len(shape))
    per_b = lambda shape: pl.BlockSpec((1,) + shape, lambda s, pt: (s // 2,) + (0,) * len(shape))
    grid_spec = pltpu.PrefetchScalarGridSpec(
        num_scalar_prefetch=1,
        grid=(2 * db,),
        in_specs=[pl.BlockSpec(memory_space=pl.ANY), pl.BlockSpec(memory_space=pl.ANY),
                  per_b((NSA_HEADS, NSA_HEAD_DIM)), per_b((NSA_HEADS, LANES)), per_b((16, 2 * lanes)),
                  per_b((w_rows, lanes)), per_b((NSA_HEADS, 3)),
                  full((lanes, CMP_LEN)), full((2, hidden, CMP_LEN * NSA_HEAD_DIM)), full((2, hidden, 1)),
                  full((2, NSA_HEAD_DIM, hidden)), full((2, NSA_HEAD_DIM, 1)), full((nch, nsp))],
        out_specs=per_b((NSA_HEADS, NSA_HEAD_DIM)),
        scratch_shapes=[pltpu.VMEM((2, n_pages, 2, page, LANES), F32),
                        pltpu.SemaphoreType.DMA((2,)),
                        pltpu.VMEM((8, nsp), F32),
                        pltpu.VMEM((NSA_HEADS, NSA_HEAD_DIM), F32)])
    y = pl.pallas_call(
        functools.partial(_sample_nsa_body, n_pages, n_cmp, past_len, w_rows),
        grid_spec=grid_spec,
        out_shape=jax.ShapeDtypeStruct((db, NSA_HEADS, NSA_HEAD_DIM), F32),
        compiler_params=_params("arbitrary"),
        name="sample_nsa",
    )(page_table, cache_cmp.reshape(n_phys, page, lanes), cache_slc.reshape(n_phys, page, lanes),
      q16, qpad, new, cache_win.reshape(db, w_rows, lanes), gates.reshape(db, NSA_HEADS, 3),
      pet, w1.transpose(0, 2, 1).astype(BF16), b1.reshape(2, hidden, 1),
      w2.transpose(0, 2, 1).astype(BF16), b2.reshape(2, NSA_HEAD_DIM, 1), ov)
    return y.reshape(db, NSA_HEADS * NSA_HEAD_DIM)


def _softplus(x):
    return jnp.maximum(x, 0.0) + jnp.log1p(jnp.exp(-jnp.abs(x)))


def _ssd_body(z_ref, xbc_ref, small_ref, cw_ref, cb_ref, dtb_ref, dtbt_ref, alog_ref, alogt_ref,
              dskip_ref, nrm_ref, rexp_ref, ltri_ref, utri_ref, y_ref, st_ref, xbuf, ht):
    c = pl.program_id(1)
    q = SSM_CHUNK
    d_inner = z_ref.shape[1]
    n_heads = d_inner // SSM_HEAD_DIM
    gw = d_inner // SSM_GROUPS
    n = SSM_STATE

    @pl.when(c == 0)
    def _():
        xbuf[0:8, :] = jnp.zeros((8, xbuf.shape[1]), F32)
        ht[...] = jnp.zeros(ht.shape, F32)

    x = xbc_ref[...]
    xbuf[8:8 + q, :] = x
    cw = cw_ref[...]
    conv = (x * cw[3:4] + xbuf[7:7 + q, :] * cw[2:3] + xbuf[6:6 + q, :] * cw[1:2]
            + xbuf[5:5 + q, :] * cw[0:1] + cb_ref[...])
    xbuf[0:8, :] = x[q - 8:q]
    xc = conv * jax.nn.sigmoid(conv)
    xs = xc[:, :d_inner]
    bm = xc[:, d_inner:d_inner + SSM_GROUPS * n]
    cm = xc[:, d_inner + SSM_GROUPS * n:]

    small = small_ref[...]
    dt = _softplus(small[:, 0:n_heads] + dtb_ref[...])
    dtt = _softplus(small.T[0:n_heads, :] + dtbt_ref[...])
    a = -jnp.exp(alog_ref[...])
    at = -jnp.exp(alogt_ref[...])
    cum = _dot3_left(ltri_ref[...], dt * a)
    cumt = _dot3(dtt * at, utri_ref[...])
    cum_last = cum[q - 1:q, :]
    rexp = rexp_ref[...]
    expcum_f = _dot3(jnp.exp(cum), rexp)
    toend_f = _dot3(jnp.exp(cum_last - cum) * dt, rexp)
    cdec_f = _dot3(jnp.broadcast_to(jnp.exp(cum_last), (8, n_heads)), rexp)[0:1]

    causal = (lax.broadcasted_iota(jnp.int32, (q, q), 0) >= lax.broadcasted_iota(jnp.int32, (q, q), 1))
    lane_lo = lax.broadcasted_iota(jnp.int32, (q, LANES), 1) < SSM_HEAD_DIM
    xs_bf = xs.astype(BF16)
    zero_bf = jnp.zeros((q, LANES), BF16)
    hpg = n_heads // SSM_GROUPS
    y_tiles, yoff_tiles = [], []
    for g in range(SSM_GROUPS):
        bg = bm[:, g * n:(g + 1) * n]
        cg_bf = cm[:, g * n:(g + 1) * n].astype(BF16)
        cb = _dot_nt(cg_bf, bg.astype(BF16))
        h_old = ht[g]
        yoff_tiles.append(_dot(cg_bf, h_old.astype(BF16)))
        for pr in range(hpg // 2):
            h0 = g * hpg + 2 * pr
            xp = xs_bf[:, h0 * SSM_HEAD_DIM:h0 * SSM_HEAD_DIM + LANES]
            acc = None
            for par, xm in ((0, jnp.where(lane_lo, xp, zero_bf)), (1, jnp.where(lane_lo, zero_bf, xp))):
                h = h0 + par
                seg = cum[:, h:h + 1] - cumt[h:h + 1, :]
                w = cb * jnp.exp(jnp.where(causal, seg, -jnp.inf)) * dtt[h:h + 1, :]
                part = _dot(w.astype(BF16), xm)
                acc = part if acc is None else acc + part
            y_tiles.append(acc)
        xw = (xs[:, g * gw:(g + 1) * gw] * toend_f[:, g * gw:(g + 1) * gw]).astype(BF16)
        ht[g] = h_old * cdec_f[:, g * gw:(g + 1) * gw] + _dot(bg.T.astype(BF16), xw)

    y = (jnp.concatenate(y_tiles, axis=1) + jnp.concatenate(yoff_tiles, axis=1) * expcum_f
         + dskip_ref[...] * xs)
    z = z_ref[...]
    y = y * (z * jax.nn.sigmoid(z))
    outs = []
    for g in range(SSM_GROUPS):
        yg = y[:, g * gw:(g + 1) * gw]
        ms = jnp.mean(yg * yg, axis=-1, keepdims=True)
        outs.append(yg * lax.rsqrt(ms + EPS) * nrm_ref[:, g * gw:(g + 1) * gw])
    y_ref[...] = jnp.concatenate(outs, axis=1).astype(BF16)
    st_ref[0] = ht[...]


def ssd_call(proj, conv_w, conv_b, dt_bias, a_log, d_skip, ssm_norm, bsz, seq):
    q = SSM_CHUNK
    nchunk = seq // q
    n_heads = dt_bias.shape[0]
    d_inner = n_heads * SSM_HEAD_DIM
    conv_dim = conv_w.shape[1]
    gw = d_inner // SSM_GROUPS
    small_blk = proj.shape[1] // LANES - 1
    rexp = jnp.asarray(np.repeat(np.eye(n_heads, dtype=np.float32), SSM_HEAD_DIM, axis=1), BF16)
    ltri = jnp.asarray(np.tril(np.ones((q, q), np.float32)), BF16)
    utri = jnp.asarray(np.triu(np.ones((q, q), np.float32)), BF16)
    const = lambda shape: pl.BlockSpec(shape, lambda b, c: (0,) * len(shape))
    return pl.pallas_call(
        _ssd_body,
        grid=(bsz, nchunk),
        in_specs=[pl.BlockSpec((q, d_inner), lambda b, c: (b * nchunk + c, 0)),
                  pl.BlockSpec((q, conv_dim), lambda b, c: (b * nchunk + c, 1)),
                  pl.BlockSpec((q, LANES), lambda b, c: (b * nchunk + c, small_blk)),
                  const((SSM_CONV, conv_dim)), const((1, conv_dim)),
                  const((1, n_heads)), const((n_heads, 1)), const((1, n_heads)), const((n_heads, 1)),
                  const((1, d_inner)), const((1, d_inner)),
                  const((n_heads, d_inner)), const((q, q)), const((q, q))],
        out_specs=[pl.BlockSpec((q, d_inner), lambda b, c: (b * nchunk + c, 0)),
                   pl.BlockSpec((1, SSM_GROUPS, SSM_STATE, gw), lambda b, c: (b, 0, 0, 0))],
        out_shape=[jax.ShapeDtypeStruct((bsz * seq, d_inner), BF16),
                   jax.ShapeDtypeStruct((bsz, SSM_GROUPS, SSM_STATE, gw), F32)],
        scratch_shapes=[pltpu.VMEM((8 + q, conv_dim), F32),
                        pltpu.VMEM((SSM_GROUPS, SSM_STATE, gw), F32)],
        compiler_params=_params("parallel", "arbitrary"),
        name="ssd",
    )(proj, proj, proj, conv_w, conv_b.reshape(1, -1),
      dt_bias.reshape(1, -1), dt_bias.reshape(-1, 1), a_log.reshape(1, -1), a_log.reshape(-1, 1),
      jnp.repeat(d_skip, SSM_HEAD_DIM).reshape(1, -1), ssm_norm.reshape(1, -1), rexp, ltri, utri)


CONF_HALO = 32


def _conf_body(u_ref, w_ref, b_ref, g_ref, beta_ref, o_ref, tail_ref, buf, shifted):
    t = pl.program_id(1)
    tm, d = o_ref.shape
    sub = 8

    @pl.when(t == 0)
    def _():
        buf[0:CONF_HALO, :] = jnp.zeros((CONF_HALO, d), F32)

    u = u_ref[...]
    a = u[:, :d] * jax.nn.sigmoid(u[:, d:])
    buf[CONF_HALO:CONF_HALO + tm, :] = a
    span = shifted.shape[1]
    for s in range(1, sub):
        shifted[s - 1] = buf[s:s + span, :]
    w = w_ref[...]
    off = CONF_HALO - (CONF_CONV - 1)
    acc = b_ref[...]
    for k in range(CONF_CONV):
        s, base = (off + k) % sub, (off + k) // sub * sub
        rows = buf[base:base + tm, :] if s == 0 else shifted[s - 1, base:base + tm, :]
        acc = acc + rows * w[k:k + 1]
    buf[0:CONF_HALO, :] = a[tm - CONF_HALO:tm]
    tail_ref[0] = a[tm - CONF_HALO:tm]
    mu = jnp.mean(acc, axis=-1, keepdims=True)
    cen = acc - mu
    var = jnp.mean(cen * cen, axis=-1, keepdims=True)
    y = cen * lax.rsqrt(var + EPS) * g_ref[...] + beta_ref[...]
    o_ref[...] = (y * jax.nn.sigmoid(y)).astype(BF16)


def conformer_mid_call(u, dw_w, dw_b, ln_g, ln_b, bsz, seq, tm):
    d = dw_w.shape[1]
    nt = seq // tm
    const = lambda shape: pl.BlockSpec(shape, lambda b, t: (0,) * len(shape))
    return pl.pallas_call(
        _conf_body,
        grid=(bsz, nt),
        in_specs=[pl.BlockSpec((tm, 2 * d), lambda b, t: (b * nt + t, 0)),
                  const((CONF_CONV, d)), const((1, d)), const((1, d)), const((1, d))],
        out_specs=[pl.BlockSpec((tm, d), lambda b, t: (b * nt + t, 0)),
                   pl.BlockSpec((1, CONF_HALO, d), lambda b, t: (b, 0, 0))],
        out_shape=[jax.ShapeDtypeStruct((bsz * seq, d), BF16),
                   jax.ShapeDtypeStruct((bsz, CONF_HALO, d), F32)],
        scratch_shapes=[pltpu.VMEM((CONF_HALO + tm, d), F32),
                        pltpu.VMEM((7, CONF_HALO - 8 + tm, d), F32)],
        compiler_params=_params("parallel", "arbitrary"),
        name="conformer_mid",
    )(u, dw_w, dw_b.reshape(1, d), ln_g.reshape(1, d), ln_b.reshape(1, d))


FFN_HALO = 8


def _ffn_up_body(x_ref, g_ref, wg_ref, wv_ref, cwg_ref, cwv_ref, cbg_ref, cbv_ref,
                 o_ref, tg_ref, tv_ref, xn_ref, ubuf, carry):
    t = pl.program_id(1)
    j = pl.program_id(2)
    tm = x_ref.shape[0]

    @pl.when(j == 0)
    def _():
        x = x_ref[...]
        ms = jnp.mean(x * x, axis=-1, keepdims=True)
        xn_ref[...] = (x * lax.rsqrt(ms + EPS) * g_ref[...]).astype(BF16)

    @pl.when(t == 0)
    def _():
        carry[j] = jnp.zeros(carry.shape[1:], F32)

    xn = xn_ref[...]
    halves = []
    for part, (w_ref, cw_ref, cb_ref, tail_ref) in enumerate(((wg_ref, cwg_ref, cbg_ref, tg_ref),
                                                              (wv_ref, cwv_ref, cbv_ref, tv_ref))):
        u = _dot(xn, w_ref[...])
        ubuf[part, 0:FFN_HALO, :] = carry[j, part]
        ubuf[part, FFN_HALO:FFN_HALO + tm, :] = u
        cw = cw_ref[...]
        halves.append(u * cw[2:3] + ubuf[part, FFN_HALO - 1:FFN_HALO - 1 + tm, :] * cw[1:2]
                      + ubuf[part, FFN_HALO - 2:FFN_HALO - 2 + tm, :] * cw[0:1] + cb_ref[...])
        carry[j, part] = u[tm - FFN_HALO:tm]
        tail_ref[0, j] = u[tm - FFN_HALO:tm]
    o_ref[...] = (jax.nn.gelu(halves[0], approximate=True) * halves[1]).astype(BF16)


def ffn_up_call(x, g, w_up, conv_w, conv_b, bsz, seq, tm, tn):
    m, k = x.shape
    d2 = w_up.shape[1]
    dff = d2 // 2
    nt = seq // tm
    nj = dff // tn
    cb = conv_b.reshape(1, d2)
    row = lambda b, t, j: b * nt + t
    act, tail_g, tail_v = pl.pallas_call(
        _ffn_up_body,
        grid=(bsz, nt, nj),
        in_specs=[pl.BlockSpec((tm, k), lambda b, t, j: (row(b, t, j), 0)),
                  pl.BlockSpec((1, k), lambda b, t, j: (0, 0)),
                  pl.BlockSpec((k, tn), lambda b, t, j: (0, j)),
                  pl.BlockSpec((k, tn), lambda b, t, j: (0, j + nj)),
                  pl.BlockSpec((FFN_CONV, tn), lambda b, t, j: (0, j)),
                  pl.BlockSpec((FFN_CONV, tn), lambda b, t, j: (0, j + nj)),
                  pl.BlockSpec((1, tn), lambda b, t, j: (0, j)),
                  pl.BlockSpec((1, tn), lambda b, t, j: (0, j + nj))],
        out_specs=[pl.BlockSpec((tm, tn), lambda b, t, j: (row(b, t, j), j)),
                   pl.BlockSpec((1, nj, FFN_HALO, tn), lambda b, t, j: (b, 0, 0, 0)),
                   pl.BlockSpec((1, nj, FFN_HALO, tn), lambda b, t, j: (b, 0, 0, 0))],
        out_shape=[jax.ShapeDtypeStruct((m, dff), BF16),
                   jax.ShapeDtypeStruct((bsz, nj, FFN_HALO, tn), F32),
                   jax.ShapeDtypeStruct((bsz, nj, FFN_HALO, tn), F32)],
        scratch_shapes=[pltpu.VMEM((tm, k), BF16),
                        pltpu.VMEM((2, FFN_HALO + tm, tn), F32),
                        pltpu.VMEM((nj, 2, FFN_HALO, tn), F32)],
        compiler_params=_params("parallel", "arbitrary", "arbitrary"),
        name="ffn_up",
    )(x, g.reshape(1, k), w_up, w_up, conv_w, conv_w, cb, cb)
    flat = lambda tail: tail.transpose(0, 2, 1, 3).reshape(bsz, FFN_HALO, dff)
    return act, jnp.concatenate([flat(tail_g), flat(tail_v)], axis=-1)


def _rms(x, g):
    return x * lax.rsqrt(jnp.mean(x * x, axis=-1, keepdims=True) + EPS) * g


def _sample_ssd_step(z, xbc, dt_raw, conv_hist, h0, conv_w, conv_b, dt_bias, a_log, d_skip, ssm_norm):
    bsz = z.shape[0]
    n_heads = dt_bias.shape[0]
    d_inner = n_heads * SSM_HEAD_DIM
    hpg = n_heads // SSM_GROUPS
    xh = jnp.concatenate([conv_hist, xbc[:, None, :]], axis=1)
    new_hist = xh[:, 1:]
    conv = jnp.sum(xh * conv_w[None], axis=1) + conv_b
    xc = conv * jax.nn.sigmoid(conv)
    xs = xc[:, :d_inner].reshape(bsz, SSM_GROUPS, hpg, SSM_HEAD_DIM)
    bm = xc[:, d_inner:d_inner + SSM_GROUPS * SSM_STATE].reshape(bsz, SSM_GROUPS, SSM_STATE)
    cm = xc[:, d_inner + SSM_GROUPS * SSM_STATE:].reshape(bsz, SSM_GROUPS, SSM_STATE)
    dt = jax.nn.softplus(dt_raw + dt_bias).reshape(bsz, SSM_GROUPS, hpg)
    a = (-jnp.exp(a_log)).reshape(SSM_GROUPS, hpg)
    dec = jnp.exp(dt * a)
    h0g = h0.reshape(bsz, SSM_GROUPS, hpg, SSM_HEAD_DIM, SSM_STATE)
    h_new = dec[..., None, None] * h0g + (dt[..., None] * xs)[..., None] * bm[:, :, None, None, :]
    y = jnp.sum(h_new * cm[:, :, None, None, :], axis=-1)
    y = y + d_skip.reshape(SSM_GROUPS, hpg)[None, :, :, None] * xs
    y = y.reshape(bsz, d_inner) * (z * jax.nn.sigmoid(z))
    y = _rms(y.reshape(bsz, SSM_GROUPS, d_inner // SSM_GROUPS),
             ssm_norm.reshape(SSM_GROUPS, d_inner // SSM_GROUPS)).reshape(bsz, d_inner)
    return y, new_hist, h_new.reshape(h0.shape)


def _prep_w_in(w_in, d_inner, conv_dim, n_heads):
    nq = NSA_HEADS * NSA_HEAD_DIM
    nkv = 2 * NSA_KV * NSA_HEAD_DIM
    o = np.cumsum([0, d_inner, conv_dim, n_heads, nq, nkv, nkv, nkv, 3 * NSA_HEADS])
    z, xbc, dt, q, kvc, kvs, kvw, gates = (w_in[:, o[k]:o[k + 1]] for k in range(8))
    small = jnp.concatenate([dt, gates], axis=1)
    small = jnp.pad(small, ((0, 0), (0, LANES - small.shape[1])))
    return jnp.concatenate([z, q, xbc, kvc, kvs, kvw, small], axis=1).astype(BF16)


def _pick_tile(m, pref):
    return pref if m % pref == 0 else m


def _col_tile(n, target):
    best = None
    for t in range(LANES, min(n, target) + 1, LANES):
        if n % t == 0:
            best = t
    return best or n


def kernel(x_prompt, x_sample, state_ssm, state_ssm_conv, cache_cmp, cache_slc, cache_win, state_conf_conv, state_ffn_conv, page_table, norm_g, ab_w_in, ab_conv_w, ab_conv_b, ab_dt_bias, ab_a_log, ab_d_skip, ab_ssm_norm, ab_cmp_pe, ab_cmp_w1, ab_cmp_b1, ab_cmp_w2, ab_cmp_b2, ab_w_out, c_w_pw1, c_b_pw1, c_dw_w, c_dw_b, c_ln_g, c_ln_b, c_w_pw2, c_b_pw2, ffn_w_up, ffn_conv_w, ffn_conv_b, ffn_w_down):
    bsz, seq, d_model = x_prompt.shape
    db, dseq, _ = x_sample.shape
    assert dseq == 1
    depth = norm_g.shape[0]
    n_heads = ab_dt_bias.shape[1]
    d_inner = n_heads * SSM_HEAD_DIM
    conv_dim = ab_conv_w.shape[2]
    d_ff2 = ffn_w_up.shape[2]
    d_conf = c_dw_w.shape[2] if c_dw_w.shape[0] else d_model
    past_len = page_table.shape[1] * cache_cmp.shape[2]
    mp, ms = bsz * seq, db
    hp = x_prompt.reshape(mp, d_model)
    hs = x_sample.reshape(ms, d_model)
    tm_p = _pick_tile(mp, 1024)
    tm_post = _pick_tile(mp, 512)
    tm_seq = _pick_tile(seq, 256)
    tn_ff = _col_tile(d_ff2 // 2, 1536)

    cos_p, sin_p = _rope_tables(jnp.arange(seq, dtype=jnp.int32))
    cos_s, sin_s = _rope_tables(jnp.full((ms,), past_len, jnp.int32))
    nchk = seq // CMP_STRIDE
    n_cmp = nchk - CMP_LEN // CMP_STRIDE + 1

    outs = {k: [] for k in ("ssm_p", "ssm_s", "sconv_p", "sconv_s", "cmp_p", "cmp_s", "slc_p", "slc_s",
                            "win_p", "win_s", "conf_p", "conf_s", "ffn_p", "ffn_s")}
    kvshape = (2, NSA_KV, NSA_HEAD_DIM)
    for layer in range(depth):
        g = norm_g[layer]
        if layer % 2 == 0:
            a = layer // 2
            w_in = _prep_w_in(ab_w_in[a], d_inner, conv_dim, n_heads)
            n_proj = w_in.shape[1]
            zero_b = jnp.zeros((n_proj,), F32)
            w_out = ab_w_out[a].astype(BF16)
            zero_d = jnp.zeros((d_model,), F32)
            tn_in = _col_tile(n_proj, 1536)
            proj = norm_matmul(hp, g[0], w_in, zero_b, tm_p, tn_in)
            q_rot, q_swp, kv_pad, kvc, kvs, kvw = rope_call(proj, cos_p, sin_p, _pick_tile(seq, 512))
            ch = kvc.reshape(bsz, nchk, CMP_STRIDE, 4, NSA_HEAD_DIM).transpose(0, 3, 1, 2, 4)
            ch = ch.reshape(bsz, 4, nchk, CMP_STRIDE * NSA_HEAD_DIM)
            pe_flat = ab_cmp_pe[a].reshape(2, 1, CMP_LEN * NSA_HEAD_DIM)
            w2p = jnp.pad(ab_cmp_w2[a], ((0, 0), (0, 0), (0, LANES - NSA_HEAD_DIM))).astype(BF16)
            b2p = jnp.pad(ab_cmp_b2[a], ((0, 0), (0, LANES - NSA_HEAD_DIM))).reshape(2, 1, LANES)
            ckv = compress_call(ch, pe_flat, ab_cmp_w1[a].astype(BF16), ab_cmp_b1[a].reshape(2, 1, -1), w2p, b2p)
            y_nsa = nsa_prompt_call(q_rot, q_swp, kv_pad, ckv, proj, bsz, seq, n_cmp, n_heads)
            y_ssm, st = ssd_call(proj, ab_conv_w[a], ab_conv_b[a], ab_dt_bias[a], ab_a_log[a],
                                 ab_d_skip[a], ab_ssm_norm[a], bsz, seq)
            hp = matmul_post([y_ssm, y_nsa], [w_out[:d_inner], w_out[d_inner:]], zero_d, hp, g[1], tm_post)
            hpg = n_heads // SSM_GROUPS
            outs["ssm_p"].append(st.reshape(bsz, SSM_GROUPS, SSM_STATE, hpg, SSM_HEAD_DIM)
                                 .transpose(0, 1, 3, 4, 2).reshape(bsz, n_heads, SSM_HEAD_DIM, SSM_STATE))
            tail_p = proj.reshape(bsz, seq, n_proj)[:, seq - (SSM_CONV - 1):]
            outs["sconv_p"].append(tail_p[:, :, d_inner + 1024:d_inner + 1024 + conv_dim])
            outs["cmp_p"].append(kvc.reshape((bsz, seq) + kvshape))
            outs["slc_p"].append(kvs.reshape((bsz, seq) + kvshape))
            outs["win_p"].append(kvw.reshape((bsz, seq) + kvshape)[:, seq - min(WINDOW, seq):])
            proj_s = norm_matmul(hs, g[0], w_in, zero_b, ms, tn_in)
            q_s, _, _, kvc_s, kvs_s, kvw_s = rope_call(proj_s, cos_s, sin_s, ms)
            z_s = proj_s[:, :d_inner]
            xbc_s = proj_s[:, d_inner + 1024:d_inner + 1024 + conv_dim]
            dt_s = proj_s[:, n_proj - LANES:n_proj - LANES + n_heads]
            gates_s = proj_s[:, n_proj - LANES + n_heads:n_proj - LANES + n_heads + 3 * NSA_HEADS]
            ys_ssm, hist_s, h_s = _sample_ssd_step(z_s, xbc_s, dt_s, state_ssm_conv[a], state_ssm[a],
                                                   ab_conv_w[a], ab_conv_b[a], ab_dt_bias[a], ab_a_log[a],
                                                   ab_d_skip[a], ab_ssm_norm[a])
            ys_nsa = sample_nsa_call(page_table, cache_cmp[a], cache_slc[a], cache_win[a], q_s, kvs_s, kvw_s,
                                     gates_s, ab_cmp_pe[a], ab_cmp_w1[a], ab_cmp_b1[a], ab_cmp_w2[a], ab_cmp_b2[a])
            kvc_s = kvc_s.reshape((ms, 1) + kvshape)
            kvs_s = kvs_s.reshape((ms, 1) + kvshape)
            kvw_s = kvw_s.reshape((ms, 1) + kvshape)
            hs = matmul_post([ys_ssm.astype(BF16), ys_nsa.astype(BF16)], [w_out[:d_inner], w_out[d_inner:]],
                             zero_d, hs, g[1], ms)
            outs["ssm_s"].append(h_s)
            outs["sconv_s"].append(hist_s)
            outs["cmp_s"].append(kvc_s)
            outs["slc_s"].append(kvs_s)
            outs["win_s"].append(kvw_s)
        else:
            c = layer // 2
            w1 = c_w_pw1[c].astype(BF16)
            w2 = c_w_pw2[c].astype(BF16)
            u = norm_matmul(hp, g[0], w1, c_b_pw1[c], tm_p, 1024)
            act, tail = conformer_mid_call(u, c_dw_w[c], c_dw_b[c], c_ln_g[c], c_ln_b[c], bsz, seq, tm_seq)
            hp = matmul_post([act], [w2], c_b_pw2[c], hp, g[1], tm_post)
            outs["conf_p"].append(tail[:, CONF_HALO - (CONF_CONV - 1):])
            u_s = norm_matmul(hs, g[0], w1, c_b_pw1[c], ms, 512)
            a_s = u_s[:, :d_conf] * jax.nn.sigmoid(u_s[:, d_conf:])
            ah = jnp.concatenate([state_conf_conv[c], a_s[:, None, :]], axis=1)
            conv = jnp.sum(ah * c_dw_w[c][None], axis=1) + c_dw_b[c]
            mu = jnp.mean(conv, axis=-1, keepdims=True)
            var = jnp.mean(jnp.square(conv - mu), axis=-1, keepdims=True)
            yln = (conv - mu) * lax.rsqrt(var + EPS) * c_ln_g[c] + c_ln_b[c]
            hs = matmul_post([(yln * jax.nn.sigmoid(yln)).astype(BF16)], [w2], c_b_pw2[c], hs, g[1], ms)
            outs["conf_s"].append(ah[:, 1:])
        w_up = ffn_w_up[layer].astype(BF16)
        w_dn = ffn_w_down[layer].astype(BF16)
        zero_up = jnp.zeros((d_ff2,), F32)
        zero_d = jnp.zeros((d_model,), F32)
        act, tail_u = ffn_up_call(hp, g[2], w_up, ffn_conv_w[layer], ffn_conv_b[layer], bsz, seq,
                                  _pick_tile(seq, 512), tn_ff)
        hp = matmul_post([act], [w_dn], zero_d, hp, g[3], tm_post)
        outs["ffn_p"].append(tail_u[:, FFN_HALO - (FFN_CONV - 1):])
        u_s = norm_matmul(hs, g[2], w_up, zero_up, ms, tn_ff)
        uh = jnp.concatenate([state_ffn_conv[layer], u_s[:, None, :]], axis=1)
        y_s = jnp.sum(uh * ffn_conv_w[layer][None], axis=1) + ffn_conv_b[layer]
        act_s = jax.nn.gelu(y_s[:, :d_ff2 // 2], approximate=True) * y_s[:, d_ff2 // 2:]
        hs = matmul_post([act_s.astype(BF16)], [w_dn], zero_d, hs, g[3], ms)
        outs["ffn_s"].append(uh[:, 1:])

    st = lambda k: jnp.stack(outs[k])
    return (hp.reshape(bsz, seq, d_model), hs.reshape(db, dseq, d_model),
            st("ssm_p"), st("ssm_s"), st("sconv_p"), st("sconv_s"), st("cmp_p"), st("cmp_s"),
            st("slc_p"), st("slc_s"), st("win_p"), st("win_s"), st("conf_p"), st("conf_s"),
            st("ffn_p"), st("ffn_s"))
```

```python
import functools
import math

import jax
import jax.numpy as jnp
import numpy as np
from jax import lax
from jax.experimental import pallas as pl
from jax.experimental.pallas import tpu as pltpu

F32 = jnp.float32
BF16 = jnp.bfloat16

EPS = 1e-6
LANES = 128
VMEM_LIMIT = 48 * 1024 * 1024

SSM_HEAD_DIM = 64
SSM_GROUPS = 4
SSM_STATE = 128
SSM_CONV = 4
SSM_CHUNK = 128
NSA_HEADS = 16
NSA_KV = 2
NSA_HPG = NSA_HEADS // NSA_KV
NSA_HEAD_DIM = 64
CMP_LEN = 32
CMP_STRIDE = 16
SLC_LEN = 64
SLC_TOPN = 16
WINDOW = 512
Q_BLOCK = 128
ROPE_THETA = 10000.0
FORCE_SCORE = 1e4
QK_SCALE_LOG2 = NSA_HEAD_DIM ** -0.5 * math.log2(math.e)
CONF_CONV = 31
FFN_CONV = 3


def _params(*sem):
    return pltpu.CompilerParams(dimension_semantics=sem, vmem_limit_bytes=VMEM_LIMIT)


def _split3(v):
    hi = v.astype(BF16)
    r1 = v - hi.astype(F32)
    mid = r1.astype(BF16)
    lo = (r1 - mid.astype(F32)).astype(BF16)
    return hi, mid, lo


def _dot(a, b):
    return jnp.dot(a, b, preferred_element_type=F32)


def _dot_nt(a, b):
    return lax.dot_general(a, b, (((1,), (1,)), ((), ())), preferred_element_type=F32)


def _dot3(v, m_bf16):
    hi, mid, lo = _split3(v)
    return _dot(hi, m_bf16) + _dot(mid, m_bf16) + _dot(lo, m_bf16)


def _dot3_left(m_bf16, v):
    hi, mid, lo = _split3(v)
    return _dot(m_bf16, hi) + _dot(m_bf16, mid) + _dot(m_bf16, lo)


def _norm_matmul_body(x_ref, g_ref, w_ref, b_ref, o_ref, xn_ref):
    @pl.when(pl.program_id(1) == 0)
    def _():
        x = x_ref[...]
        ms = jnp.mean(x * x, axis=-1, keepdims=True)
        xn_ref[...] = (x * lax.rsqrt(ms + EPS) * g_ref[...]).astype(BF16)

    o_ref[...] = _dot(xn_ref[...], w_ref[...]) + b_ref[...]


def norm_matmul(x, g, w, b, tm, tn):
    m, k = x.shape
    n = w.shape[1]
    return pl.pallas_call(
        _norm_matmul_body,
        grid=(m // tm, n // tn),
        in_specs=[pl.BlockSpec((tm, k), lambda i, j: (i, 0)),
                  pl.BlockSpec((1, k), lambda i, j: (0, 0)),
                  pl.BlockSpec((k, tn), lambda i, j: (0, j)),
                  pl.BlockSpec((1, tn), lambda i, j: (0, j))],
        out_specs=pl.BlockSpec((tm, tn), lambda i, j: (i, j)),
        out_shape=jax.ShapeDtypeStruct((m, n), F32),
        scratch_shapes=[pltpu.VMEM((tm, k), BF16)],
        compiler_params=_params("parallel", "arbitrary"),
        name="norm_matmul",
    )(x, g.reshape(1, k), w, b.reshape(1, n))


def _matmul_post_body(n_pairs, *refs):
    a_refs = refs[:n_pairs]
    w_refs = refs[n_pairs:2 * n_pairs]
    b_ref, r_ref, g_ref, o_ref = refs[2 * n_pairs:]
    acc = _dot(a_refs[0][...], w_refs[0][...])
    for a_ref, w_ref in zip(a_refs[1:], w_refs[1:]):
        acc = acc + _dot(a_ref[...], w_ref[...])
    acc = acc + b_ref[...]
    ms = jnp.mean(acc * acc, axis=-1, keepdims=True)
    o_ref[...] = r_ref[...] + acc * lax.rsqrt(ms + EPS) * g_ref[...]


def matmul_post(a_list, w, b, resid, g, tm):
    m, n = resid.shape
    n_pairs = len(a_list)
    in_specs = [pl.BlockSpec((tm, a.shape[1]), lambda i: (i, 0)) for a in a_list]
    offset = 0
    for a in a_list:
        k = a.shape[1]
        assert offset % k == 0
        in_specs.append(pl.BlockSpec((k, n), functools.partial(lambda i, blk: (blk, 0), blk=offset // k)))
        offset += k
    assert offset == w.shape[0]
    w_list = [w] * n_pairs
    in_specs += [pl.BlockSpec((1, n), lambda i: (0, 0)),
                 pl.BlockSpec((tm, n), lambda i: (i, 0)),
                 pl.BlockSpec((1, n), lambda i: (0, 0))]
    return pl.pallas_call(
        functools.partial(_matmul_post_body, n_pairs),
        grid=(m // tm,),
        in_specs=in_specs,
        out_specs=pl.BlockSpec((tm, n), lambda i: (i, 0)),
        out_shape=jax.ShapeDtypeStruct((m, n), F32),
        compiler_params=_params("parallel"),
        name="matmul_post",
    )(*a_list, *w_list, b.reshape(1, n), resid, g.reshape(1, n))


def _rope_tables(pos):
    half = NSA_HEAD_DIM // 2
    inv = ROPE_THETA ** (-jnp.arange(half, dtype=F32) / half)
    ang = pos.astype(F32)[:, None] * inv[None, :]
    cos, sin = jnp.cos(ang), jnp.sin(ang)
    reps = LANES // NSA_HEAD_DIM
    cosf = jnp.tile(jnp.concatenate([cos, cos], axis=-1), (1, reps))
    sinf = jnp.tile(jnp.concatenate([-sin, sin], axis=-1), (1, reps))
    return cosf, sinf


def _rotate_half_partner(x):
    w = x.shape[-1]
    lane = lax.broadcasted_iota(jnp.int32, x.shape, x.ndim - 1)
    first = (lane % NSA_HEAD_DIM) < (NSA_HEAD_DIM // 2)
    return jnp.where(first, pltpu.roll(x, w - NSA_HEAD_DIM // 2, x.ndim - 1),
                     pltpu.roll(x, NSA_HEAD_DIM // 2, x.ndim - 1))


def _swap_head_pair(x):
    w = x.shape[-1]
    lane = lax.broadcasted_iota(jnp.int32, x.shape, x.ndim - 1)
    first = (lane % LANES) < NSA_HEAD_DIM
    return jnp.where(first, pltpu.roll(x, w - NSA_HEAD_DIM, x.ndim - 1),
                     pltpu.roll(x, NSA_HEAD_DIM, x.ndim - 1))


def _rope_body(q_ref, kv_ref, cos_ref, sin_ref, q_o, qs_o, kv_o, kvc_o, kvs_o, kvw_o):
    cos = cos_ref[...]
    sin = sin_ref[...]
    q = q_ref[...]
    nq = q.shape[1] // LANES
    cq = jnp.concatenate([cos] * nq, axis=1)
    sq = jnp.concatenate([sin] * nq, axis=1)
    qr = (q * cq + _rotate_half_partner(q) * sq) * QK_SCALE_LOG2
    q_o[...] = qr.astype(BF16)
    qs_o[...] = _swap_head_pair(qr).astype(BF16)

    kv = kv_ref[...]
    lane = lax.broadcasted_iota(jnp.int32, (kv.shape[0], LANES), 1)
    lo = lane < NSA_HEAD_DIM
    zero = jnp.zeros((kv.shape[0], LANES), F32)
    padded = []
    for c, full_o in enumerate((kvc_o, kvs_o, kvw_o)):
        k = kv[:, c * 256:c * 256 + LANES]
        v = kv[:, c * 256 + LANES:(c + 1) * 256]
        kr = k * cos + _rotate_half_partner(k) * sin
        full_o[...] = jnp.concatenate([kr, v], axis=1)
        if c > 0:
            kr_sw = pltpu.roll(kr, NSA_HEAD_DIM, 1)
            v_sw = pltpu.roll(v, NSA_HEAD_DIM, 1)
            padded += [jnp.where(lo, kr, zero), jnp.where(lo, kr_sw, zero),
                       jnp.where(lo, v, zero), jnp.where(lo, v_sw, zero)]
    kv_o[...] = jnp.concatenate(padded, axis=1).astype(BF16)


def rope_call(proj, cosf, sinf, tm):
    m = proj.shape[0]
    nblk = cosf.shape[0] // tm
    f32s = jax.ShapeDtypeStruct((m, 256), F32)
    return pl.pallas_call(
        _rope_body,
        grid=(m // tm,),
        in_specs=[pl.BlockSpec((tm, 1024), lambda i: (i, 2)),
                  pl.BlockSpec((tm, 768), lambda i: (i, 8)),
                  pl.BlockSpec((tm, LANES), lambda i: (i % nblk, 0)),
                  pl.BlockSpec((tm, LANES), lambda i: (i % nblk, 0))],
        out_specs=[pl.BlockSpec((tm, 1024), lambda i: (i, 0)),
                   pl.BlockSpec((tm, 1024), lambda i: (i, 0)),
                   pl.BlockSpec((tm, 1024), lambda i: (i, 0)),
                   pl.BlockSpec((tm, 256), lambda i: (i, 0)),
                   pl.BlockSpec((tm, 256), lambda i: (i, 0)),
                   pl.BlockSpec((tm, 256), lambda i: (i, 0))],
        out_shape=[jax.ShapeDtypeStruct((m, 1024), BF16), jax.ShapeDtypeStruct((m, 1024), BF16),
                   jax.ShapeDtypeStruct((m, 1024), BF16), f32s, f32s, f32s],
        compiler_params=_params("parallel"),
        name="rope",
    )(proj, proj, cosf, sinf)


def _compress_body(ch_ref, pe_ref, w1_ref, b1_ref, w2_ref, b2_ref, o_ref):
    ch = ch_ref[0, 0]
    nc, half = ch.shape
    pe = pe_ref[0]
    top = _dot((ch + pe[:, :half]).astype(BF16), w1_ref[0, :half, :])
    bot = _dot((ch + pe[:, half:]).astype(BF16), w1_ref[0, half:, :])
    hid = top + pltpu.roll(bot, nc - 1, 0) + b1_ref[0]
    hid = hid * jax.nn.sigmoid(hid)
    o_ref[0, 0] = _dot(hid.astype(BF16), w2_ref[0]) + b2_ref[0]


def compress_call(ch, pe_flat, w1, b1, w2p, b2p):
    bsz, _, nc, half = ch.shape
    hidden = w1.shape[-1]
    return pl.pallas_call(
        _compress_body,
        grid=(bsz, 4),
        in_specs=[pl.BlockSpec((1, 1, nc, half), lambda b, q: (b, q, 0, 0)),
                  pl.BlockSpec((1, 1, 2 * half), lambda b, q: (q // 2, 0, 0)),
                  pl.BlockSpec((1, 2 * half, hidden), lambda b, q: (q // 2, 0, 0)),
                  pl.BlockSpec((1, 1, hidden), lambda b, q: (q // 2, 0, 0)),
                  pl.BlockSpec((1, hidden, LANES), lambda b, q: (q // 2, 0, 0)),
                  pl.BlockSpec((1, 1, LANES), lambda b, q: (q // 2, 0, 0))],
        out_specs=pl.BlockSpec((1, 1, nc, LANES), lambda b, q: (b, q, 0, 0)),
        out_shape=jax.ShapeDtypeStruct((bsz, 4, nc, LANES), F32),
        compiler_params=_params("parallel", "parallel"),
        name="compress",
    )(ch, pe_flat, w1, b1, w2p, b2p)


def _overlap_np(nc, ns):
    i = np.arange(nc)[:, None]
    j = np.arange(ns)[None, :]
    return ((i * CMP_STRIDE < (j + 1) * SLC_LEN) & (i * CMP_STRIDE + CMP_LEN > j * SLC_LEN)).astype(np.float32)


def _nsa_t_body(seq, n_cmp, ck, wk, n_dt, q_ref, qs_ref, kvs_ref, kvw_ref, ckv_ref, small_ref, ovt_ref, o_ref):
    i = pl.program_id(1)
    tq = Q_BLOCK
    dh = NSA_HEAD_DIM
    n_slc = seq // SLC_LEN
    nsp = -(-n_slc // LANES) * LANES
    nc = ckv_ref.shape[2]
    npair = NSA_HPG // 2
    rows = NSA_HPG * tq
    neg = -jnp.inf
    tpos_t = i * tq + lax.broadcasted_iota(jnp.int32, (1, tq), 1)
    q = q_ref[...]
    qs = qs_ref[...]
    gates_t = jax.nn.sigmoid(small_ref[...]).T

    def mask_heads(s, mask):
        return jnp.concatenate([jnp.where(mask, s[:, h * tq:(h + 1) * tq], neg) for h in range(NSA_HPG)], axis=1)

    def values_t(v):
        return v.astype(F32).T[0:dh].astype(BF16)

    def gated_pair_tiles(o_t, g, branch):
        tiles = []
        for p in range(npair):
            c_even = n_dt + 3 * (g * NSA_HPG + 2 * p) + branch
            even = o_t[:, p * tq:(p + 1) * tq] * gates_t[c_even:c_even + 1, :]
            odd = o_t[:, (npair + p) * tq:(npair + p + 1) * tq] * gates_t[c_even + 3:c_even + 4, :]
            tiles.append(jnp.concatenate([even, odd], axis=0))
        return tiles

    out_tiles, qgs, sels = [], [], []
    for g in range(NSA_KV):
        qg = jnp.concatenate([q[:, (g * npair + p) * LANES:(g * npair + p + 1) * LANES] for p in range(npair)]
                             + [qs[:, (g * npair + p) * LANES:(g * npair + p + 1) * LANES] for p in range(npair)],
                             axis=0)
        s_c = _dot_nt(ckv_ref[0, g].astype(BF16), qg)
        cidx = lax.broadcasted_iota(jnp.int32, (nc, 1), 0)
        s_c = mask_heads(s_c, (cidx * CMP_STRIDE + (CMP_LEN - 1) <= tpos_t) & (cidx < n_cmp))
        mx = jnp.max(s_c, axis=0, keepdims=True)
        e = jnp.exp2(s_c - jnp.where(mx == neg, 0.0, mx))
        d = jnp.sum(e, axis=0, keepdims=True)
        p_c = e / jnp.where(d > 0, d, 1.0)
        vc_t = ckv_ref[0, NSA_KV + g].T[0:dh].astype(BF16)
        tiles = gated_pair_tiles(_dot(vc_t, p_c.astype(BF16)), g, 0)
        p_sum = p_c[:, 0:tq]
        for h in range(1, NSA_HPG):
            p_sum = p_sum + p_c[:, h * tq:(h + 1) * tq]
        ovt = ovt_ref[...]
        imp_t = sum(_dot(ovt, part) for part in _split3(p_sum))
        jidx = lax.broadcasted_iota(jnp.int32, (n_slc, 1), 0)
        qblk = tpos_t // SLC_LEN
        valid = jidx * SLC_LEN <= tpos_t
        forced = (jidx == 0) | (jidx == qblk) | (jidx == qblk - 1)
        score = jnp.where(valid, imp_t + FORCE_SCORE * forced.astype(F32), neg)
        rank = jnp.zeros((n_slc, tq), F32)
        for k in range(n_slc):
            row = score[k:k + 1, :]
            ahead = (row > score) | ((row == score) & (jidx > k))
            rank = rank + ahead.astype(F32)
        sel_t = (rank < float(min(SLC_TOPN, n_slc))).astype(BF16)
        if nsp > n_slc:
            sel_t = jnp.concatenate([sel_t, jnp.zeros((nsp - n_slc, tq), BF16)], axis=0)
        qgs.append(qg)
        sels.append(sel_t)

        w_start = jnp.clip((i - WINDOW // tq) * tq, 0, seq - wk)
        w_start = pl.multiple_of(w_start, tq)
        k_w = kvw_ref[pl.ds(w_start, wk), g * LANES:(g + 1) * LANES]
        v_w = kvw_ref[pl.ds(w_start, wk), (NSA_KV + g) * LANES:(NSA_KV + g + 1) * LANES]
        dpos = tpos_t - (w_start + lax.broadcasted_iota(jnp.int32, (wk, 1), 0))
        s_w = mask_heads(_dot_nt(k_w, qg), (dpos >= 0) & (dpos < WINDOW))
        e_w = jnp.exp2(s_w - jnp.max(s_w, axis=0, keepdims=True))
        o_w = _dot(values_t(v_w), e_w.astype(BF16)) / jnp.sum(e_w, axis=0, keepdims=True)
        out_tiles.append([t + u for t, u in zip(tiles, gated_pair_tiles(o_w, g, 2))])

    def slc_step(c, carry):
        start = pl.multiple_of(c * ck, ck)
        blk = c * (ck // SLC_LEN) + lax.broadcasted_iota(jnp.int32, (ck, nsp), 0) // SLC_LEN
        expand_t = (lax.broadcasted_iota(jnp.int32, (ck, nsp), 1) == blk).astype(BF16)
        causal = start + lax.broadcasted_iota(jnp.int32, (ck, 1), 0) <= tpos_t
        new = []
        for g in range(NSA_KV):
            m_i, l_i, acc = carry[g]
            k_c = kvs_ref[pl.ds(start, ck), g * LANES:(g + 1) * LANES]
            v_c = kvs_ref[pl.ds(start, ck), (NSA_KV + g) * LANES:(NSA_KV + g + 1) * LANES]
            mb = (_dot(expand_t, sels[g]) > 0.5) & causal
            s = mask_heads(_dot_nt(k_c, qgs[g]), mb)
            m_new = jnp.maximum(m_i, jnp.max(s, axis=0, keepdims=True))
            alpha = jnp.exp2(m_i - m_new)
            p = jnp.exp2(s - m_new)
            l_new = alpha * l_i + jnp.sum(p, axis=0, keepdims=True)
            new.append((m_new, l_new, alpha * acc + _dot(values_t(v_c), p.astype(BF16))))
        return tuple(new)

    n_chunks = (i * tq + tq - 1) // ck + 1
    init = (jnp.full((1, rows), neg, F32), jnp.zeros((1, rows), F32), jnp.zeros((dh, rows), F32))
    final = lax.fori_loop(0, n_chunks, slc_step, (init,) * NSA_KV)
    tiles_all = []
    for g in range(NSA_KV):
        _, l_s, acc_s = final[g]
        tiles_all += [(t + u).T for t, u in zip(out_tiles[g], gated_pair_tiles(acc_s / l_s, g, 1))]
    o_ref[...] = jnp.concatenate(tiles_all, axis=1).astype(BF16)


def nsa_prompt_call(q_rot, q_swp, kv_pad, ckv, proj, bsz, seq, n_cmp, n_dt):
    nq = seq // Q_BLOCK
    nc = ckv.shape[2]
    n_slc = seq // SLC_LEN
    ck = min(512, seq)
    wk = min(WINDOW + Q_BLOCK, seq)
    ovt = jnp.asarray(np.pad(_overlap_np(n_cmp, n_slc), ((0, nc - n_cmp), (0, 0))).T, BF16)
    small_blk = proj.shape[1] // LANES - 1
    return pl.pallas_call(
        functools.partial(_nsa_t_body, seq, n_cmp, ck, wk, n_dt),
        grid=(bsz, nq),
        in_specs=[pl.BlockSpec((Q_BLOCK, 1024), lambda b, i: (b * nq + i, 0)),
                  pl.BlockSpec((Q_BLOCK, 1024), lambda b, i: (b * nq + i, 0)),
                  pl.BlockSpec((seq, 512), lambda b, i: (b, 0)),
                  pl.BlockSpec((seq, 512), lambda b, i: (b, 1)),
                  pl.BlockSpec((1, 4, nc, LANES), lambda b, i: (b, 0, 0, 0)),
                  pl.BlockSpec((Q_BLOCK, LANES), lambda b, i: (b * nq + i, small_blk)),
                  pl.BlockSpec((n_slc, nc), lambda b, i: (0, 0))],
        out_specs=pl.BlockSpec((Q_BLOCK, 1024), lambda b, i: (b * nq + i, 0)),
        out_shape=jax.ShapeDtypeStruct((bsz * seq, 1024), BF16),
        compiler_params=_params("parallel", "arbitrary"),
        name="nsa_prompt",
    )(q_rot, q_swp, kv_pad, kv_pad, ckv, proj, ovt)


def _page_copy(cache_hbm, pt_ref, buf, sem, b, p, half, slot):
    src = cache_hbm.at[pt_ref[b, p], :, pl.ds(half * LANES, LANES)]
    return pltpu.make_async_copy(src, buf.at[slot, p, half], sem.at[slot])


def _sample_nsa_body(n_pages, n_cmp, qpos, w_rows,
                     pt_ref, ccmp_hbm, cslc_hbm, q16_ref, qpad_ref, new_ref, win_ref, gates_ref,
                     pet_ref, w1t_ref, b1_ref, w2t_ref, b2_ref, ov_ref,
                     y_ref, buf, sem, sel_scr, oc_scr):
    s = pl.program_id(0)
    nb = pl.num_programs(0) // 2
    b = s // 2
    page = buf.shape[3]
    n_rows = n_pages * page
    nch = n_rows // CMP_STRIDE
    nsp = ov_ref.shape[1]
    heads = NSA_HEADS
    row_g0 = lax.broadcasted_iota(jnp.int32, (heads, 1), 0) < NSA_HPG

    def fetch(cache_hbm, bb, slot):
        for p in range(n_pages):
            for half in range(2):
                _page_copy(cache_hbm, pt_ref, buf, sem, bb, p, half, slot).start()

    def wait(cache_hbm, bb, slot):
        for p in range(n_pages):
            for half in range(2):
                _page_copy(cache_hbm, pt_ref, buf, sem, bb, p, half, slot).wait()

    @pl.when(s == 0)
    def _():
        fetch(ccmp_hbm, 0, 0)

    @pl.when(s % 2 == 0)
    def _():
        fetch(cslc_hbm, b, 1)
        wait(ccmp_hbm, b, 0)
        tops = [[] for _ in range(4)]
        bots = [[] for _ in range(4)]
        for r in range(CMP_STRIDE):
            for kind in range(2):
                x_r = buf[pl.ds(0, 1), :, pl.ds(kind, 1), pl.ds(r, page // CMP_STRIDE, stride=CMP_STRIDE), :]
                xt = x_r.reshape(nch, LANES).T
                pe_rows = pl.ds(kind * LANES, LANES)
                top = (xt + pet_ref[pe_rows, r:r + 1]).astype(BF16)
                bot = (xt + pet_ref[pe_rows, CMP_STRIDE + r:CMP_STRIDE + r + 1]).astype(BF16)
                for g in range(NSA_KV):
                    tops[kind * NSA_KV + g].append(top[g * NSA_HEAD_DIM:(g + 1) * NSA_HEAD_DIM])
                    bots[kind * NSA_KV + g].append(bot[g * NSA_HEAD_DIM:(g + 1) * NSA_HEAD_DIM])
        half_w = CMP_STRIDE * NSA_HEAD_DIM
        ckv_t = []
        for qq in range(4):
            kind = qq // NSA_KV
            w1t = w1t_ref[kind]
            hid = (_dot(w1t[:, :half_w], jnp.concatenate(tops[qq], axis=0))
                   + pltpu.roll(_dot(w1t[:, half_w:], jnp.concatenate(bots[qq], axis=0)), nch - 1, 1)
                   + b1_ref[kind])
            hid = hid * jax.nn.sigmoid(hid)
            ckv_t.append((_dot(w2t_ref[kind], hid.astype(BF16)) + b2_ref[kind]).astype(BF16))
        q16 = q16_ref[0]
        cidx = lax.broadcasted_iota(jnp.int32, (1, nch), 1)
        m_c = (cidx * CMP_STRIDE + (CMP_LEN - 1) <= qpos) & (cidx < n_cmp)
        oc = []
        for g in range(NSA_KV):
            s_c = jnp.where(m_c, _dot(q16, ckv_t[g]), -jnp.inf)
            mx = jnp.max(s_c, axis=-1, keepdims=True)
            mx = jnp.where(mx == -jnp.inf, 0.0, mx)
            e = jnp.exp2(s_c - mx)
            d = jnp.sum(e, axis=-1, keepdims=True)
            p_c = e / jnp.where(d > 0, d, 1.0)
            oc.append(_dot_nt(p_c.astype(BF16), ckv_t[NSA_KV + g]))
            in_group = row_g0 if g == 0 else jnp.logical_not(row_g0)
            p_sum = jnp.sum(jnp.where(in_group, p_c, 0.0), axis=0, keepdims=True)
            imp = _dot3(jnp.broadcast_to(p_sum, (8, nch)), ov_ref[...])[0:1]
            jidx = lax.broadcasted_iota(jnp.int32, (1, nsp), 1)
            qblk = qpos // SLC_LEN
            forced = (jidx == 0) | (jidx == qblk) | (jidx == qblk - 1)
            score = jnp.where(jidx * SLC_LEN <= qpos, imp + FORCE_SCORE * forced.astype(F32), -jnp.inf)
            s_j = jnp.broadcast_to(score, (nsp, nsp))
            s_k = s_j.T
            kk = lax.broadcasted_iota(jnp.int32, (nsp, nsp), 0)
            jj = lax.broadcasted_iota(jnp.int32, (nsp, nsp), 1)
            ahead = (s_k > s_j) | ((s_k == s_j) & (kk < jj))
            rank = jnp.sum(ahead.astype(F32), axis=0, keepdims=True)
            sel_scr[g:g + 1, :] = (rank < float(SLC_TOPN)).astype(F32)
        oc_scr[...] = jnp.where(row_g0, oc[0], oc[1])

    @pl.when(s % 2 == 1)
    def _():
        @pl.when(b + 1 < nb)
        def _():
            fetch(ccmp_hbm, b + 1, 0)

        wait(cslc_hbm, b, 1)
        qpad = qpad_ref[0]
        sel16 = jnp.where(row_g0, jnp.broadcast_to(sel_scr[0:1, :], (heads, nsp)),
                          jnp.broadcast_to(sel_scr[1:2, :], (heads, nsp)))
        sel16_bf = sel16.astype(BF16)
        ppc = 8
        ck = ppc * page
        n_ck = n_rows // ck
        vals, scores = [], []
        for c in range(n_ck):
            k_c = buf[1, c * ppc:(c + 1) * ppc, 0].reshape(ck, LANES).astype(BF16)
            vals.append(buf[1, c * ppc:(c + 1) * ppc, 1].reshape(ck, LANES).astype(BF16))
            blk = c * (ck // SLC_LEN) + lax.broadcasted_iota(jnp.int32, (nsp, ck), 1) // SLC_LEN
            expand = (lax.broadcasted_iota(jnp.int32, (nsp, ck), 0) == blk).astype(BF16)
            kpos = c * ck + lax.broadcasted_iota(jnp.int32, (1, ck), 1)
            mask = (_dot(sel16_bf, expand) > 0.5) & (kpos <= qpos)
            scores.append(jnp.where(mask, _dot_nt(qpad, k_c), -jnp.inf))
        new = new_ref[0]
        ks_new = new[:, 0:LANES].astype(BF16)
        vs_new = new[:, LANES:2 * LANES].astype(BF16)
        kw_new = new[:, 2 * LANES:3 * LANES].astype(BF16)
        vw_new = new[:, 3 * LANES:].astype(BF16)
        first = lax.broadcasted_iota(jnp.int32, (1, new.shape[0]), 1) == 0
        nblk = n_rows // SLC_LEN
        s_new = jnp.where(first & (sel16[:, nblk:nblk + 1] > 0.5), _dot_nt(qpad, ks_new), -jnp.inf)
        mx = jnp.max(s_new, axis=-1, keepdims=True)
        for sc in scores:
            mx = jnp.maximum(mx, jnp.max(sc, axis=-1, keepdims=True))
        e_new = jnp.exp2(s_new - mx)
        den = jnp.sum(e_new, axis=-1, keepdims=True)
        acc = _dot(e_new.astype(BF16), vs_new)
        for sc, v_c in zip(scores, vals):
            e = jnp.exp2(sc - mx)
            den = den + jnp.sum(e, axis=-1, keepdims=True)
            acc = acc + _dot(e.astype(BF16), v_c)
        o_s = acc / den
        k_w = win_ref[0, :, 0:LANES].astype(BF16)
        v_w = win_ref[0, :, LANES:2 * LANES].astype(BF16)
        dpos = w_rows - lax.broadcasted_iota(jnp.int32, (1, w_rows), 1)
        m_w = (dpos >= 0) & (dpos < WINDOW) & (qpos - dpos >= 0)
        s_w = jnp.where(m_w, _dot_nt(qpad, k_w), -jnp.inf)
        s_wn = jnp.where(first, _dot_nt(qpad, kw_new), -jnp.inf)
        mxw = jnp.maximum(jnp.max(s_w, axis=-1, keepdims=True), jnp.max(s_wn, axis=-1, keepdims=True))
        e_w = jnp.exp2(s_w - mxw)
        e_wn = jnp.exp2(s_wn - mxw)
        o_w = ((_dot(e_w.astype(BF16), v_w) + _dot(e_wn.astype(BF16), vw_new))
               / (jnp.sum(e_w, axis=-1, keepdims=True) + jnp.sum(e_wn, axis=-1, keepdims=True)))

        def own_group(o):
            return jnp.where(row_g0, o, pltpu.roll(o, NSA_HEAD_DIM, 1))[:, :NSA_HEAD_DIM]

        gt = jax.nn.sigmoid(gates_ref[0])
        y_ref[0] = gt[:, 0:1] * oc_scr[...] + gt[:, 1:2] * own_group(o_s) + gt[:, 2:3] * own_group(o_w)


def sample_nsa_call(page_table, cache_cmp, cache_slc, cache_win, q_rot, kvs_new, kvw_new, gates, pe, w1, b1, w2, b2):
    db, n_pages = page_table.shape
    n_phys, page = cache_cmp.shape[:2]
    past_len = n_pages * page
    nch = past_len // CMP_STRIDE
    n_cmp = nch - CMP_LEN // CMP_STRIDE + 1
    n_slc = -(-(past_len + 1) // SLC_LEN)
    nsp = -(-n_slc // LANES) * LANES
    w_rows = cache_win.shape[1]
    lanes = 4 * NSA_HEAD_DIM
    q16 = q_rot.reshape(db, NSA_HEADS, NSA_HEAD_DIM)
    q4 = q_rot.reshape(db, NSA_KV, NSA_HPG, NSA_HEAD_DIM)
    qpad = jnp.concatenate([jnp.pad(q4[:, g], ((0, 0), (0, 0), (g * NSA_HEAD_DIM, LANES - (g + 1) * NSA_HEAD_DIM)))
                            for g in range(NSA_KV)], axis=1)
    new = jnp.pad(jnp.concatenate([kvs_new, kvw_new], axis=1)[:, None, :], ((0, 0), (0, 15), (0, 0)))
    hidden = w1.shape[-1]
    pet = jnp.repeat(pe, NSA_KV, axis=0).transpose(0, 2, 1).reshape(lanes, CMP_LEN)
    ov = jnp.asarray(np.pad(_overlap_np(n_cmp, n_slc), ((0, nch - n_cmp), (0, nsp - n_slc))), BF16)
    full = lambda shape: pl.BlockSpec(shape, lambda s, pt: (0,) * len(shape))
    per_b = lambda shape: pl.BlockSpec((1,) + shape, lambda s, pt: (s // 2,) + (0,) * len(shape))
    grid_spec = pltpu.PrefetchScalarGridSpec(
        num_scalar_prefetch=1,
        grid=(2 * db,),
        in_specs=[pl.BlockSpec(memory_space=pl.ANY), pl.BlockSpec(memory_space=pl.ANY),
                  per_b((NSA_HEADS, NSA_HEAD_DIM)), per_b((NSA_HEADS, LANES)), per_b((16, 2 * lanes)),
                  per_b((w_rows, lanes)), per_b((NSA_HEADS, 3)),
                  full((lanes, CMP_LEN)), full((2, hidden, CMP_LEN * NSA_HEAD_DIM)), full((2, hidden, 1)),
                  full((2, NSA_HEAD_DIM, hidden)), full((2, NSA_HEAD_DIM, 1)), full((nch, nsp))],
        out_specs=per_b((NSA_HEADS, NSA_HEAD_DIM)),
        scratch_shapes=[pltpu.VMEM((2, n_pages, 2, page, LANES), F32),
                        pltpu.SemaphoreType.DMA((2,)),
                        pltpu.VMEM((8, nsp), F32),
                        pltpu.VMEM((NSA_HEADS, NSA_HEAD_DIM), F32)])
    y = pl.pallas_call(
        functools.partial(_sample_nsa_body, n_pages, n_cmp, past_len, w_rows),
        grid_spec=grid_spec,
        out_shape=jax.ShapeDtypeStruct((db, NSA_HEADS, NSA_HEAD_DIM), F32),
        compiler_params=_params("arbitrary"),
        name="sample_nsa",
    )(page_table, cache_cmp.reshape(n_phys, page, lanes), cache_slc.reshape(n_phys, page, lanes),
      q16, qpad, new, cache_win.reshape(db, w_rows, lanes), gates.reshape(db, NSA_HEADS, 3),
      pet, w1.transpose(0, 2, 1).astype(BF16), b1.reshape(2, hidden, 1),
      w2.transpose(0, 2, 1).astype(BF16), b2.reshape(2, NSA_HEAD_DIM, 1), ov)
    return y.reshape(db, NSA_HEADS * NSA_HEAD_DIM)


def _softplus(x):
    return jnp.maximum(x, 0.0) + jnp.log1p(jnp.exp(-jnp.abs(x)))


def _ssd_body(z_ref, xbc_ref, small_ref, cw_ref, cb_ref, dtb_ref, dtbt_ref, alog_ref, alogt_ref,
              dskip_ref, nrm_ref, rexp_ref, ltri_ref, utri_ref, y_ref, st_ref, xbuf, ht):
    c = pl.program_id(1)
    q = SSM_CHUNK
    d_inner = z_ref.shape[1]
    n_heads = d_inner // SSM_HEAD_DIM
    gw = d_inner // SSM_GROUPS
    n = SSM_STATE

    @pl.when(c == 0)
    def _():
        xbuf[0:8, :] = jnp.zeros((8, xbuf.shape[1]), F32)
        ht[...] = jnp.zeros(ht.shape, F32)

    x = xbc_ref[...]
    xbuf[8:8 + q, :] = x
    cw = cw_ref[...]
    conv = (x * cw[3:4] + xbuf[7:7 + q, :] * cw[2:3] + xbuf[6:6 + q, :] * cw[1:2]
            + xbuf[5:5 + q, :] * cw[0:1] + cb_ref[...])
    xbuf[0:8, :] = x[q - 8:q]
    xc = conv * jax.nn.sigmoid(conv)
    xs = xc[:, :d_inner]
    bm = xc[:, d_inner:d_inner + SSM_GROUPS * n]
    cm = xc[:, d_inner + SSM_GROUPS * n:]

    small = small_ref[...]
    dt = _softplus(small[:, 0:n_heads] + dtb_ref[...])
    dtt = _softplus(small.T[0:n_heads, :] + dtbt_ref[...])
    a = -jnp.exp(alog_ref[...])
    at = -jnp.exp(alogt_ref[...])
    cum = _dot3_left(ltri_ref[...], dt * a)
    cumt = _dot3(dtt * at, utri_ref[...])
    cum_last = cum[q - 1:q, :]
    rexp = rexp_ref[...]
    expcum_f = _dot3(jnp.exp(cum), rexp)
    toend_f = _dot3(jnp.exp(cum_last - cum) * dt, rexp)
    cdec_f = _dot3(jnp.broadcast_to(jnp.exp(cum_last), (8, n_heads)), rexp)[0:1]

    causal = (lax.broadcasted_iota(jnp.int32, (q, q), 0) >= lax.broadcasted_iota(jnp.int32, (q, q), 1))
    lane_lo = lax.broadcasted_iota(jnp.int32, (q, LANES), 1) < SSM_HEAD_DIM
    xs_bf = xs.astype(BF16)
    zero_bf = jnp.zeros((q, LANES), BF16)
    hpg = n_heads // SSM_GROUPS
    y_tiles, yoff_tiles = [], []
    for g in range(SSM_GROUPS):
        bg = bm[:, g * n:(g + 1) * n]
        cg_bf = cm[:, g * n:(g + 1) * n].astype(BF16)
        cb = _dot_nt(cg_bf, bg.astype(BF16))
        h_old = ht[g]
        yoff_tiles.append(_dot(cg_bf, h_old.astype(BF16)))
        for pr in range(hpg // 2):
            h0 = g * hpg + 2 * pr
            xp = xs_bf[:, h0 * SSM_HEAD_DIM:h0 * SSM_HEAD_DIM + LANES]
            acc = None
            for par, xm in ((0, jnp.where(lane_lo, xp, zero_bf)), (1, jnp.where(lane_lo, zero_bf, xp))):
                h = h0 + par
                seg = cum[:, h:h + 1] - cumt[h:h + 1, :]
                w = cb * jnp.exp(jnp.where(causal, seg, -jnp.inf)) * dtt[h:h + 1, :]
                part = _dot(w.astype(BF16), xm)
                acc = part if acc is None else acc + part
            y_tiles.append(acc)
        xw = (xs[:, g * gw:(g + 1) * gw] * toend_f[:, g * gw:(g + 1) * gw]).astype(BF16)
        ht[g] = h_old * cdec_f[:, g * gw:(g + 1) * gw] + _dot(bg.T.astype(BF16), xw)

    y = (jnp.concatenate(y_tiles, axis=1) + jnp.concatenate(yoff_tiles, axis=1) * expcum_f
         + dskip_ref[...] * xs)
    z = z_ref[...]
    y = y * (z * jax.nn.sigmoid(z))
    outs = []
    for g in range(SSM_GROUPS):
        yg = y[:, g * gw:(g + 1) * gw]
        ms = jnp.mean(yg * yg, axis=-1, keepdims=True)
        outs.append(yg * lax.rsqrt(ms + EPS) * nrm_ref[:, g * gw:(g + 1) * gw])
    y_ref[...] = jnp.concatenate(outs, axis=1).astype(BF16)
    st_ref[0] = ht[...]


def ssd_call(proj, conv_w, conv_b, dt_bias, a_log, d_skip, ssm_norm, bsz, seq):
    q = SSM_CHUNK
    nchunk = seq // q
    n_heads = dt_bias.shape[0]
    d_inner = n_heads * SSM_HEAD_DIM
    conv_dim = conv_w.shape[1]
    gw = d_inner // SSM_GROUPS
    small_blk = proj.shape[1] // LANES - 1
    rexp = jnp.asarray(np.repeat(np.eye(n_heads, dtype=np.float32), SSM_HEAD_DIM, axis=1), BF16)
    ltri = jnp.asarray(np.tril(np.ones((q, q), np.float32)), BF16)
    utri = jnp.asarray(np.triu(np.ones((q, q), np.float32)), BF16)
    const = lambda shape: pl.BlockSpec(shape, lambda b, c: (0,) * len(shape))
    return pl.pallas_call(
        _ssd_body,
        grid=(bsz, nchunk),
        in_specs=[pl.BlockSpec((q, d_inner), lambda b, c: (b * nchunk + c, 0)),
                  pl.BlockSpec((q, conv_dim), lambda b, c: (b * nchunk + c, 1)),
                  pl.BlockSpec((q, LANES), lambda b, c: (b * nchunk + c, small_blk)),
                  const((SSM_CONV, conv_dim)), const((1, conv_dim)),
                  const((1, n_heads)), const((n_heads, 1)), const((1, n_heads)), const((n_heads, 1)),
                  const((1, d_inner)), const((1, d_inner)),
                  const((n_heads, d_inner)), const((q, q)), const((q, q))],
        out_specs=[pl.BlockSpec((q, d_inner), lambda b, c: (b * nchunk + c, 0)),
                   pl.BlockSpec((1, SSM_GROUPS, SSM_STATE, gw), lambda b, c: (b, 0, 0, 0))],
        out_shape=[jax.ShapeDtypeStruct((bsz * seq, d_inner), BF16),
                   jax.ShapeDtypeStruct((bsz, SSM_GROUPS, SSM_STATE, gw), F32)],
        scratch_shapes=[pltpu.VMEM((8 + q, conv_dim), F32),
                        pltpu.VMEM((SSM_GROUPS, SSM_STATE, gw), F32)],
        compiler_params=_params("parallel", "arbitrary"),
        name="ssd",
    )(proj, proj, proj, conv_w, conv_b.reshape(1, -1),
      dt_bias.reshape(1, -1), dt_bias.reshape(-1, 1), a_log.reshape(1, -1), a_log.reshape(-1, 1),
      jnp.repeat(d_skip, SSM_HEAD_DIM).reshape(1, -1), ssm_norm.reshape(1, -1), rexp, ltri, utri)


CONF_HALO = 32


def _conf_body(x_ref, ng_ref, w1_ref, b1_ref, w_ref, b_ref, g_ref, beta_ref, o_ref, tail_ref, buf, shifted):
    t = pl.program_id(1)
    tm, d = o_ref.shape
    sub = 8

    @pl.when(t == 0)
    def _():
        buf[0:CONF_HALO, :] = jnp.zeros((CONF_HALO, d), F32)

    x = x_ref[...]
    ms = jnp.mean(x * x, axis=-1, keepdims=True)
    xn = (x * lax.rsqrt(ms + EPS) * ng_ref[...]).astype(BF16)
    u = _dot(xn, w1_ref[...]) + b1_ref[...]
    a = u[:, :d] * jax.nn.sigmoid(u[:, d:])
    buf[CONF_HALO:CONF_HALO + tm, :] = a
    span = shifted.shape[1]
    for s in range(1, sub):
        shifted[s - 1] = buf[s:s + span, :]
    w = w_ref[...]
    off = CONF_HALO - (CONF_CONV - 1)
    acc = b_ref[...]
    for k in range(CONF_CONV):
        s, base = (off + k) % sub, (off + k) // sub * sub
        rows = buf[base:base + tm, :] if s == 0 else shifted[s - 1, base:base + tm, :]
        acc = acc + rows * w[k:k + 1]
    buf[0:CONF_HALO, :] = a[tm - CONF_HALO:tm]
    tail_ref[0] = a[tm - CONF_HALO:tm]
    mu = jnp.mean(acc, axis=-1, keepdims=True)
    cen = acc - mu
    var = jnp.mean(cen * cen, axis=-1, keepdims=True)
    y = cen * lax.rsqrt(var + EPS) * g_ref[...] + beta_ref[...]
    o_ref[...] = (y * jax.nn.sigmoid(y)).astype(BF16)


def conformer_mid_call(x, norm_g, w1, b1, dw_w, dw_b, ln_g, ln_b, bsz, seq, tm):
    d = dw_w.shape[1]
    k = x.shape[1]
    nt = seq // tm
    const = lambda shape: pl.BlockSpec(shape, lambda b, t: (0,) * len(shape))
    return pl.pallas_call(
        _conf_body,
        grid=(bsz, nt),
        in_specs=[pl.BlockSpec((tm, k), lambda b, t: (b * nt + t, 0)),
                  const((1, k)), const((k, 2 * d)), const((1, 2 * d)),
                  const((CONF_CONV, d)), const((1, d)), const((1, d)), const((1, d))],
        out_specs=[pl.BlockSpec((tm, d), lambda b, t: (b * nt + t, 0)),
                   pl.BlockSpec((1, CONF_HALO, d), lambda b, t: (b, 0, 0))],
        out_shape=[jax.ShapeDtypeStruct((bsz * seq, d), BF16),
                   jax.ShapeDtypeStruct((bsz, CONF_HALO, d), F32)],
        scratch_shapes=[pltpu.VMEM((CONF_HALO + tm, d), F32),
                        pltpu.VMEM((7, CONF_HALO - 8 + tm, d), F32)],
        compiler_params=_params("parallel", "arbitrary"),
        name="conformer_mid",
    )(x, norm_g.reshape(1, k), w1, b1.reshape(1, 2 * d), dw_w, dw_b.reshape(1, d), ln_g.reshape(1, d),
      ln_b.reshape(1, d))


FFN_HALO = 8


def _ffn_up_body(x_ref, g_ref, wg_ref, wv_ref, cwg_ref, cwv_ref, cbg_ref, cbv_ref,
                 o_ref, tg_ref, tv_ref, xn_ref, ubuf, carry):
    t = pl.program_id(1)
    j = pl.program_id(2)
    tm = x_ref.shape[0]

    @pl.when(j == 0)
    def _():
        x = x_ref[...]
        ms = jnp.mean(x * x, axis=-1, keepdims=True)
        xn_ref[...] = (x * lax.rsqrt(ms + EPS) * g_ref[...]).astype(BF16)

    @pl.when(t == 0)
    def _():
        carry[j] = jnp.zeros(carry.shape[1:], F32)

    xn = xn_ref[...]
    halves = []
    for part, (w_ref, cw_ref, cb_ref, tail_ref) in enumerate(((wg_ref, cwg_ref, cbg_ref, tg_ref),
                                                              (wv_ref, cwv_ref, cbv_ref, tv_ref))):
        u = _dot(xn, w_ref[...])
        ubuf[part, 0:FFN_HALO, :] = carry[j, part]
        ubuf[part, FFN_HALO:FFN_HALO + tm, :] = u
        cw = cw_ref[...]
        halves.append(u * cw[2:3] + ubuf[part, FFN_HALO - 1:FFN_HALO - 1 + tm, :] * cw[1:2]
                      + ubuf[part, FFN_HALO - 2:FFN_HALO - 2 + tm, :] * cw[0:1] + cb_ref[...])
        carry[j, part] = u[tm - FFN_HALO:tm]
        tail_ref[0, j] = u[tm - FFN_HALO:tm]
    o_ref[...] = (jax.nn.gelu(halves[0], approximate=True) * halves[1]).astype(BF16)


def ffn_up_call(x, g, w_up, conv_w, conv_b, bsz, seq, tm, tn):
    m, k = x.shape
    d2 = w_up.shape[1]
    dff = d2 // 2
    nt = seq // tm
    nj = dff // tn
    cb = conv_b.reshape(1, d2)
    row = lambda b, t, j: b * nt + t
    act, tail_g, tail_v = pl.pallas_call(
        _ffn_up_body,
        grid=(bsz, nt, nj),
        in_specs=[pl.BlockSpec((tm, k), lambda b, t, j: (row(b, t, j), 0)),
                  pl.BlockSpec((1, k), lambda b, t, j: (0, 0)),
                  pl.BlockSpec((k, tn), lambda b, t, j: (0, j)),
                  pl.BlockSpec((k, tn), lambda b, t, j: (0, j + nj)),
                  pl.BlockSpec((FFN_CONV, tn), lambda b, t, j: (0, j)),
                  pl.BlockSpec((FFN_CONV, tn), lambda b, t, j: (0, j + nj)),
                  pl.BlockSpec((1, tn), lambda b, t, j: (0, j)),
                  pl.BlockSpec((1, tn), lambda b, t, j: (0, j + nj))],
        out_specs=[pl.BlockSpec((tm, tn), lambda b, t, j: (row(b, t, j), j)),
                   pl.BlockSpec((1, nj, FFN_HALO, tn), lambda b, t, j: (b, 0, 0, 0)),
                   pl.BlockSpec((1, nj, FFN_HALO, tn), lambda b, t, j: (b, 0, 0, 0))],
        out_shape=[jax.ShapeDtypeStruct((m, dff), BF16),
                   jax.ShapeDtypeStruct((bsz, nj, FFN_HALO, tn), F32),
                   jax.ShapeDtypeStruct((bsz, nj, FFN_HALO, tn), F32)],
        scratch_shapes=[pltpu.VMEM((tm, k), BF16),
                        pltpu.VMEM((2, FFN_HALO + tm, tn), F32),
                        pltpu.VMEM((nj, 2, FFN_HALO, tn), F32)],
        compiler_params=_params("parallel", "arbitrary", "arbitrary"),
        name="ffn_up",
    )(x, g.reshape(1, k), w_up, w_up, conv_w, conv_w, cb, cb)
    flat = lambda tail: tail.transpose(0, 2, 1, 3).reshape(bsz, FFN_HALO, dff)
    return act, jnp.concatenate([flat(tail_g), flat(tail_v)], axis=-1)


def _rms(x, g):
    return x * lax.rsqrt(jnp.mean(x * x, axis=-1, keepdims=True) + EPS) * g


def _sample_ssd_step(z, xbc, dt_raw, conv_hist, h0, conv_w, conv_b, dt_bias, a_log, d_skip, ssm_norm):
    bsz = z.shape[0]
    n_heads = dt_bias.shape[0]
    d_inner = n_heads * SSM_HEAD_DIM
    hpg = n_heads // SSM_GROUPS
    xh = jnp.concatenate([conv_hist, xbc[:, None, :]], axis=1)
    new_hist = xh[:, 1:]
    conv = jnp.sum(xh * conv_w[None], axis=1) + conv_b
    xc = conv * jax.nn.sigmoid(conv)
    xs = xc[:, :d_inner].reshape(bsz, SSM_GROUPS, hpg, SSM_HEAD_DIM)
    bm = xc[:, d_inner:d_inner + SSM_GROUPS * SSM_STATE].reshape(bsz, SSM_GROUPS, SSM_STATE)
    cm = xc[:, d_inner + SSM_GROUPS * SSM_STATE:].reshape(bsz, SSM_GROUPS, SSM_STATE)
    dt = jax.nn.softplus(dt_raw + dt_bias).reshape(bsz, SSM_GROUPS, hpg)
    a = (-jnp.exp(a_log)).reshape(SSM_GROUPS, hpg)
    dec = jnp.exp(dt * a)
    h0g = h0.reshape(bsz, SSM_GROUPS, hpg, SSM_HEAD_DIM, SSM_STATE)
    h_new = dec[..., None, None] * h0g + (dt[..., None] * xs)[..., None] * bm[:, :, None, None, :]
    y = jnp.sum(h_new * cm[:, :, None, None, :], axis=-1)
    y = y + d_skip.reshape(SSM_GROUPS, hpg)[None, :, :, None] * xs
    y = y.reshape(bsz, d_inner) * (z * jax.nn.sigmoid(z))
    y = _rms(y.reshape(bsz, SSM_GROUPS, d_inner // SSM_GROUPS),
             ssm_norm.reshape(SSM_GROUPS, d_inner // SSM_GROUPS)).reshape(bsz, d_inner)
    return y, new_hist, h_new.reshape(h0.shape)


def _prep_w_in(w_in, d_inner, conv_dim, n_heads):
    nq = NSA_HEADS * NSA_HEAD_DIM
    nkv = 2 * NSA_KV * NSA_HEAD_DIM
    o = np.cumsum([0, d_inner, conv_dim, n_heads, nq, nkv, nkv, nkv, 3 * NSA_HEADS])
    z, xbc, dt, q, kvc, kvs, kvw, gates = (w_in[:, o[k]:o[k + 1]] for k in range(8))
    small = jnp.concatenate([dt, gates], axis=1)
    small = jnp.pad(small, ((0, 0), (0, LANES - small.shape[1])))
    return jnp.concatenate([z, q, xbc, kvc, kvs, kvw, small], axis=1).astype(BF16)


def _pick_tile(m, pref):
    return pref if m % pref == 0 else m


def _col_tile(n, target):
    best = None
    for t in range(LANES, min(n, target) + 1, LANES):
        if n % t == 0:
            best = t
    return best or n


def kernel(x_prompt, x_sample, state_ssm, state_ssm_conv, cache_cmp, cache_slc, cache_win, state_conf_conv, state_ffn_conv, page_table, norm_g, ab_w_in, ab_conv_w, ab_conv_b, ab_dt_bias, ab_a_log, ab_d_skip, ab_ssm_norm, ab_cmp_pe, ab_cmp_w1, ab_cmp_b1, ab_cmp_w2, ab_cmp_b2, ab_w_out, c_w_pw1, c_b_pw1, c_dw_w, c_dw_b, c_ln_g, c_ln_b, c_w_pw2, c_b_pw2, ffn_w_up, ffn_conv_w, ffn_conv_b, ffn_w_down):
    bsz, seq, d_model = x_prompt.shape
    db, dseq, _ = x_sample.shape
    assert dseq == 1
    depth = norm_g.shape[0]
    n_heads = ab_dt_bias.shape[1]
    d_inner = n_heads * SSM_HEAD_DIM
    conv_dim = ab_conv_w.shape[2]
    d_ff2 = ffn_w_up.shape[2]
    d_conf = c_dw_w.shape[2] if c_dw_w.shape[0] else d_model
    past_len = page_table.shape[1] * cache_cmp.shape[2]
    mp, ms = bsz * seq, db
    hp = x_prompt.reshape(mp, d_model)
    hs = x_sample.reshape(ms, d_model)
    tm_p = _pick_tile(mp, 1024)
    tm_post = _pick_tile(mp, 512)
    tm_seq = _pick_tile(seq, 256)
    tn_ff = _col_tile(d_ff2 // 2, 1536)

    cos_p, sin_p = _rope_tables(jnp.arange(seq, dtype=jnp.int32))
    cos_s, sin_s = _rope_tables(jnp.full((ms,), past_len, jnp.int32))
    nchk = seq // CMP_STRIDE
    n_cmp = nchk - CMP_LEN // CMP_STRIDE + 1

    outs = {k: [] for k in ("ssm_p", "ssm_s", "sconv_p", "sconv_s", "cmp_p", "cmp_s", "slc_p", "slc_s",
                            "win_p", "win_s", "conf_p", "conf_s", "ffn_p", "ffn_s")}
    kvshape = (2, NSA_KV, NSA_HEAD_DIM)
    for layer in range(depth):
        g = norm_g[layer]
        if layer % 2 == 0:
            a = layer // 2
            w_in = _prep_w_in(ab_w_in[a], d_inner, conv_dim, n_heads)
            n_proj = w_in.shape[1]
            zero_b = jnp.zeros((n_proj,), F32)
            w_out = ab_w_out[a].astype(BF16)
            zero_d = jnp.zeros((d_model,), F32)
            tn_in = _col_tile(n_proj, 1536)
            proj = norm_matmul(hp, g[0], w_in, zero_b, tm_p, tn_in)
            q_rot, q_swp, kv_pad, kvc, kvs, kvw = rope_call(proj, cos_p, sin_p, _pick_tile(seq, 512))
            ch = kvc.reshape(bsz, nchk, CMP_STRIDE, 4, NSA_HEAD_DIM).transpose(0, 3, 1, 2, 4)
            ch = ch.reshape(bsz, 4, nchk, CMP_STRIDE * NSA_HEAD_DIM)
            pe_flat = ab_cmp_pe[a].reshape(2, 1, CMP_LEN * NSA_HEAD_DIM)
            w2p = jnp.pad(ab_cmp_w2[a], ((0, 0), (0, 0), (0, LANES - NSA_HEAD_DIM))).astype(BF16)
            b2p = jnp.pad(ab_cmp_b2[a], ((0, 0), (0, LANES - NSA_HEAD_DIM))).reshape(2, 1, LANES)
            ckv = compress_call(ch, pe_flat, ab_cmp_w1[a].astype(BF16), ab_cmp_b1[a].reshape(2, 1, -1), w2p, b2p)
            y_nsa = nsa_prompt_call(q_rot, q_swp, kv_pad, ckv, proj, bsz, seq, n_cmp, n_heads)
            y_ssm, st = ssd_call(proj, ab_conv_w[a], ab_conv_b[a], ab_dt_bias[a], ab_a_log[a],
                                 ab_d_skip[a], ab_ssm_norm[a], bsz, seq)
            hp = matmul_post([y_ssm, y_nsa], w_out, zero_d, hp, g[1], tm_post)
            hpg = n_heads // SSM_GROUPS
            outs["ssm_p"].append(st.reshape(bsz, SSM_GROUPS, SSM_STATE, hpg, SSM_HEAD_DIM)
                                 .transpose(0, 1, 3, 4, 2).reshape(bsz, n_heads, SSM_HEAD_DIM, SSM_STATE))
            tail_p = proj.reshape(bsz, seq, n_proj)[:, seq - (SSM_CONV - 1):]
            outs["sconv_p"].append(tail_p[:, :, d_inner + 1024:d_inner + 1024 + conv_dim])
            outs["cmp_p"].append(kvc.reshape((bsz, seq) + kvshape))
            outs["slc_p"].append(kvs.reshape((bsz, seq) + kvshape))
            outs["win_p"].append(kvw.reshape((bsz, seq) + kvshape)[:, seq - min(WINDOW, seq):])
            proj_s = norm_matmul(hs, g[0], w_in, zero_b, ms, tn_in)
            q_s, _, _, kvc_s, kvs_s, kvw_s = rope_call(proj_s, cos_s, sin_s, ms)
            z_s = proj_s[:, :d_inner]
            xbc_s = proj_s[:, d_inner + 1024:d_inner + 1024 + conv_dim]
            dt_s = proj_s[:, n_proj - LANES:n_proj - LANES + n_heads]
            gates_s = proj_s[:, n_proj - LANES + n_heads:n_proj - LANES + n_heads + 3 * NSA_HEADS]
            ys_ssm, hist_s, h_s = _sample_ssd_step(z_s, xbc_s, dt_s, state_ssm_conv[a], state_ssm[a],
                                                   ab_conv_w[a], ab_conv_b[a], ab_dt_bias[a], ab_a_log[a],
                                                   ab_d_skip[a], ab_ssm_norm[a])
            ys_nsa = sample_nsa_call(page_table, cache_cmp[a], cache_slc[a], cache_win[a], q_s, kvs_s, kvw_s,
                                     gates_s, ab_cmp_pe[a], ab_cmp_w1[a], ab_cmp_b1[a], ab_cmp_w2[a], ab_cmp_b2[a])
            kvc_s = kvc_s.reshape((ms, 1) + kvshape)
            kvs_s = kvs_s.reshape((ms, 1) + kvshape)
            kvw_s = kvw_s.reshape((ms, 1) + kvshape)
            hs = matmul_post([ys_ssm.astype(BF16), ys_nsa.astype(BF16)], w_out, zero_d, hs, g[1], ms)
            outs["ssm_s"].append(h_s)
            outs["sconv_s"].append(hist_s)
            outs["cmp_s"].append(kvc_s)
            outs["slc_s"].append(kvs_s)
            outs["win_s"].append(kvw_s)
        else:
            c = layer // 2
            w1 = c_w_pw1[c].astype(BF16)
            w2 = c_w_pw2[c].astype(BF16)
            act, tail = conformer_mid_call(hp, g[0], w1, c_b_pw1[c], c_dw_w[c], c_dw_b[c], c_ln_g[c], c_ln_b[c],
                                           bsz, seq, tm_seq)
            hp = matmul_post([act], w2, c_b_pw2[c], hp, g[1], tm_post)
            outs["conf_p"].append(tail[:, CONF_HALO - (CONF_CONV - 1):])
            u_s = norm_matmul(hs, g[0], w1, c_b_pw1[c], ms, 512)
            a_s = u_s[:, :d_conf] * jax.nn.sigmoid(u_s[:, d_conf:])
            ah = jnp.concatenate([state_conf_conv[c], a_s[:, None, :]], axis=1)
            conv = jnp.sum(ah * c_dw_w[c][None], axis=1) + c_dw_b[c]
            mu = jnp.mean(conv, axis=-1, keepdims=True)
            var = jnp.mean(jnp.square(conv - mu), axis=-1, keepdims=True)
            yln = (conv - mu) * lax.rsqrt(var + EPS) * c_ln_g[c] + c_ln_b[c]
            hs = matmul_post([(yln * jax.nn.sigmoid(yln)).astype(BF16)], w2, c_b_pw2[c], hs, g[1], ms)
            outs["conf_s"].append(ah[:, 1:])
        w_up = ffn_w_up[layer].astype(BF16)
        w_dn = ffn_w_down[layer].astype(BF16)
        zero_up = jnp.zeros((d_ff2,), F32)
        zero_d = jnp.zeros((d_model,), F32)
        act, tail_u = ffn_up_call(hp, g[2], w_up, ffn_conv_w[layer], ffn_conv_b[layer], bsz, seq,
                                  _pick_tile(seq, 512), tn_ff)
        hp = matmul_post([act], w_dn, zero_d, hp, g[3], tm_post)
        outs["ffn_p"].append(tail_u[:, FFN_HALO - (FFN_CONV - 1):])
        u_s = norm_matmul(hs, g[2], w_up, zero_up, ms, tn_ff)
        uh = jnp.concatenate([state_ffn_conv[layer], u_s[:, None, :]], axis=1)
        y_s = jnp.sum(uh * ffn_conv_w[layer][None], axis=1) + ffn_conv_b[layer]
        act_s = jax.nn.gelu(y_s[:, :d_ff2 // 2], approximate=True) * y_s[:, d_ff2 // 2:]
        hs = matmul_post([act_s.astype(BF16)], w_dn, zero_d, hs, g[3], ms)
        outs["ffn_s"].append(uh[:, 1:])

    st = lambda k: jnp.stack(outs[k])
    return (hp.reshape(bsz, seq, d_model), hs.reshape(db, dseq, d_model),
            st("ssm_p"), st("ssm_s"), st("sconv_p"), st("sconv_s"), st("cmp_p"), st("cmp_s"),
            st("slc_p"), st("slc_s"), st("win_p"), st("win_s"), st("conf_p"), st("conf_s"),
            st("ffn_p"), st("ffn_s"))
```

```python
import functools
import math

import jax
import jax.numpy as jnp
import numpy as np
from jax import lax
from jax.experimental import pallas as pl
from jax.experimental.pallas import tpu as pltpu

F32 = jnp.float32
BF16 = jnp.bfloat16

EPS = 1e-6
LANES = 128
VMEM_LIMIT = 48 * 1024 * 1024

SSM_HEAD_DIM = 64
SSM_GROUPS = 4
SSM_STATE = 128
SSM_CONV = 4
SSM_CHUNK = 128
NSA_HEADS = 16
NSA_KV = 2
NSA_HPG = NSA_HEADS // NSA_KV
NSA_HEAD_DIM = 64
CMP_LEN = 32
CMP_STRIDE = 16
SLC_LEN = 64
SLC_TOPN = 16
WINDOW = 512
Q_BLOCK = 128
ROPE_THETA = 10000.0
FORCE_SCORE = 1e4
QK_SCALE_LOG2 = NSA_HEAD_DIM ** -0.5 * math.log2(math.e)
CONF_CONV = 31
FFN_CONV = 3


def _params(*sem):
    return pltpu.CompilerParams(dimension_semantics=sem, vmem_limit_bytes=VMEM_LIMIT)


def _split3(v):
    hi = v.astype(BF16)
    r1 = v - hi.astype(F32)
    mid = r1.astype(BF16)
    lo = (r1 - mid.astype(F32)).astype(BF16)
    return hi, mid, lo


def _dot(a, b):
    return jnp.dot(a, b, preferred_element_type=F32)


def _dot_nt(a, b):
    return lax.dot_general(a, b, (((1,), (1,)), ((), ())), preferred_element_type=F32)


def _dot3(v, m_bf16):
    hi, mid, lo = _split3(v)
    return _dot(hi, m_bf16) + _dot(mid, m_bf16) + _dot(lo, m_bf16)


def _dot3_left(m_bf16, v):
    hi, mid, lo = _split3(v)
    return _dot(m_bf16, hi) + _dot(m_bf16, mid) + _dot(m_bf16, lo)


def _norm_matmul_body(x_ref, g_ref, w_ref, b_ref, o_ref, xn_ref):
    @pl.when(pl.program_id(1) == 0)
    def _():
        x = x_ref[...]
        ms = jnp.mean(x * x, axis=-1, keepdims=True)
        xn_ref[...] = (x * lax.rsqrt(ms + EPS) * g_ref[...]).astype(BF16)

    o_ref[...] = _dot(xn_ref[...], w_ref[...]) + b_ref[...]


def norm_matmul(x, g, w, b, tm, tn):
    m, k = x.shape
    n = w.shape[1]
    return pl.pallas_call(
        _norm_matmul_body,
        grid=(m // tm, n // tn),
        in_specs=[pl.BlockSpec((tm, k), lambda i, j: (i, 0)),
                  pl.BlockSpec((1, k), lambda i, j: (0, 0)),
                  pl.BlockSpec((k, tn), lambda i, j: (0, j)),
                  pl.BlockSpec((1, tn), lambda i, j: (0, j))],
        out_specs=pl.BlockSpec((tm, tn), lambda i, j: (i, j)),
        out_shape=jax.ShapeDtypeStruct((m, n), F32),
        scratch_shapes=[pltpu.VMEM((tm, k), BF16)],
        compiler_params=_params("parallel", "arbitrary"),
        name="norm_matmul",
    )(x, g.reshape(1, k), w, b.reshape(1, n))


def _matmul_post_body(n_pairs, *refs):
    a_refs = refs[:n_pairs]
    w_refs = refs[n_pairs:2 * n_pairs]
    b_ref, r_ref, g_ref, o_ref = refs[2 * n_pairs:]
    acc = _dot(a_refs[0][...], w_refs[0][...])
    for a_ref, w_ref in zip(a_refs[1:], w_refs[1:]):
        acc = acc + _dot(a_ref[...], w_ref[...])
    acc = acc + b_ref[...]
    ms = jnp.mean(acc * acc, axis=-1, keepdims=True)
    o_ref[...] = r_ref[...] + acc * lax.rsqrt(ms + EPS) * g_ref[...]


def matmul_post(a_list, w, b, resid, g, tm):
    m, n = resid.shape
    n_pairs = len(a_list)
    in_specs = [pl.BlockSpec((tm, a.shape[1]), lambda i: (i, 0)) for a in a_list]
    offset = 0
    for a in a_list:
        k = a.shape[1]
        assert offset % k == 0
        in_specs.append(pl.BlockSpec((k, n), functools.partial(lambda i, blk: (blk, 0), blk=offset // k)))
        offset += k
    assert offset == w.shape[0]
    w_list = [w] * n_pairs
    in_specs += [pl.BlockSpec((1, n), lambda i: (0, 0)),
                 pl.BlockSpec((tm, n), lambda i: (i, 0)),
                 pl.BlockSpec((1, n), lambda i: (0, 0))]
    return pl.pallas_call(
        functools.partial(_matmul_post_body, n_pairs),
        grid=(m // tm,),
        in_specs=in_specs,
        out_specs=pl.BlockSpec((tm, n), lambda i: (i, 0)),
        out_shape=jax.ShapeDtypeStruct((m, n), F32),
        compiler_params=_params("parallel"),
        name="matmul_post",
    )(*a_list, *w_list, b.reshape(1, n), resid, g.reshape(1, n))


def _rope_tables(pos):
    half = NSA_HEAD_DIM // 2
    inv = ROPE_THETA ** (-jnp.arange(half, dtype=F32) / half)
    ang = pos.astype(F32)[:, None] * inv[None, :]
    cos, sin = jnp.cos(ang), jnp.sin(ang)
    reps = LANES // NSA_HEAD_DIM
    cosf = jnp.tile(jnp.concatenate([cos, cos], axis=-1), (1, reps))
    sinf = jnp.tile(jnp.concatenate([-sin, sin], axis=-1), (1, reps))
    return cosf, sinf


def _rotate_half_partner(x):
    w = x.shape[-1]
    lane = lax.broadcasted_iota(jnp.int32, x.shape, x.ndim - 1)
    first = (lane % NSA_HEAD_DIM) < (NSA_HEAD_DIM // 2)
    return jnp.where(first, pltpu.roll(x, w - NSA_HEAD_DIM // 2, x.ndim - 1),
                     pltpu.roll(x, NSA_HEAD_DIM // 2, x.ndim - 1))


def _swap_head_pair(x):
    w = x.shape[-1]
    lane = lax.broadcasted_iota(jnp.int32, x.shape, x.ndim - 1)
    first = (lane % LANES) < NSA_HEAD_DIM
    return jnp.where(first, pltpu.roll(x, w - NSA_HEAD_DIM, x.ndim - 1),
                     pltpu.roll(x, NSA_HEAD_DIM, x.ndim - 1))


def _rope_body(q_ref, kv_ref, cos_ref, sin_ref, q_o, qs_o, kv_o, kvc_o, kvs_o, kvw_o):
    cos = cos_ref[...]
    sin = sin_ref[...]
    q = q_ref[...]
    nq = q.shape[1] // LANES
    cq = jnp.concatenate([cos] * nq, axis=1)
    sq = jnp.concatenate([sin] * nq, axis=1)
    qr = (q * cq + _rotate_half_partner(q) * sq) * QK_SCALE_LOG2
    q_o[...] = qr.astype(BF16)
    qs_o[...] = _swap_head_pair(qr).astype(BF16)

    kv = kv_ref[...]
    lane = lax.broadcasted_iota(jnp.int32, (kv.shape[0], LANES), 1)
    lo = lane < NSA_HEAD_DIM
    zero = jnp.zeros((kv.shape[0], LANES), F32)
    padded = []
    for c, full_o in enumerate((kvc_o, kvs_o, kvw_o)):
        k = kv[:, c * 256:c * 256 + LANES]
        v = kv[:, c * 256 + LANES:(c + 1) * 256]
        kr = k * cos + _rotate_half_partner(k) * sin
        full_o[...] = jnp.concatenate([kr, v], axis=1)
        if c > 0:
            kr_sw = pltpu.roll(kr, NSA_HEAD_DIM, 1)
            v_sw = pltpu.roll(v, NSA_HEAD_DIM, 1)
            padded += [jnp.where(lo, kr, zero), jnp.where(lo, kr_sw, zero),
                       jnp.where(lo, v, zero), jnp.where(lo, v_sw, zero)]
    kv_o[...] = jnp.concatenate(padded, axis=1).astype(BF16)


def rope_call(proj, cosf, sinf, tm):
    m = proj.shape[0]
    nblk = cosf.shape[0] // tm
    f32s = jax.ShapeDtypeStruct((m, 256), F32)
    return pl.pallas_call(
        _rope_body,
        grid=(m // tm,),
        in_specs=[pl.BlockSpec((tm, 1024), lambda i: (i, 2)),
                  pl.BlockSpec((tm, 768), lambda i: (i, 8)),
                  pl.BlockSpec((tm, LANES), lambda i: (i % nblk, 0)),
                  pl.BlockSpec((tm, LANES), lambda i: (i % nblk, 0))],
        out_specs=[pl.BlockSpec((tm, 1024), lambda i: (i, 0)),
                   pl.BlockSpec((tm, 1024), lambda i: (i, 0)),
                   pl.BlockSpec((tm, 1024), lambda i: (i, 0)),
                   pl.BlockSpec((tm, 256), lambda i: (i, 0)),
                   pl.BlockSpec((tm, 256), lambda i: (i, 0)),
                   pl.BlockSpec((tm, 256), lambda i: (i, 0))],
        out_shape=[jax.ShapeDtypeStruct((m, 1024), BF16), jax.ShapeDtypeStruct((m, 1024), BF16),
                   jax.ShapeDtypeStruct((m, 1024), BF16), f32s, f32s, f32s],
        compiler_params=_params("parallel"),
        name="rope",
    )(proj, proj, cosf, sinf)


def _compress_body(ch_ref, pe_ref, w1_ref, b1_ref, w2_ref, b2_ref, o_ref):
    ch = ch_ref[0, 0]
    nc, half = ch.shape
    pe = pe_ref[0]
    top = _dot((ch + pe[:, :half]).astype(BF16), w1_ref[0, :half, :])
    bot = _dot((ch + pe[:, half:]).astype(BF16), w1_ref[0, half:, :])
    hid = top + pltpu.roll(bot, nc - 1, 0) + b1_ref[0]
    hid = hid * jax.nn.sigmoid(hid)
    o_ref[0, 0] = _dot(hid.astype(BF16), w2_ref[0]) + b2_ref[0]


def compress_call(ch, pe_flat, w1, b1, w2p, b2p):
    bsz, _, nc, half = ch.shape
    hidden = w1.shape[-1]
    return pl.pallas_call(
        _compress_body,
        grid=(bsz, 4),
        in_specs=[pl.BlockSpec((1, 1, nc, half), lambda b, q: (b, q, 0, 0)),
                  pl.BlockSpec((1, 1, 2 * half), lambda b, q: (q // 2, 0, 0)),
                  pl.BlockSpec((1, 2 * half, hidden), lambda b, q: (q // 2, 0, 0)),
                  pl.BlockSpec((1, 1, hidden), lambda b, q: (q // 2, 0, 0)),
                  pl.BlockSpec((1, hidden, LANES), lambda b, q: (q // 2, 0, 0)),
                  pl.BlockSpec((1, 1, LANES), lambda b, q: (q // 2, 0, 0))],
        out_specs=pl.BlockSpec((1, 1, nc, LANES), lambda b, q: (b, q, 0, 0)),
        out_shape=jax.ShapeDtypeStruct((bsz, 4, nc, LANES), F32),
        compiler_params=_params("parallel", "parallel"),
        name="compress",
    )(ch, pe_flat, w1, b1, w2p, b2p)


def _overlap_np(nc, ns):
    i = np.arange(nc)[:, None]
    j = np.arange(ns)[None, :]
    return ((i * CMP_STRIDE < (j + 1) * SLC_LEN) & (i * CMP_STRIDE + CMP_LEN > j * SLC_LEN)).astype(np.float32)


def _nsa_t_body(seq, n_cmp, ck, wk, n_dt, q_ref, qs_ref, kvs_ref, kvw_ref, ckv_ref, small_ref, ovt_ref, o_ref):
    i = pl.program_id(1)
    tq = Q_BLOCK
    dh = NSA_HEAD_DIM
    n_slc = seq // SLC_LEN
    nsp = -(-n_slc // LANES) * LANES
    nc = ckv_ref.shape[2]
    npair = NSA_HPG // 2
    rows = NSA_HPG * tq
    neg = -jnp.inf
    tpos_t = i * tq + lax.broadcasted_iota(jnp.int32, (1, tq), 1)
    q = q_ref[...]
    qs = qs_ref[...]
    gates_t = jax.nn.sigmoid(small_ref[...]).T

    def mask_heads(s, mask):
        return jnp.concatenate([jnp.where(mask, s[:, h * tq:(h + 1) * tq], neg) for h in range(NSA_HPG)], axis=1)

    def values_t(v):
        return v.astype(F32).T[0:dh].astype(BF16)

    def gated_pair_tiles(o_t, g, branch):
        tiles = []
        for p in range(npair):
            c_even = n_dt + 3 * (g * NSA_HPG + 2 * p) + branch
            even = o_t[:, p * tq:(p + 1) * tq] * gates_t[c_even:c_even + 1, :]
            odd = o_t[:, (npair + p) * tq:(npair + p + 1) * tq] * gates_t[c_even + 3:c_even + 4, :]
            tiles.append(jnp.concatenate([even, odd], axis=0))
        return tiles

    out_tiles, qgs, sels = [], [], []
    for g in range(NSA_KV):
        qg = jnp.concatenate([q[:, (g * npair + p) * LANES:(g * npair + p + 1) * LANES] for p in range(npair)]
                             + [qs[:, (g * npair + p) * LANES:(g * npair + p + 1) * LANES] for p in range(npair)],
                             axis=0)
        s_c = _dot_nt(ckv_ref[0, g].astype(BF16), qg)
        cidx = lax.broadcasted_iota(jnp.int32, (nc, 1), 0)
        s_c = mask_heads(s_c, (cidx * CMP_STRIDE + (CMP_LEN - 1) <= tpos_t) & (cidx < n_cmp))
        mx = jnp.max(s_c, axis=0, keepdims=True)
        e = jnp.exp2(s_c - jnp.where(mx == neg, 0.0, mx))
        d = jnp.sum(e, axis=0, keepdims=True)
        p_c = e / jnp.where(d > 0, d, 1.0)
        vc_t = ckv_ref[0, NSA_KV + g].T[0:dh].astype(BF16)
        tiles = gated_pair_tiles(_dot(vc_t, p_c.astype(BF16)), g, 0)
        p_sum = p_c[:, 0:tq]
        for h in range(1, NSA_HPG):
            p_sum = p_sum + p_c[:, h * tq:(h + 1) * tq]
        ovt = ovt_ref[...]
        imp_t = sum(_dot(ovt, part) for part in _split3(p_sum))
        jidx = lax.broadcasted_iota(jnp.int32, (n_slc, 1), 0)
        qblk = tpos_t // SLC_LEN
        valid = jidx * SLC_LEN <= tpos_t
        forced = (jidx == 0) | (jidx == qblk) | (jidx == qblk - 1)
        score = jnp.where(valid, imp_t + FORCE_SCORE * forced.astype(F32), neg)
        rank = jnp.zeros((n_slc, tq), F32)
        for k in range(n_slc):
            row = score[k:k + 1, :]
            ahead = (row > score) | ((row == score) & (jidx > k))
            rank = rank + ahead.astype(F32)
        sel_t = (rank < float(min(SLC_TOPN, n_slc))).astype(BF16)
        if nsp > n_slc:
            sel_t = jnp.concatenate([sel_t, jnp.zeros((nsp - n_slc, tq), BF16)], axis=0)
        qgs.append(qg)
        sels.append(sel_t)

        w_start = jnp.clip((i - WINDOW // tq) * tq, 0, seq - wk)
        w_start = pl.multiple_of(w_start, tq)
        k_w = kvw_ref[pl.ds(w_start, wk), g * LANES:(g + 1) * LANES]
        v_w = kvw_ref[pl.ds(w_start, wk), (NSA_KV + g) * LANES:(NSA_KV + g + 1) * LANES]
        dpos = tpos_t - (w_start + lax.broadcasted_iota(jnp.int32, (wk, 1), 0))
        s_w = mask_heads(_dot_nt(k_w, qg), (dpos >= 0) & (dpos < WINDOW))
        e_w = jnp.exp2(s_w - jnp.max(s_w, axis=0, keepdims=True))
        o_w = _dot(values_t(v_w), e_w.astype(BF16)) / jnp.sum(e_w, axis=0, keepdims=True)
        out_tiles.append([t + u for t, u in zip(tiles, gated_pair_tiles(o_w, g, 2))])

    def slc_step(c, carry):
        start = pl.multiple_of(c * ck, ck)
        blk = c * (ck // SLC_LEN) + lax.broadcasted_iota(jnp.int32, (ck, nsp), 0) // SLC_LEN
        expand_t = (lax.broadcasted_iota(jnp.int32, (ck, nsp), 1) == blk).astype(BF16)
        causal = start + lax.broadcasted_iota(jnp.int32, (ck, 1), 0) <= tpos_t
        new = []
        for g in range(NSA_KV):
            m_i, l_i, acc = carry[g]
            k_c = kvs_ref[pl.ds(start, ck), g * LANES:(g + 1) * LANES]
            v_c = kvs_ref[pl.ds(start, ck), (NSA_KV + g) * LANES:(NSA_KV + g + 1) * LANES]
            mb = (_dot(expand_t, sels[g]) > 0.5) & causal
            s = mask_heads(_dot_nt(k_c, qgs[g]), mb)
            m_new = jnp.maximum(m_i, jnp.max(s, axis=0, keepdims=True))
            alpha = jnp.exp2(m_i - m_new)
            p = jnp.exp2(s - m_new)
            l_new = alpha * l_i + jnp.sum(p, axis=0, keepdims=True)
            new.append((m_new, l_new, alpha * acc + _dot(values_t(v_c), p.astype(BF16))))
        return tuple(new)

    n_chunks = (i * tq + tq - 1) // ck + 1
    init = (jnp.full((1, rows), neg, F32), jnp.zeros((1, rows), F32), jnp.zeros((dh, rows), F32))
    final = lax.fori_loop(0, n_chunks, slc_step, (init,) * NSA_KV)
    tiles_all = []
    for g in range(NSA_KV):
        _, l_s, acc_s = final[g]
        tiles_all += [(t + u).T for t, u in zip(out_tiles[g], gated_pair_tiles(acc_s / l_s, g, 1))]
    o_ref[...] = jnp.concatenate(tiles_all, axis=1).astype(BF16)


def nsa_prompt_call(q_rot, q_swp, kv_pad, ckv, proj, bsz, seq, n_cmp, n_dt):
    nq = seq // Q_BLOCK
    nc = ckv.shape[2]
    n_slc = seq // SLC_LEN
    ck = min(512, seq)
    wk = min(WINDOW + Q_BLOCK, seq)
    ovt = jnp.asarray(np.pad(_overlap_np(n_cmp, n_slc), ((0, nc - n_cmp), (0, 0))).T, BF16)
    small_blk = proj.shape[1] // LANES - 1
    return pl.pallas_call(
        functools.partial(_nsa_t_body, seq, n_cmp, ck, wk, n_dt),
        grid=(bsz, nq),
        in_specs=[pl.BlockSpec((Q_BLOCK, 1024), lambda b, i: (b * nq + i, 0)),
                  pl.BlockSpec((Q_BLOCK, 1024), lambda b, i: (b * nq + i, 0)),
                  pl.BlockSpec((seq, 512), lambda b, i: (b, 0)),
                  pl.BlockSpec((seq, 512), lambda b, i: (b, 1)),
                  pl.BlockSpec((1, 4, nc, LANES), lambda b, i: (b, 0, 0, 0)),
                  pl.BlockSpec((Q_BLOCK, LANES), lambda b, i: (b * nq + i, small_blk)),
                  pl.BlockSpec((n_slc, nc), lambda b, i: (0, 0))],
        out_specs=pl.BlockSpec((Q_BLOCK, 1024), lambda b, i: (b * nq + i, 0)),
        out_shape=jax.ShapeDtypeStruct((bsz * seq, 1024), BF16),
        compiler_params=_params("parallel", "arbitrary"),
        name="nsa_prompt",
    )(q_rot, q_swp, kv_pad, kv_pad, ckv, proj, ovt)


def _page_copy(cache_hbm, pt_ref, buf, sem, b, p, half, slot):
    src = cache_hbm.at[pt_ref[b, p], :, pl.ds(half * LANES, LANES)]
    return pltpu.make_async_copy(src, buf.at[slot, p, half], sem.at[slot])


def _sample_nsa_body(n_pages, n_cmp, qpos, w_rows,
                     pt_ref, ccmp_hbm, cslc_hbm, q16_ref, qpad_ref, new_ref, win_ref, gates_ref,
                     pet_ref, w1t_ref, b1_ref, w2t_ref, b2_ref, ov_ref,
                     y_ref, buf, sem, sel_scr, oc_scr):
    s = pl.program_id(0)
    nb = pl.num_programs(0) // 2
    b = s // 2
    page = buf.shape[3]
    n_rows = n_pages * page
    nch = n_rows // CMP_STRIDE
    nsp = ov_ref.shape[1]
    heads = NSA_HEADS
    row_g0 = lax.broadcasted_iota(jnp.int32, (heads, 1), 0) < NSA_HPG

    def fetch(cache_hbm, bb, slot):
        for p in range(n_pages):
            for half in range(2):
                _page_copy(cache_hbm, pt_ref, buf, sem, bb, p, half, slot).start()

    def wait(cache_hbm, bb, slot):
        for p in range(n_pages):
            for half in range(2):
                _page_copy(cache_hbm, pt_ref, buf, sem, bb, p, half, slot).wait()

    @pl.when(s == 0)
    def _():
        fetch(ccmp_hbm, 0, 0)

    @pl.when(s % 2 == 0)
    def _():
        fetch(cslc_hbm, b, 1)
        wait(ccmp_hbm, b, 0)
        tops = [[] for _ in range(4)]
        bots = [[] for _ in range(4)]
        for r in range(CMP_STRIDE):
            for kind in range(2):
                x_r = buf[pl.ds(0, 1), :, pl.ds(kind, 1), pl.ds(r, page // CMP_STRIDE, stride=CMP_STRIDE), :]
                xt = x_r.reshape(nch, LANES).T
                pe_rows = pl.ds(kind * LANES, LANES)
                top = (xt + pet_ref[pe_rows, r:r + 1]).astype(BF16)
                bot = (xt + pet_ref[pe_rows, CMP_STRIDE + r:CMP_STRIDE + r + 1]).astype(BF16)
                for g in range(NSA_KV):
                    tops[kind * NSA_KV + g].append(top[g * NSA_HEAD_DIM:(g + 1) * NSA_HEAD_DIM])
                    bots[kind * NSA_KV + g].append(bot[g * NSA_HEAD_DIM:(g + 1) * NSA_HEAD_DIM])
        half_w = CMP_STRIDE * NSA_HEAD_DIM
        ckv_t = []
        for qq in range(4):
            kind = qq // NSA_KV
            w1t = w1t_ref[kind]
            hid = (_dot(w1t[:, :half_w], jnp.concatenate(tops[qq], axis=0))
                   + pltpu.roll(_dot(w1t[:, half_w:], jnp.concatenate(bots[qq], axis=0)), nch - 1, 1)
                   + b1_ref[kind])
            hid = hid * jax.nn.sigmoid(hid)
            ckv_t.append((_dot(w2t_ref[kind], hid.astype(BF16)) + b2_ref[kind]).astype(BF16))
        q16 = q16_ref[0]
        cidx = lax.broadcasted_iota(jnp.int32, (1, nch), 1)
        m_c = (cidx * CMP_STRIDE + (CMP_LEN - 1) <= qpos) & (cidx < n_cmp)
        oc = []
        for g in range(NSA_KV):
            s_c = jnp.where(m_c, _dot(q16, ckv_t[g]), -jnp.inf)
            mx = jnp.max(s_c, axis=-1, keepdims=True)
            mx = jnp.where(mx == -jnp.inf, 0.0, mx)
            e = jnp.exp2(s_c - mx)
            d = jnp.sum(e, axis=-1, keepdims=True)
            p_c = e / jnp.where(d > 0, d, 1.0)
            oc.append(_dot_nt(p_c.astype(BF16), ckv_t[NSA_KV + g]))
            in_group = row_g0 if g == 0 else jnp.logical_not(row_g0)
            p_sum = jnp.sum(jnp.where(in_group, p_c, 0.0), axis=0, keepdims=True)
            imp = _dot3(jnp.broadcast_to(p_sum, (8, nch)), ov_ref[...])[0:1]
            jidx = lax.broadcasted_iota(jnp.int32, (1, nsp), 1)
            qblk = qpos // SLC_LEN
            forced = (jidx == 0) | (jidx == qblk) | (jidx == qblk - 1)
            score = jnp.where(jidx * SLC_LEN <= qpos, imp + FORCE_SCORE * forced.astype(F32), -jnp.inf)
            s_j = jnp.broadcast_to(score, (nsp, nsp))
            s_k = s_j.T
            kk = lax.broadcasted_iota(jnp.int32, (nsp, nsp), 0)
            jj = lax.broadcasted_iota(jnp.int32, (nsp, nsp), 1)
            ahead = (s_k > s_j) | ((s_k == s_j) & (kk < jj))
            rank = jnp.sum(ahead.astype(F32), axis=0, keepdims=True)
            sel_scr[g:g + 1, :] = (rank < float(SLC_TOPN)).astype(F32)
        oc_scr[...] = jnp.where(row_g0, oc[0], oc[1])

    @pl.when(s % 2 == 1)
    def _():
        @pl.when(b + 1 < nb)
        def _():
            fetch(ccmp_hbm, b + 1, 0)

        wait(cslc_hbm, b, 1)
        qpad = qpad_ref[0]
        sel16 = jnp.where(row_g0, jnp.broadcast_to(sel_scr[0:1, :], (heads, nsp)),
                          jnp.broadcast_to(sel_scr[1:2, :], (heads, nsp)))
        sel16_bf = sel16.astype(BF16)
        ppc = 8
        ck = ppc * page
        n_ck = n_rows // ck
        vals, scores = [], []
        for c in range(n_ck):
            k_c = buf[1, c * ppc:(c + 1) * ppc, 0].reshape(ck, LANES).astype(BF16)
            vals.append(buf[1, c * ppc:(c + 1) * ppc, 1].reshape(ck, LANES).astype(BF16))
            blk = c * (ck // SLC_LEN) + lax.broadcasted_iota(jnp.int32, (nsp, ck), 1) // SLC_LEN
            expand = (lax.broadcasted_iota(jnp.int32, (nsp, ck), 0) == blk).astype(BF16)
            kpos = c * ck + lax.broadcasted_iota(jnp.int32, (1, ck), 1)
            mask = (_dot(sel16_bf, expand) > 0.5) & (kpos <= qpos)
            scores.append(jnp.where(mask, _dot_nt(qpad, k_c), -jnp.inf))
        new = new_ref[0]
        ks_new = new[:, 0:LANES].astype(BF16)
        vs_new = new[:, LANES:2 * LANES].astype(BF16)
        kw_new = new[:, 2 * LANES:3 * LANES].astype(BF16)
        vw_new = new[:, 3 * LANES:].astype(BF16)
        first = lax.broadcasted_iota(jnp.int32, (1, new.shape[0]), 1) == 0
        nblk = n_rows // SLC_LEN
        s_new = jnp.where(first & (sel16[:, nblk:nblk + 1] > 0.5), _dot_nt(qpad, ks_new), -jnp.inf)
        mx = jnp.max(s_new, axis=-1, keepdims=True)
        for sc in scores:
            mx = jnp.maximum(mx, jnp.max(sc, axis=-1, keepdims=True))
        e_new = jnp.exp2(s_new - mx)
        den = jnp.sum(e_new, axis=-1, keepdims=True)
        acc = _dot(e_new.astype(BF16), vs_new)
        for sc, v_c in zip(scores, vals):
            e = jnp.exp2(sc - mx)
            den = den + jnp.sum(e, axis=-1, keepdims=True)
            acc = acc + _dot(e.astype(BF16), v_c)
        o_s = acc / den
        k_w = win_ref[0, :, 0:LANES].astype(BF16)
        v_w = win_ref[0, :, LANES:2 * LANES].astype(BF16)
        dpos = w_rows - lax.broadcasted_iota(jnp.int32, (1, w_rows), 1)
        m_w = (dpos >= 0) & (dpos < WINDOW) & (qpos - dpos >= 0)
        s_w = jnp.where(m_w, _dot_nt(qpad, k_w), -jnp.inf)
        s_wn = jnp.where(first, _dot_nt(qpad, kw_new), -jnp.inf)
        mxw = jnp.maximum(jnp.max(s_w, axis=-1, keepdims=True), jnp.max(s_wn, axis=-1, keepdims=True))
        e_w = jnp.exp2(s_w - mxw)
        e_wn = jnp.exp2(s_wn - mxw)
        o_w = ((_dot(e_w.astype(BF16), v_w) + _dot(e_wn.astype(BF16), vw_new))
               / (jnp.sum(e_w, axis=-1, keepdims=True) + jnp.sum(e_wn, axis=-1, keepdims=True)))

        def own_group(o):
            return jnp.where(row_g0, o, pltpu.roll(o, NSA_HEAD_DIM, 1))[:, :NSA_HEAD_DIM]

        gt = jax.nn.sigmoid(gates_ref[0])
        y_ref[0] = gt[:, 0:1] * oc_scr[...] + gt[:, 1:2] * own_group(o_s) + gt[:, 2:3] * own_group(o_w)


def sample_nsa_call(page_table, cache_cmp, cache_slc, cache_win, q_rot, kvs_new, kvw_new, gates, pe, w1, b1, w2, b2):
    db, n_pages = page_table.shape
    n_phys, page = cache_cmp.shape[:2]
    past_len = n_pages * page
    nch = past_len // CMP_STRIDE
    n_cmp = nch - CMP_LEN // CMP_STRIDE + 1
    n_slc = -(-(past_len + 1) // SLC_LEN)
    nsp = -(-n_slc // LANES) * LANES
    w_rows = cache_win.shape[1]
    lanes = 4 * NSA_HEAD_DIM
    q16 = q_rot.reshape(db, NSA_HEADS, NSA_HEAD_DIM)
    q4 = q_rot.reshape(db, NSA_KV, NSA_HPG, NSA_HEAD_DIM)
    qpad = jnp.concatenate([jnp.pad(q4[:, g], ((0, 0), (0, 0), (g * NSA_HEAD_DIM, LANES - (g + 1) * NSA_HEAD_DIM)))
                            for g in range(NSA_KV)], axis=1)
    new = jnp.pad(jnp.concatenate([kvs_new, kvw_new], axis=1)[:, None, :], ((0, 0), (0, 15), (0, 0)))
    hidden = w1.shape[-1]
    pet = jnp.repeat(pe, NSA_KV, axis=0).transpose(0, 2, 1).reshape(lanes, CMP_LEN)
    ov = jnp.asarray(np.pad(_overlap_np(n_cmp, n_slc), ((0, nch - n_cmp), (0, nsp - n_slc))), BF16)
    full = lambda shape: pl.BlockSpec(shape, lambda s, pt: (0,) * len(shape))
    per_b = lambda shape: pl.BlockSpec((1,) + shape, lambda s, pt: (s // 2,) + (0,) * len(shape))
    grid_spec = pltpu.PrefetchScalarGridSpec(
        num_scalar_prefetch=1,
        grid=(2 * db,),
        in_specs=[pl.BlockSpec(memory_space=pl.ANY), pl.BlockSpec(memory_space=pl.ANY),
                  per_b((NSA_HEADS, NSA_HEAD_DIM)), per_b((NSA_HEADS, LANES)), per_b((16, 2 * lanes)),
                  per_b((w_rows, lanes)), per_b((NSA_HEADS, 3)),
                  full((lanes, CMP_LEN)), full((2, hidden, CMP_LEN * NSA_HEAD_DIM)), full((2, hidden, 1)),
                  full((2, NSA_HEAD_DIM, hidden)), full((2, NSA_HEAD_DIM, 1)), full((nch, nsp))],
        out_specs=per_b((NSA_HEADS, NSA_HEAD_DIM)),
        scratch_shapes=[pltpu.VMEM((2, n_pages, 2, page, LANES), F32),
                        pltpu.SemaphoreType.DMA((2,)),
                        pltpu.VMEM((8, nsp), F32),
                        pltpu.VMEM((NSA_HEADS, NSA_HEAD_DIM), F32)])
    y = pl.pallas_call(
        functools.partial(_sample_nsa_body, n_pages, n_cmp, past_len, w_rows),
        grid_spec=grid_spec,
        out_shape=jax.ShapeDtypeStruct((db, NSA_HEADS, NSA_HEAD_DIM), F32),
        compiler_params=_params("arbitrary"),
        name="sample_nsa",
    )(page_table, cache_cmp.reshape(n_phys, page, lanes), cache_slc.reshape(n_phys, page, lanes),
      q16, qpad, new, cache_win.reshape(db, w_rows, lanes), gates.reshape(db, NSA_HEADS, 3),
      pet, w1.transpose(0, 2, 1).astype(BF16), b1.reshape(2, hidden, 1),
      w2.transpose(0, 2, 1).astype(BF16), b2.reshape(2, NSA_HEAD_DIM, 1), ov)
    return y.reshape(db, NSA_HEADS * NSA_HEAD_DIM)


def _softplus(x):
    return jnp.maximum(x, 0.0) + jnp.log1p(jnp.exp(-jnp.abs(x)))


def _ssd_body(z_ref, xbc_ref, small_ref, cw_ref, cb_ref, dtb_ref, dtbt_ref, alog_ref, alogt_ref,
              dskip_ref, nrm_ref, rexp_ref, ltri_ref, utri_ref, y_ref, st_ref, xbuf, ht):
    c = pl.program_id(1)
    q = SSM_CHUNK
    d_inner = z_ref.shape[1]
    n_heads = d_inner // SSM_HEAD_DIM
    gw = d_inner // SSM_GROUPS
    n = SSM_STATE

    @pl.when(c == 0)
    def _():
        xbuf[0:8, :] = jnp.zeros((8, xbuf.shape[1]), F32)
        ht[...] = jnp.zeros(ht.shape, F32)

    x = xbc_ref[...]
    xbuf[8:8 + q, :] = x
    cw = cw_ref[...]
    conv = (x * cw[3:4] + xbuf[7:7 + q, :] * cw[2:3] + xbuf[6:6 + q, :] * cw[1:2]
            + xbuf[5:5 + q, :] * cw[0:1] + cb_ref[...])
    xbuf[0:8, :] = x[q - 8:q]
    xc = conv * jax.nn.sigmoid(conv)
    xs = xc[:, :d_inner]
    bm = xc[:, d_inner:d_inner + SSM_GROUPS * n]
    cm = xc[:, d_inner + SSM_GROUPS * n:]

    small = small_ref[...]
    dt = _softplus(small[:, 0:n_heads] + dtb_ref[...])
    dtt = _softplus(small.T[0:n_heads, :] + dtbt_ref[...])
    a = -jnp.exp(alog_ref[...])
    at = -jnp.exp(alogt_ref[...])
    cum = _dot3_left(ltri_ref[...], dt * a)
    cumt = _dot3(dtt * at, utri_ref[...])
    cum_last = cum[q - 1:q, :]
    rexp = rexp_ref[...]
    expcum_f = _dot3(jnp.exp(cum), rexp)
    toend_f = _dot3(jnp.exp(cum_last - cum) * dt, rexp)
    cdec_f = _dot3(jnp.broadcast_to(jnp.exp(cum_last), (8, n_heads)), rexp)[0:1]

    causal = (lax.broadcasted_iota(jnp.int32, (q, q), 0) >= lax.broadcasted_iota(jnp.int32, (q, q), 1))
    lane_lo = lax.broadcasted_iota(jnp.int32, (q, LANES), 1) < SSM_HEAD_DIM
    xs_bf = xs.astype(BF16)
    zero_bf = jnp.zeros((q, LANES), BF16)
    hpg = n_heads // SSM_GROUPS
    y_tiles, yoff_tiles = [], []
    for g in range(SSM_GROUPS):
        bg = bm[:, g * n:(g + 1) * n]
        cg_bf = cm[:, g * n:(g + 1) * n].astype(BF16)
        cb = _dot_nt(cg_bf, bg.astype(BF16))
        h_old = ht[g]
        yoff_tiles.append(_dot(cg_bf, h_old.astype(BF16)))
        for pr in range(hpg // 2):
            h0 = g * hpg + 2 * pr
            xp = xs_bf[:, h0 * SSM_HEAD_DIM:h0 * SSM_HEAD_DIM + LANES]
            acc = None
            for par, xm in ((0, jnp.where(lane_lo, xp, zero_bf)), (1, jnp.where(lane_lo, zero_bf, xp))):
                h = h0 + par
                seg = cum[:, h:h + 1] - cumt[h:h + 1, :]
                w = cb * jnp.exp(jnp.where(causal, seg, -jnp.inf)) * dtt[h:h + 1, :]
                part = _dot(w.astype(BF16), xm)
                acc = part if acc is None else acc + part
            y_tiles.append(acc)
        xw = (xs[:, g * gw:(g + 1) * gw] * toend_f[:, g * gw:(g + 1) * gw]).astype(BF16)
        ht[g] = h_old * cdec_f[:, g * gw:(g + 1) * gw] + _dot(bg.T.astype(BF16), xw)

    y = (jnp.concatenate(y_tiles, axis=1) + jnp.concatenate(yoff_tiles, axis=1) * expcum_f
         + dskip_ref[...] * xs)
    z = z_ref[...]
    y = y * (z * jax.nn.sigmoid(z))
    outs = []
    for g in range(SSM_GROUPS):
        yg = y[:, g * gw:(g + 1) * gw]
        ms = jnp.mean(yg * yg, axis=-1, keepdims=True)
        outs.append(yg * lax.rsqrt(ms + EPS) * nrm_ref[:, g * gw:(g + 1) * gw])
    y_ref[...] = jnp.concatenate(outs, axis=1).astype(BF16)
    st_ref[0] = ht[...]


def ssd_call(proj, conv_w, conv_b, dt_bias, a_log, d_skip, ssm_norm, bsz, seq):
    q = SSM_CHUNK
    nchunk = seq // q
    n_heads = dt_bias.shape[0]
    d_inner = n_heads * SSM_HEAD_DIM
    conv_dim = conv_w.shape[1]
    gw = d_inner // SSM_GROUPS
    small_blk = proj.shape[1] // LANES - 1
    rexp = jnp.asarray(np.repeat(np.eye(n_heads, dtype=np.float32), SSM_HEAD_DIM, axis=1), BF16)
    ltri = jnp.asarray(np.tril(np.ones((q, q), np.float32)), BF16)
    utri = jnp.asarray(np.triu(np.ones((q, q), np.float32)), BF16)
    const = lambda shape: pl.BlockSpec(shape, lambda b, c: (0,) * len(shape))
    return pl.pallas_call(
        _ssd_body,
        grid=(bsz, nchunk),
        in_specs=[pl.BlockSpec((q, d_inner), lambda b, c: (b * nchunk + c, 0)),
                  pl.BlockSpec((q, conv_dim), lambda b, c: (b * nchunk + c, 1)),
                  pl.BlockSpec((q, LANES), lambda b, c: (b * nchunk + c, small_blk)),
                  const((SSM_CONV, conv_dim)), const((1, conv_dim)),
                  const((1, n_heads)), const((n_heads, 1)), const((1, n_heads)), const((n_heads, 1)),
                  const((1, d_inner)), const((1, d_inner)),
                  const((n_heads, d_inner)), const((q, q)), const((q, q))],
        out_specs=[pl.BlockSpec((q, d_inner), lambda b, c: (b * nchunk + c, 0)),
                   pl.BlockSpec((1, SSM_GROUPS, SSM_STATE, gw), lambda b, c: (b, 0, 0, 0))],
        out_shape=[jax.ShapeDtypeStruct((bsz * seq, d_inner), BF16),
                   jax.ShapeDtypeStruct((bsz, SSM_GROUPS, SSM_STATE, gw), F32)],
        scratch_shapes=[pltpu.VMEM((8 + q, conv_dim), F32),
                        pltpu.VMEM((SSM_GROUPS, SSM_STATE, gw), F32)],
        compiler_params=_params("parallel", "arbitrary"),
        name="ssd",
    )(proj, proj, proj, conv_w, conv_b.reshape(1, -1),
      dt_bias.reshape(1, -1), dt_bias.reshape(-1, 1), a_log.reshape(1, -1), a_log.reshape(-1, 1),
      jnp.repeat(d_skip, SSM_HEAD_DIM).reshape(1, -1), ssm_norm.reshape(1, -1), rexp, ltri, utri)


CONF_HALO = 32


def _conf_body(x_ref, ng_ref, w1_ref, b1_ref, w_ref, b_ref, g_ref, beta_ref, o_ref, tail_ref, buf, shifted):
    t = pl.program_id(1)
    tm, d = o_ref.shape
    sub = 8

    @pl.when(t == 0)
    def _():
        buf[0:CONF_HALO, :] = jnp.zeros((CONF_HALO, d), F32)

    x = x_ref[...]
    ms = jnp.mean(x * x, axis=-1, keepdims=True)
    xn = (x * lax.rsqrt(ms + EPS) * ng_ref[...]).astype(BF16)
    u = _dot(xn, w1_ref[...]) + b1_ref[...]
    a = u[:, :d] * jax.nn.sigmoid(u[:, d:])
    buf[CONF_HALO:CONF_HALO + tm, :] = a
    span = shifted.shape[1]
    for s in range(1, sub):
        shifted[s - 1] = buf[s:s + span, :]
    w = w_ref[...]
    off = CONF_HALO - (CONF_CONV - 1)
    acc = b_ref[...]
    for k in range(CONF_CONV):
        s, base = (off + k) % sub, (off + k) // sub * sub
        rows = buf[base:base + tm, :] if s == 0 else shifted[s - 1, base:base + tm, :]
        acc = acc + rows * w[k:k + 1]
    buf[0:CONF_HALO, :] = a[tm - CONF_HALO:tm]
    tail_ref[0] = a[tm - CONF_HALO:tm]
    mu = jnp.mean(acc, axis=-1, keepdims=True)
    cen = acc - mu
    var = jnp.mean(cen * cen, axis=-1, keepdims=True)
    y = cen * lax.rsqrt(var + EPS) * g_ref[...] + beta_ref[...]
    o_ref[...] = (y * jax.nn.sigmoid(y)).astype(BF16)


def conformer_mid_call(x, norm_g, w1, b1, dw_w, dw_b, ln_g, ln_b, bsz, seq, tm):
    d = dw_w.shape[1]
    k = x.shape[1]
    nt = seq // tm
    const = lambda shape: pl.BlockSpec(shape, lambda b, t: (0,) * len(shape))
    return pl.pallas_call(
        _conf_body,
        grid=(bsz, nt),
        in_specs=[pl.BlockSpec((tm, k), lambda b, t: (b * nt + t, 0)),
                  const((1, k)), const((k, 2 * d)), const((1, 2 * d)),
                  const((CONF_CONV, d)), const((1, d)), const((1, d)), const((1, d))],
        out_specs=[pl.BlockSpec((tm, d), lambda b, t: (b * nt + t, 0)),
                   pl.BlockSpec((1, CONF_HALO, d), lambda b, t: (b, 0, 0))],
        out_shape=[jax.ShapeDtypeStruct((bsz * seq, d), BF16),
                   jax.ShapeDtypeStruct((bsz, CONF_HALO, d), F32)],
        scratch_shapes=[pltpu.VMEM((CONF_HALO + tm, d), F32),
                        pltpu.VMEM((7, CONF_HALO - 8 + tm, d), F32)],
        compiler_params=_params("parallel", "arbitrary"),
        name="conformer_mid",
    )(x, norm_g.reshape(1, k), w1, b1.reshape(1, 2 * d), dw_w, dw_b.reshape(1, d), ln_g.reshape(1, d),
      ln_b.reshape(1, d))


FFN_HALO = 8


def _ffn_up_body(x_ref, g_ref, wg_ref, wv_ref, cwg_ref, cwv_ref, cbg_ref, cbv_ref,
                 o_ref, tg_ref, tv_ref, xn_ref, ubuf, carry):
    t = pl.program_id(1)
    j = pl.program_id(2)
    tm = x_ref.shape[0]

    @pl.when(j == 0)
    def _():
        x = x_ref[...]
        ms = jnp.mean(x * x, axis=-1, keepdims=True)
        xn_ref[...] = (x * lax.rsqrt(ms + EPS) * g_ref[...]).astype(BF16)

    @pl.when(t == 0)
    def _():
        carry[j] = jnp.zeros(carry.shape[1:], F32)

    xn = xn_ref[...]
    halves = []
    for part, (w_ref, cw_ref, cb_ref, tail_ref) in enumerate(((wg_ref, cwg_ref, cbg_ref, tg_ref),
                                                              (wv_ref, cwv_ref, cbv_ref, tv_ref))):
        u = _dot(xn, w_ref[...])
        ubuf[part, 0:FFN_HALO, :] = carry[j, part]
        ubuf[part, FFN_HALO:FFN_HALO + tm, :] = u
        cw = cw_ref[...]
        halves.append(u * cw[2:3] + ubuf[part, FFN_HALO - 1:FFN_HALO - 1 + tm, :] * cw[1:2]
                      + ubuf[part, FFN_HALO - 2:FFN_HALO - 2 + tm, :] * cw[0:1] + cb_ref[...])
        carry[j, part] = u[tm - FFN_HALO:tm]
        tail_ref[0, j] = u[tm - FFN_HALO:tm]
    o_ref[...] = (jax.nn.gelu(halves[0], approximate=True) * halves[1]).astype(BF16)


def ffn_up_call(x, g, w_up, conv_w, conv_b, bsz, seq, tm, tn):
    m, k = x.shape
    d2 = w_up.shape[1]
    dff = d2 // 2
    nt = seq // tm
    nj = dff // tn
    cb = conv_b.reshape(1, d2)
    row = lambda b, t, j: b * nt + t
    act, tail_g, tail_v = pl.pallas_call(
        _ffn_up_body,
        grid=(bsz, nt, nj),
        in_specs=[pl.BlockSpec((tm, k), lambda b, t, j: (row(b, t, j), 0)),
                  pl.BlockSpec((1, k), lambda b, t, j: (0, 0)),
                  pl.BlockSpec((k, tn), lambda b, t, j: (0, j)),
                  pl.BlockSpec((k, tn), lambda b, t, j: (0, j + nj)),
                  pl.BlockSpec((FFN_CONV, tn), lambda b, t, j: (0, j)),
                  pl.BlockSpec((FFN_CONV, tn), lambda b, t, j: (0, j + nj)),
                  pl.BlockSpec((1, tn), lambda b, t, j: (0, j)),
                  pl.BlockSpec((1, tn), lambda b, t, j: (0, j + nj))],
        out_specs=[pl.BlockSpec((tm, tn), lambda b, t, j: (row(b, t, j), j)),
                   pl.BlockSpec((1, nj, FFN_HALO, tn), lambda b, t, j: (b, 0, 0, 0)),
                   pl.BlockSpec((1, nj, FFN_HALO, tn), lambda b, t, j: (b, 0, 0, 0))],
        out_shape=[jax.ShapeDtypeStruct((m, dff), BF16),
                   jax.ShapeDtypeStruct((bsz, nj, FFN_HALO, tn), F32),
                   jax.ShapeDtypeStruct((bsz, nj, FFN_HALO, tn), F32)],
        scratch_shapes=[pltpu.VMEM((tm, k), BF16),
                        pltpu.VMEM((2, FFN_HALO + tm, tn), F32),
                        pltpu.VMEM((nj, 2, FFN_HALO, tn), F32)],
        compiler_params=_params("parallel", "arbitrary", "arbitrary"),
        name="ffn_up",
    )(x, g.reshape(1, k), w_up, w_up, conv_w, conv_w, cb, cb)
    flat = lambda tail: tail.transpose(0, 2, 1, 3).reshape(bsz, FFN_HALO, dff)
    return act, jnp.concatenate([flat(tail_g), flat(tail_v)], axis=-1)


def _rms(x, g):
    return x * lax.rsqrt(jnp.mean(x * x, axis=-1, keepdims=True) + EPS) * g


def _sample_ssd_step(z, xbc, dt_raw, conv_hist, h0, conv_w, conv_b, dt_bias, a_log, d_skip, ssm_norm):
    bsz = z.shape[0]
    n_heads = dt_bias.shape[0]
    d_inner = n_heads * SSM_HEAD_DIM
    hpg = n_heads // SSM_GROUPS
    xh = jnp.concatenate([conv_hist, xbc[:, None, :]], axis=1)
    new_hist = xh[:, 1:]
    conv = jnp.sum(xh * conv_w[None], axis=1) + conv_b
    xc = conv * jax.nn.sigmoid(conv)
    xs = xc[:, :d_inner].reshape(bsz, SSM_GROUPS, hpg, SSM_HEAD_DIM)
    bm = xc[:, d_inner:d_inner + SSM_GROUPS * SSM_STATE].reshape(bsz, SSM_GROUPS, SSM_STATE)
    cm = xc[:, d_inner + SSM_GROUPS * SSM_STATE:].reshape(bsz, SSM_GROUPS, SSM_STATE)
    dt = jax.nn.softplus(dt_raw + dt_bias).reshape(bsz, SSM_GROUPS, hpg)
    a = (-jnp.exp(a_log)).reshape(SSM_GROUPS, hpg)
    dec = jnp.exp(dt * a)
    h0g = h0.reshape(bsz, SSM_GROUPS, hpg, SSM_HEAD_DIM, SSM_STATE)
    h_new = dec[..., None, None] * h0g + (dt[..., None] * xs)[..., None] * bm[:, :, None, None, :]
    y = jnp.sum(h_new * cm[:, :, None, None, :], axis=-1)
    y = y + d_skip.reshape(SSM_GROUPS, hpg)[None, :, :, None] * xs
    y = y.reshape(bsz, d_inner) * (z * jax.nn.sigmoid(z))
    y = _rms(y.reshape(bsz, SSM_GROUPS, d_inner // SSM_GROUPS),
             ssm_norm.reshape(SSM_GROUPS, d_inner // SSM_GROUPS)).reshape(bsz, d_inner)
    return y, new_hist, h_new.reshape(h0.shape)


def _prep_w_in(w_in, d_inner, conv_dim, n_heads):
    nq = NSA_HEADS * NSA_HEAD_DIM
    nkv = 2 * NSA_KV * NSA_HEAD_DIM
    o = np.cumsum([0, d_inner, conv_dim, n_heads, nq, nkv, nkv, nkv, 3 * NSA_HEADS])
    z, xbc, dt, q, kvc, kvs, kvw, gates = (w_in[:, o[k]:o[k + 1]] for k in range(8))
    small = jnp.concatenate([dt, gates], axis=1)
    small = jnp.pad(small, ((0, 0), (0, LANES - small.shape[1])))
    return jnp.concatenate([z, q, xbc, kvc, kvs, kvw, small], axis=1).astype(BF16)


def _pick_tile(m, pref):
    return pref if m % pref == 0 else m


def _col_tile(n, target):
    best = None
    for t in range(LANES, min(n, target) + 1, LANES):
        if n % t == 0:
            best = t
    return best or n


def kernel(x_prompt, x_sample, state_ssm, state_ssm_conv, cache_cmp, cache_slc, cache_win, state_conf_conv, state_ffn_conv, page_table, norm_g, ab_w_in, ab_conv_w, ab_conv_b, ab_dt_bias, ab_a_log, ab_d_skip, ab_ssm_norm, ab_cmp_pe, ab_cmp_w1, ab_cmp_b1, ab_cmp_w2, ab_cmp_b2, ab_w_out, c_w_pw1, c_b_pw1, c_dw_w, c_dw_b, c_ln_g, c_ln_b, c_w_pw2, c_b_pw2, ffn_w_up, ffn_conv_w, ffn_conv_b, ffn_w_down):
    bsz, seq, d_model = x_prompt.shape
    db, dseq, _ = x_sample.shape
    assert dseq == 1
    depth = norm_g.shape[0]
    n_heads = ab_dt_bias.shape[1]
    d_inner = n_heads * SSM_HEAD_DIM
    conv_dim = ab_conv_w.shape[2]
    d_ff2 = ffn_w_up.shape[2]
    d_conf = c_dw_w.shape[2] if c_dw_w.shape[0] else d_model
    past_len = page_table.shape[1] * cache_cmp.shape[2]
    mp, ms = bsz * seq, db
    hp = x_prompt.reshape(mp, d_model)
    hs = x_sample.reshape(ms, d_model)
    tm_p = _pick_tile(mp, 1024)
    tm_post = _pick_tile(mp, 512)
    tm_seq = _pick_tile(seq, 512)
    tn_ff = _col_tile(d_ff2 // 2, 1536)

    cos_p, sin_p = _rope_tables(jnp.arange(seq, dtype=jnp.int32))
    cos_s, sin_s = _rope_tables(jnp.full((ms,), past_len, jnp.int32))
    nchk = seq // CMP_STRIDE
    n_cmp = nchk - CMP_LEN // CMP_STRIDE + 1

    outs = {k: [] for k in ("ssm_p", "ssm_s", "sconv_p", "sconv_s", "cmp_p", "cmp_s", "slc_p", "slc_s",
                            "win_p", "win_s", "conf_p", "conf_s", "ffn_p", "ffn_s")}
    kvshape = (2, NSA_KV, NSA_HEAD_DIM)
    for layer in range(depth):
        g = norm_g[layer]
        if layer % 2 == 0:
            a = layer // 2
            w_in = _prep_w_in(ab_w_in[a], d_inner, conv_dim, n_heads)
            n_proj = w_in.shape[1]
            zero_b = jnp.zeros((n_proj,), F32)
            w_out = ab_w_out[a].astype(BF16)
            zero_d = jnp.zeros((d_model,), F32)
            tn_in = _col_tile(n_proj, 1536)
            proj = norm_matmul(hp, g[0], w_in, zero_b, tm_p, tn_in)
            q_rot, q_swp, kv_pad, kvc, kvs, kvw = rope_call(proj, cos_p, sin_p, _pick_tile(seq, 512))
            ch = kvc.reshape(bsz, nchk, CMP_STRIDE, 4, NSA_HEAD_DIM).transpose(0, 3, 1, 2, 4)
            ch = ch.reshape(bsz, 4, nchk, CMP_STRIDE * NSA_HEAD_DIM)
            pe_flat = ab_cmp_pe[a].reshape(2, 1, CMP_LEN * NSA_HEAD_DIM)
            w2p = jnp.pad(ab_cmp_w2[a], ((0, 0), (0, 0), (0, LANES - NSA_HEAD_DIM))).astype(BF16)
            b2p = jnp.pad(ab_cmp_b2[a], ((0, 0), (0, LANES - NSA_HEAD_DIM))).reshape(2, 1, LANES)
            ckv = compress_call(ch, pe_flat, ab_cmp_w1[a].astype(BF16), ab_cmp_b1[a].reshape(2, 1, -1), w2p, b2p)
            y_nsa = nsa_prompt_call(q_rot, q_swp, kv_pad, ckv, proj, bsz, seq, n_cmp, n_heads)
            y_ssm, st = ssd_call(proj, ab_conv_w[a], ab_conv_b[a], ab_dt_bias[a], ab_a_log[a],
                                 ab_d_skip[a], ab_ssm_norm[a], bsz, seq)
            hp = matmul_post([y_ssm, y_nsa], w_out, zero_d, hp, g[1], tm_post)
            hpg = n_heads // SSM_GROUPS
            outs["ssm_p"].append(st.reshape(bsz, SSM_GROUPS, SSM_STATE, hpg, SSM_HEAD_DIM)
                                 .transpose(0, 1, 3, 4, 2).reshape(bsz, n_heads, SSM_HEAD_DIM, SSM_STATE))
            tail_p = proj.reshape(bsz, seq, n_proj)[:, seq - (SSM_CONV - 1):]
            outs["sconv_p"].append(tail_p[:, :, d_inner + 1024:d_inner + 1024 + conv_dim])
            outs["cmp_p"].append(kvc.reshape((bsz, seq) + kvshape))
            outs["slc_p"].append(kvs.reshape((bsz, seq) + kvshape))
            outs["win_p"].append(kvw.reshape((bsz, seq) + kvshape)[:, seq - min(WINDOW, seq):])
            proj_s = norm_matmul(hs, g[0], w_in, zero_b, ms, tn_in)
            q_s, _, _, kvc_s, kvs_s, kvw_s = rope_call(proj_s, cos_s, sin_s, ms)
            z_s = proj_s[:, :d_inner]
            xbc_s = proj_s[:, d_inner + 1024:d_inner + 1024 + conv_dim]
            dt_s = proj_s[:, n_proj - LANES:n_proj - LANES + n_heads]
            gates_s = proj_s[:, n_proj - LANES + n_heads:n_proj - LANES + n_heads + 3 * NSA_HEADS]
            ys_ssm, hist_s, h_s = _sample_ssd_step(z_s, xbc_s, dt_s, state_ssm_conv[a], state_ssm[a],
                                                   ab_conv_w[a], ab_conv_b[a], ab_dt_bias[a], ab_a_log[a],
                                                   ab_d_skip[a], ab_ssm_norm[a])
            ys_nsa = sample_nsa_call(page_table, cache_cmp[a], cache_slc[a], cache_win[a], q_s, kvs_s, kvw_s,
                                     gates_s, ab_cmp_pe[a], ab_cmp_w1[a], ab_cmp_b1[a], ab_cmp_w2[a], ab_cmp_b2[a])
            kvc_s = kvc_s.reshape((ms, 1) + kvshape)
            kvs_s = kvs_s.reshape((ms, 1) + kvshape)
            kvw_s = kvw_s.reshape((ms, 1) + kvshape)
            hs = matmul_post([ys_ssm.astype(BF16), ys_nsa.astype(BF16)], w_out, zero_d, hs, g[1], ms)
            outs["ssm_s"].append(h_s)
            outs["sconv_s"].append(hist_s)
            outs["cmp_s"].append(kvc_s)
            outs["slc_s"].append(kvs_s)
            outs["win_s"].append(kvw_s)
        else:
            c = layer // 2
            w1 = c_w_pw1[c].astype(BF16)
            w2 = c_w_pw2[c].astype(BF16)
            act, tail = conformer_mid_call(hp, g[0], w1, c_b_pw1[c], c_dw_w[c], c_dw_b[c], c_ln_g[c], c_ln_b[c],
                                           bsz, seq, tm_seq)
            hp = matmul_post([act], w2, c_b_pw2[c], hp, g[1], tm_post)
            outs["conf_p"].append(tail[:, CONF_HALO - (CONF_CONV - 1):])
            u_s = norm_matmul(hs, g[0], w1, c_b_pw1[c], ms, 512)
            a_s = u_s[:, :d_conf] * jax.nn.sigmoid(u_s[:, d_conf:])
            ah = jnp.concatenate([state_conf_conv[c], a_s[:, None, :]], axis=1)
            conv = jnp.sum(ah * c_dw_w[c][None], axis=1) + c_dw_b[c]
            mu = jnp.mean(conv, axis=-1, keepdims=True)
            var = jnp.mean(jnp.square(conv - mu), axis=-1, keepdims=True)
            yln = (conv - mu) * lax.rsqrt(var + EPS) * c_ln_g[c] + c_ln_b[c]
            hs = matmul_post([(yln * jax.nn.sigmoid(yln)).astype(BF16)], w2, c_b_pw2[c], hs, g[1], ms)
            outs["conf_s"].append(ah[:, 1:])
        w_up = ffn_w_up[layer].astype(BF16)
        w_dn = ffn_w_down[layer].astype(BF16)
        zero_up = jnp.zeros((d_ff2,), F32)
        zero_d = jnp.zeros((d_model,), F32)
        act, tail_u = ffn_up_call(hp, g[2], w_up, ffn_conv_w[layer], ffn_conv_b[layer], bsz, seq,
                                  _pick_tile(seq, 512), tn_ff)
        hp = matmul_post([act], w_dn, zero_d, hp, g[3], tm_post)
        outs["ffn_p"].append(tail_u[:, FFN_HALO - (FFN_CONV - 1):])
        u_s = norm_matmul(hs, g[2], w_up, zero_up, ms, tn_ff)
        uh = jnp.concatenate([state_ffn_conv[layer], u_s[:, None, :]], axis=1)
        y_s = jnp.sum(uh * ffn_conv_w[layer][None], axis=1) + ffn_conv_b[layer]
        act_s = jax.nn.gelu(y_s[:, :d_ff2 // 2], approximate=True) * y_s[:, d_ff2 // 2:]
        hs = matmul_post([act_s.astype(BF16)], w_dn, zero_d, hs, g[3], ms)
        outs["ffn_s"].append(uh[:, 1:])

    st = lambda k: jnp.stack(outs[k])
    return (hp.reshape(bsz, seq, d_model), hs.reshape(db, dseq, d_model),
            st("ssm_p"), st("ssm_s"), st("sconv_p"), st("sconv_s"), st("cmp_p"), st("cmp_s"),
            st("slc_p"), st("slc_s"), st("win_p"), st("win_s"), st("conf_p"), st("conf_s"),
            st("ffn_p"), st("ffn_s"))
```
